```python
import jax, jax.numpy as jnp
from jax import lax
import numpy as np

D_MODEL = 1024
BATCH = 8
SEQ = 2048
DEPTH = 2

HEAD_DIM = 64
RWKV_HEADS = 4
RWKV_WIDTH = RWKV_HEADS * HEAD_DIM
RWKV_DECAY_RANK = 32
RWKV_AAA_RANK = 32
RWKV_GATE_RANK = 64
RWKV_GN_EPS = 64e-5
RWKV_IN = 3 * RWKV_WIDTH + RWKV_DECAY_RANK + RWKV_AAA_RANK + RWKV_GATE_RANK
ATTN_GROUPS = ((128, 1), (512, 4), (2048, 16))
ATTN_HEADS_PER_GROUP = 2
ATTN_HEADS = len(ATTN_GROUPS) * ATTN_HEADS_PER_GROUP
ATTN_WIDTH = ATTN_HEADS * HEAD_DIM
ATTN_OUT_WIDTH = ATTN_HEADS_PER_GROUP * HEAD_DIM
ATTN_BLOCK = 128
ATTN_IN = 3 * ATTN_WIDTH
ROPE_THETA = 10000.0
MLSTM_HEADS = 4
MLSTM_WIDTH = MLSTM_HEADS * HEAD_DIM
MLSTM_CONV = 4
MLSTM_CHUNK = 64
MLSTM_IN = 4 * MLSTM_WIDTH + 2 * MLSTM_HEADS
HGRN_HEADS = 4
HGRN_WIDTH = HGRN_HEADS * HEAD_DIM
HGRN_CHUNK = 64
HGRN_IN = 4 * HGRN_WIDTH
N_BRANCHES = 4
GATE_IN = N_BRANCHES * D_MODEL
N_IN = RWKV_IN + ATTN_IN + MLSTM_IN + HGRN_IN + GATE_IN
D_FF = 2816
FFN_CONV = 3
NORM_EPS = 1e-6
NEG_INF = -1e30

kernel_name = 'hybrid_rwkv7_dilattn_mlstm_hgrn2_gated'


def split_cols(z, widths):
    return jnp.split(z, np.cumsum(widths)[:-1].tolist(), axis=-1)


def rmsnorm(x, g):
    xf = x.astype(jnp.float32)
    y = xf * lax.rsqrt(jnp.mean(xf * xf, axis=-1, keepdims=True) + NORM_EPS)
    return (y * g.astype(jnp.float32)).astype(x.dtype)


def token_shift(z):
    return jnp.pad(z, ((0, 0), (1, 0), (0, 0)))[:, :-1]


def causal_dwconv(x, w, b):
    K, C = w.shape
    y = lax.conv_general_dilated(x, w.astype(x.dtype)[:, None, :], window_strides=(1,),
                                 padding=[(K - 1, 0)], dimension_numbers=('NWC', 'WIO', 'NWC'),
                                 feature_group_count=C)
    return y + b.astype(x.dtype)


def rope(x, pos):
    half = x.shape[-1] // 2
    inv_freq = ROPE_THETA ** (-jnp.arange(half, dtype=jnp.float32) / half)
    ang = pos.astype(jnp.float32)[:, None] * inv_freq[None, :]
    cos = jnp.cos(ang)[None, :, None, :].astype(x.dtype)
    sin = jnp.sin(ang)[None, :, None, :].astype(x.dtype)
    x1, x2 = x[..., :half], x[..., half:]
    return jnp.concatenate([x1 * cos - x2 * sin, x1 * sin + x2 * cos], axis=-1)


def rwkv7_scan(r, w, k, v, a, b):
    Bsz, T, H, N = r.shape

    def step(S, inp):
        rt, wt, kt, vt, at, bt = inp
        sa = jnp.einsum('bhvk,bhk->bhv', S, at)
        S = S * wt[:, :, None, :] + sa[..., None] * bt[:, :, None, :] + vt[..., None] * kt[:, :, None, :]
        return S, jnp.einsum('bhvk,bhk->bhv', S, rt)

    xs = tuple(t.transpose(1, 0, 2, 3) for t in (r, w, k, v, a, b))
    _, y = lax.scan(step, jnp.zeros((Bsz, H, N, N), jnp.float32), xs)
    return y.transpose(1, 0, 2, 3)


def rwkv7_branch(z, mu, w0, w2, a0, a2, g2, k_k, k_a, r_k, ln_g, ln_b):
    Bsz, T, _ = z.shape
    f32 = jnp.float32
    z = z + mu * (token_shift(z) - z)
    r, k, v, wl, al, gl = split_cols(z, [RWKV_WIDTH, RWKV_WIDTH, RWKV_WIDTH,
                                         RWKV_DECAY_RANK, RWKV_AAA_RANK, RWKV_GATE_RANK])
    logw = -jax.nn.softplus(-(w0 + jnp.tanh(wl) @ w2).astype(f32)) - 0.5
    decay = jnp.exp(-jnp.exp(logw))
    a = jax.nn.sigmoid((a0 + al @ a2).astype(f32))
    g = jax.nn.sigmoid(gl) @ g2

    def heads(t):
        return t.astype(f32).reshape(Bsz, T, RWKV_HEADS, HEAD_DIM)

    kk = heads(k * k_k)
    kk = kk * lax.rsqrt(jnp.sum(kk * kk, axis=-1, keepdims=True) + 1e-12)
    k = k.astype(f32) * (1.0 + (a - 1.0) * k_a)
    rh, kh, vh, ah = heads(r), heads(k), heads(v), heads(a)
    y = rwkv7_scan(rh, heads(decay), kh, vh, -kk, kk * ah)
    mean = jnp.mean(y, axis=-1, keepdims=True)
    var = jnp.mean(jnp.square(y - mean), axis=-1, keepdims=True)
    y = ((y - mean) * lax.rsqrt(var + RWKV_GN_EPS)).reshape(Bsz, T, RWKV_WIDTH) * ln_g + ln_b
    bonus = jnp.sum(rh * kh * r_k, axis=-1, keepdims=True) * vh
    return (y + bonus.reshape(Bsz, T, RWKV_WIDTH)) * g


def dilated_window_attention(q, k, v, window, dilation):
    Bsz, T, H, Dh = q.shape
    L = T // dilation
    span = window // dilation
    nb = -(-L // ATTN_BLOCK)
    Lp = nb * ATTN_BLOCK

    def gather(t):
        t = t.reshape(Bsz, L, dilation, H, Dh).transpose(0, 2, 3, 1, 4)
        t = jnp.pad(t, ((0, 0), (0, 0), (0, 0), (0, Lp - L), (0, 0)))
        return t.reshape(Bsz, dilation, H, nb, ATTN_BLOCK, Dh)

    def with_prev(t):
        prev = jnp.pad(t, ((0, 0), (0, 0), (0, 0), (1, 0), (0, 0), (0, 0)))[:, :, :, :-1]
        return jnp.concatenate([prev, t], axis=4)

    qb = gather(q).astype(jnp.float32)
    kw = with_prev(gather(k)).astype(jnp.float32)
    vw = with_prev(gather(v)).astype(jnp.float32)
    qi = jnp.arange(ATTN_BLOCK)[:, None]
    kj = jnp.arange(2 * ATTN_BLOCK)[None, :]
    dist = qi - kj + ATTN_BLOCK
    band = (dist >= 0) & (dist <= span)
    key_step = jnp.arange(nb)[:, None, None] * ATTN_BLOCK + kj[None] - ATTN_BLOCK
    mask = band[None] & (key_step >= 0)
    s = jnp.einsum('bdhnqc,bdhnkc->bdhnqk', qb, kw)
    s = jnp.where(mask, s, NEG_INF)
    lse = jax.nn.logsumexp(s, axis=-1)
    p = jnp.exp(s - lse[..., None])
    o = jnp.einsum('bdhnqk,bdhnkc->bdhnqc', p, vw)
    o = o.reshape(Bsz, dilation, H, Lp, Dh)[:, :, :, :L].transpose(0, 3, 1, 2, 4).reshape(Bsz, T, H, Dh)
    lse = lse.reshape(Bsz, dilation, H, Lp)[:, :, :, :L].transpose(0, 3, 1, 2).reshape(Bsz, T, H)
    return o, lse


def dilated_attention_branch(z):
    Bsz, T, _ = z.shape
    q, k, v = split_cols(z, [ATTN_WIDTH, ATTN_WIDTH, ATTN_WIDTH])
    q = q.reshape(Bsz, T, ATTN_HEADS, HEAD_DIM)
    k = k.reshape(Bsz, T, ATTN_HEADS, HEAD_DIM)
    v = v.reshape(Bsz, T, ATTN_HEADS, HEAD_DIM)
    pos = jnp.arange(T)
    q = rope(q, pos) * (HEAD_DIM ** -0.5)
    k = rope(k, pos)
    outs, lses = [], []
    for gi, (win, dil) in enumerate(ATTN_GROUPS):
        sl = slice(gi * ATTN_HEADS_PER_GROUP, (gi + 1) * ATTN_HEADS_PER_GROUP)
        o, lse = dilated_window_attention(q[:, :, sl], k[:, :, sl], v[:, :, sl], win, dil)
        outs.append(o)
        lses.append(lse)
    alpha = jax.nn.softmax(jnp.stack(lses, axis=0), axis=0)
    o = jnp.sum(alpha[..., None] * jnp.stack(outs, axis=0), axis=0)
    return o.reshape(Bsz, T, ATTN_OUT_WIDTH)


def mlstm_chunkwise(q, k, v, i_pre, f_pre):
    Bsz, H, T, Dk = q.shape
    Dv = v.shape[-1]
    Lc = MLSTM_CHUNK
    nc = T // Lc
    lf = jax.nn.log_sigmoid(f_pre)

    def chunks(t):
        return jnp.moveaxis(t.reshape((Bsz, H, nc, Lc) + t.shape[3:]), 2, 0)

    causal = jnp.tril(jnp.ones((Lc, Lc), bool))

    def step(carry, inp):
        Cm, nv, m = carry
        qc, kc, vc, ic, fc = inp
        b = jnp.cumsum(fc, axis=-1)
        d_intra = jnp.where(causal, b[..., :, None] - b[..., None, :] + ic[..., None, :], -jnp.inf)
        d_inter = b + m[..., None]
        m_t = jnp.maximum(d_inter, jnp.max(d_intra, axis=-1))
        s = jnp.einsum('bhtd,bhsd->bhts', qc, kc) * jnp.exp(d_intra - m_t[..., None])
        w_inter = jnp.exp(d_inter - m_t)
        num = w_inter[..., None] * jnp.einsum('bhtd,bhdv->bhtv', qc, Cm) + jnp.einsum('bhts,bhsv->bhtv', s, vc)
        den = w_inter * jnp.einsum('bhtd,bhd->bht', qc, nv) + jnp.sum(s, axis=-1)
        h = num / jnp.maximum(jnp.abs(den), jnp.exp(-m_t))[..., None]
        b_end = b[..., -1]
        g_end = b_end[..., None] - b + ic
        m_new = jnp.maximum(b_end + m, jnp.max(g_end, axis=-1))
        w_k = jnp.exp(g_end - m_new[..., None])
        dec = jnp.exp(b_end + m - m_new)
        Cm = dec[..., None, None] * Cm + jnp.einsum('bhs,bhsd,bhsv->bhdv', w_k, kc, vc)
        nv = dec[..., None] * nv + jnp.einsum('bhs,bhsd->bhd', w_k, kc)
        return (Cm, nv, m_new), h

    init = (jnp.zeros((Bsz, H, Dk, Dv), jnp.float32), jnp.zeros((Bsz, H, Dk), jnp.float32),
            jnp.zeros((Bsz, H), jnp.float32))
    _, h = lax.scan(step, init, (chunks(q), chunks(k), chunks(v), chunks(i_pre), chunks(lf)))
    return jnp.moveaxis(h, 0, 2).reshape(Bsz, H, T, Dv)


def mlstm_branch(z, conv_w, conv_b, i_b, f_b):
    Bsz, T, _ = z.shape
    f32 = jnp.float32
    qk, v, o, ig, fg = split_cols(z, [2 * MLSTM_WIDTH, MLSTM_WIDTH, MLSTM_WIDTH, MLSTM_HEADS, MLSTM_HEADS])
    qk = jax.nn.silu(causal_dwconv(qk, conv_w, conv_b))
    q, k = split_cols(qk, [MLSTM_WIDTH, MLSTM_WIDTH])

    def heads(t):
        return t.astype(f32).reshape(Bsz, T, MLSTM_HEADS, HEAD_DIM).transpose(0, 2, 1, 3)

    i_pre = (ig + i_b).astype(f32).transpose(0, 2, 1)
    f_pre = (fg + f_b).astype(f32).transpose(0, 2, 1)
    h = mlstm_chunkwise(heads(q), heads(k) * (HEAD_DIM ** -0.5), heads(v), i_pre, f_pre)
    h = h.transpose(0, 2, 1, 3).reshape(Bsz, T, MLSTM_WIDTH)
    return jax.nn.sigmoid(o.astype(f32)) * h


def hgrn2_chunkwise(q, k, v, logf):
    Bsz, H, T, Dk = q.shape
    Dv = v.shape[-1]
    Lc = HGRN_CHUNK
    nc = T // Lc

    def chunks(t):
        return jnp.moveaxis(t.reshape(Bsz, H, nc, Lc, t.shape[-1]), 2, 0)

    causal = jnp.tril(jnp.ones((Lc, Lc), bool))[:, :, None]

    def step(S, inp):
        qc, kc, vc, gc = inp
        b = jnp.cumsum(gc, axis=2)
        decay = jnp.exp(jnp.where(causal, b[:, :, :, None, :] - b[:, :, None, :, :], -jnp.inf))
        attn = jnp.einsum('bhtc,bhsc,bhtsc->bhts', qc, kc, decay)
        o = jnp.einsum('bhtc,bhcv->bhtv', qc * jnp.exp(b), S) + jnp.einsum('bhts,bhsv->bhtv', attn, vc)
        b_end = b[:, :, -1]
        S = jnp.exp(b_end)[..., None] * S + jnp.einsum('bhsc,bhsv->bhcv', kc * jnp.exp(b_end[:, :, None] - b), vc)
        return S, o

    _, o = lax.scan(step, jnp.zeros((Bsz, H, Dk, Dv), jnp.float32),
                    (chunks(q), chunks(k), chunks(v), chunks(logf)))
    return jnp.moveaxis(o, 0, 2).reshape(Bsz, H, T, Dv)


def hgrn2_branch(z, lower_bound, norm_g):
    Bsz, T, _ = z.shape
    f32 = jnp.float32
    q, f, i, g = split_cols(z, [HGRN_WIDTH] * 4)
    fgate = lower_bound + (1.0 - lower_bound) * jax.nn.sigmoid(f.astype(f32))

    def heads(t):
        return t.astype(f32).reshape(Bsz, T, HGRN_HEADS, HEAD_DIM).transpose(0, 2, 1, 3)

    o = hgrn2_chunkwise(heads(jax.nn.silu(q)), heads(1.0 - fgate), heads(i), heads(jnp.log(fgate)))
    o = o.transpose(0, 2, 1, 3)
    o = o * lax.rsqrt(jnp.mean(o * o, axis=-1, keepdims=True) + NORM_EPS)
    return o.reshape(Bsz, T, HGRN_WIDTH) * norm_g * jax.nn.sigmoid(g.astype(f32))


def hybrid_mixer(xn, w_in, b_gate, rwkv_mu, rwkv_w0, rwkv_w2, rwkv_a0, rwkv_a2, rwkv_g2,
                 rwkv_k_k, rwkv_k_a, rwkv_r_k, rwkv_ln_g, rwkv_ln_b, mlstm_conv_w, mlstm_conv_b,
                 mlstm_i_b, mlstm_f_b, lower_bound, hgrn_norm_g, p_rwkv, p_attn, p_mlstm, p_hgrn, w_out):
    Bsz, T, D = xn.shape
    dt = xn.dtype
    z = xn @ w_in
    zA, zB, zC, zD, zG = split_cols(z, [RWKV_IN, ATTN_IN, MLSTM_IN, HGRN_IN, GATE_IN])
    yA = rwkv7_branch(zA, rwkv_mu, rwkv_w0, rwkv_w2, rwkv_a0, rwkv_a2, rwkv_g2,
                      rwkv_k_k, rwkv_k_a, rwkv_r_k, rwkv_ln_g, rwkv_ln_b)
    yB = dilated_attention_branch(zB)
    yC = mlstm_branch(zC, mlstm_conv_w, mlstm_conv_b, mlstm_i_b, mlstm_f_b)
    yD = hgrn2_branch(zD, lower_bound, hgrn_norm_g)
    gates = jax.nn.sigmoid(zG.reshape(Bsz, T, N_BRANCHES, D) + b_gate)
    merged = (gates[:, :, 0] * (yA.astype(dt) @ p_rwkv)
              + gates[:, :, 1] * (yB.astype(dt) @ p_attn)
              + gates[:, :, 2] * (yC.astype(dt) @ p_mlstm)
              + gates[:, :, 3] * (yD.astype(dt) @ p_hgrn))
    return merged @ w_out


def conv_ffn(xn, w_up, conv_w, conv_b, w_down):
    h = causal_dwconv(xn @ w_up, conv_w, conv_b)
    u, gt = split_cols(h, [D_FF, D_FF])
    return (jax.nn.silu(gt) * u) @ w_down


def setup_inputs(seed: int = 0) -> dict:
    key = jax.random.key(seed)
    keys = iter(jax.random.split(key, 40))

    def nrm(shape, scale):
        return scale * jax.random.normal(next(keys), shape, jnp.float32)

    L = DEPTH
    return {
        'x': nrm((BATCH, SEQ, D_MODEL), 1.0),
        'norm_mix_g': 1.0 + nrm((L, D_MODEL), 0.02),
        'w_in': nrm((L, D_MODEL, N_IN), D_MODEL ** -0.5),
        'b_gate': nrm((L, N_BRANCHES, D_MODEL), 0.01),
        'rwkv_mu': jax.random.uniform(next(keys), (L, RWKV_IN), jnp.float32, 0.1, 0.9),
        'rwkv_w0': -1.0 + nrm((L, RWKV_WIDTH), 0.5),
        'rwkv_w2': nrm((L, RWKV_DECAY_RANK, RWKV_WIDTH), 0.1),
        'rwkv_a0': nrm((L, RWKV_WIDTH), 0.1),
        'rwkv_a2': nrm((L, RWKV_AAA_RANK, RWKV_WIDTH), 0.1),
        'rwkv_g2': nrm((L, RWKV_GATE_RANK, RWKV_WIDTH), RWKV_GATE_RANK ** -0.5),
        'rwkv_k_k': 0.85 + nrm((L, RWKV_WIDTH), 0.05),
        'rwkv_k_a': 1.0 + nrm((L, RWKV_WIDTH), 0.05),
        'rwkv_r_k': nrm((L, RWKV_HEADS, HEAD_DIM), 0.1),
        'rwkv_ln_g': 1.0 + nrm((L, RWKV_WIDTH), 0.02),
        'rwkv_ln_b': nrm((L, RWKV_WIDTH), 0.01),
        'mlstm_conv_w': nrm((L, MLSTM_CONV, 2 * MLSTM_WIDTH), MLSTM_CONV ** -0.5),
        'mlstm_conv_b': nrm((L, 2 * MLSTM_WIDTH), 0.01),
        'mlstm_i_b': nrm((L, MLSTM_HEADS), 0.1),
        'mlstm_f_b': 3.0 + nrm((L, MLSTM_HEADS), 0.5),
        'hgrn_lb_logits': nrm((L, HGRN_WIDTH), 0.5),
        'hgrn_norm_g': 1.0 + nrm((L, HGRN_WIDTH), 0.02),
        'p_rwkv': nrm((L, RWKV_WIDTH, D_MODEL), RWKV_WIDTH ** -0.5),
        'p_attn': nrm((L, ATTN_OUT_WIDTH, D_MODEL), ATTN_OUT_WIDTH ** -0.5),
        'p_mlstm': nrm((L, MLSTM_WIDTH, D_MODEL), MLSTM_WIDTH ** -0.5),
        'p_hgrn': nrm((L, HGRN_WIDTH, D_MODEL), HGRN_WIDTH ** -0.5),
        'w_out': nrm((L, D_MODEL, D_MODEL), D_MODEL ** -0.5),
        'norm_ffn_g': 1.0 + nrm((L, D_MODEL), 0.02),
        'w_up': nrm((L, D_MODEL, 2 * D_FF), D_MODEL ** -0.5),
        'ffn_conv_w': nrm((L, FFN_CONV, 2 * D_FF), FFN_CONV ** -0.5),
        'ffn_conv_b': nrm((L, 2 * D_FF), 0.01),
        'w_down': nrm((L, D_FF, D_MODEL), D_FF ** -0.5),
        'final_norm_g': 1.0 + nrm((D_MODEL,), 0.02),
    }


def reference(x, norm_mix_g, w_in, b_gate, rwkv_mu, rwkv_w0, rwkv_w2, rwkv_a0, rwkv_a2, rwkv_g2,
              rwkv_k_k, rwkv_k_a, rwkv_r_k, rwkv_ln_g, rwkv_ln_b, mlstm_conv_w, mlstm_conv_b,
              mlstm_i_b, mlstm_f_b, hgrn_lb_logits, hgrn_norm_g, p_rwkv, p_attn, p_mlstm, p_hgrn,
              w_out, norm_ffn_g, w_up, ffn_conv_w, ffn_conv_b, w_down, final_norm_g):
    lb_p = jax.nn.softmax(hgrn_lb_logits.astype(jnp.float32), axis=0)
    lower_bounds = jnp.cumsum(lb_p, axis=0) - lb_p[0]
    h = x
    for l in range(DEPTH):
        xn = rmsnorm(h, norm_mix_g[l])
        mix = hybrid_mixer(xn, w_in[l], b_gate[l], rwkv_mu[l], rwkv_w0[l], rwkv_w2[l], rwkv_a0[l],
                           rwkv_a2[l], rwkv_g2[l], rwkv_k_k[l], rwkv_k_a[l], rwkv_r_k[l], rwkv_ln_g[l],
                           rwkv_ln_b[l], mlstm_conv_w[l], mlstm_conv_b[l], mlstm_i_b[l], mlstm_f_b[l],
                           lower_bounds[l], hgrn_norm_g[l], p_rwkv[l], p_attn[l], p_mlstm[l], p_hgrn[l],
                           w_out[l])
        h = h + mix.astype(h.dtype)
        xn = rmsnorm(h, norm_ffn_g[l])
        h = h + conv_ffn(xn, w_up[l], ffn_conv_w[l], ffn_conv_b[l], w_down[l]).astype(h.dtype)
    return rmsnorm(h, final_norm_g)
```

```python
import functools

import jax
import jax.numpy as jnp
from jax import lax
from jax.experimental import pallas as pl
from jax.experimental.pallas import tpu as pltpu

F32 = jnp.float32
BF16 = jnp.bfloat16

HEAD_DIM = 64
N_HEADS = 4
WIDTH = N_HEADS * HEAD_DIM
CHUNK = 64
RWKV_LORA = 128
RWKV_IN = 3 * WIDTH + RWKV_LORA
RWKV_GN_EPS = 64e-5
ATTN_GROUPS = ((128, 1), (512, 4), (2048, 16))
ATTN_HEADS = 6
ATTN_WIDTH = ATTN_HEADS * HEAD_DIM
ATTN_OUT = 2 * HEAD_DIM
ROPE_THETA = 10000.0
MLSTM_CONV = 4
MLSTM_IN_PAD = 4 * WIDTH + 128
HGRN_IN = 4 * WIDTH
N_BRANCHES = 4
FFN_CONV = 3
NORM_EPS = 1e-6
NEG_INF = -1e30

TOKEN_TILE = 256
SEQ_TILE = 256
FF_TILE = 256
CARRY_ROWS = 8
VMEM_LIMIT = 56 * 1024 * 1024


def _dot(a, b):
    return jnp.dot(a.astype(BF16), b.astype(BF16), preferred_element_type=F32)


def _dot_nt(a, b):
    return lax.dot_general(a.astype(BF16), b.astype(BF16), (((1,), (1,)), ((), ())),
                           preferred_element_type=F32)


def _dot_tn(a, b):
    return lax.dot_general(a.astype(BF16), b.astype(BF16), (((0,), (0,)), ((), ())),
                           preferred_element_type=F32)


def _split3(x):
    hi = x.astype(BF16)
    r1 = x - hi.astype(F32)
    mid = r1.astype(BF16)
    lo = (r1 - mid.astype(F32)).astype(BF16)
    return hi, mid, lo


def _sel_dot(m01, x):
    hi, mid, lo = _split3(x)
    return (jnp.dot(m01, hi, preferred_element_type=F32)
            + jnp.dot(m01, mid, preferred_element_type=F32)
            + jnp.dot(m01, lo, preferred_element_type=F32))


def _dot_sel(x, m01):
    hi, mid, lo = _split3(x)
    return (jnp.dot(hi, m01, preferred_element_type=F32)
            + jnp.dot(mid, m01, preferred_element_type=F32)
            + jnp.dot(lo, m01, preferred_element_type=F32))


def _iota(shape, dim):
    return lax.broadcasted_iota(jnp.int32, shape, dim)


def _onehot(cond):
    return jnp.where(cond, 1.0, 0.0).astype(BF16)


def _chunk_cumsum_matrix(n):
    r, c = _iota((n, n), 0), _iota((n, n), 1)
    return _onehot((jnp.right_shift(r, 6) == jnp.right_shift(c, 6)) & (c <= r))


def _chunk_total_matrix(n):
    r, c = _iota((n, n), 0), _iota((n, n), 1)
    return _onehot(jnp.right_shift(r, 6) == jnp.right_shift(c, 6))


def _head_sum_matrix(n):
    r, c = _iota((n, n), 0), _iota((n, n), 1)
    return _onehot(jnp.right_shift(r, 6) == jnp.right_shift(c, 6))


def _shift_rows(x, carry, j):
    ext = jnp.concatenate([carry, x], axis=0)
    return pltpu.roll(ext, j, axis=0)[CARRY_ROWS:]


def _softplus(x):
    return jnp.maximum(x, 0.0) + jnp.log1p(jnp.exp(-jnp.abs(x)))


def _sigmoid(x):
    return jax.nn.sigmoid(x)


def _silu(x):
    return x * _sigmoid(x)


def _rms(x, g):
    return x * lax.rsqrt(jnp.mean(x * x, axis=-1, keepdims=True) + NORM_EPS) * g


def _const_spec(shape, single_buffer=False):
    nd = len(shape)
    if single_buffer:
        return pl.BlockSpec(shape, lambda *_: (0,) * nd, pipeline_mode=pl.Buffered(1))
    return pl.BlockSpec(shape, lambda *_: (0,) * nd)


def _params(*sem):
    return pltpu.CompilerParams(dimension_semantics=sem, vmem_limit_bytes=VMEM_LIMIT)


def _inproj_kernel(x_ref, g_ref, w_ref, cos_ref, sin_ref, za_ref, zb_ref, zc_ref, zd_ref):
    xb = _rms(x_ref[...], g_ref[...]).astype(BF16)

    def mm(lo, width):
        return jnp.dot(xb, w_ref[:, lo:lo + width], preferred_element_type=F32)

    za_ref[...] = mm(0, RWKV_IN)
    o = RWKV_IN
    qk = mm(o, 2 * ATTN_WIDTH)
    vv = mm(o + 2 * ATTN_WIDTH, ATTN_WIDTH)
    qk_sw = mm(o + 3 * ATTN_WIDTH, 2 * ATTN_WIDTH)
    cos = jnp.concatenate([cos_ref[...]] * (2 * ATTN_WIDTH // 128), axis=1)
    sin = jnp.concatenate([sin_ref[...]] * (2 * ATTN_WIDTH // 128), axis=1)
    roped = qk * cos + qk_sw * sin
    zb_ref[:, 0:ATTN_WIDTH] = roped[:, 0:ATTN_WIDTH] * (HEAD_DIM ** -0.5)
    zb_ref[:, ATTN_WIDTH:2 * ATTN_WIDTH] = roped[:, ATTN_WIDTH:]
    zb_ref[:, 2 * ATTN_WIDTH:] = vv
    o += 5 * ATTN_WIDTH
    zc_ref[...] = mm(o, MLSTM_IN_PAD)
    o += MLSTM_IN_PAD
    zd_ref[...] = mm(o, HGRN_IN)


def _inproj(h, g, w_cat, cos_t, sin_t, seq):
    n, d = h.shape
    tm = TOKEN_TILE
    nt = seq // tm
    ncols = w_cat.shape[1]
    widths = (RWKV_IN, 3 * ATTN_WIDTH, MLSTM_IN_PAD, HGRN_IN)
    row = lambda w: pl.BlockSpec((tm, w), lambda i: (i, 0))
    return pl.pallas_call(
        _inproj_kernel,
        grid=(n // tm,),
        in_specs=[row(d), _const_spec((1, d)), _const_spec((d, ncols), True),
                  pl.BlockSpec((tm, 128), lambda i: (i % nt, 0)),
                  pl.BlockSpec((tm, 128), lambda i: (i % nt, 0))],
        out_specs=[row(w) for w in widths],
        out_shape=[jax.ShapeDtypeStruct((n, w), F32) for w in widths],
        compiler_params=_params("parallel"),
        name="inproj",
    )(h, g, w_cat, cos_t, sin_t)


def _tri_inverse(a):
    n = CHUNK
    r, c = _iota((n, n), 0), _iota((n, n), 1)
    same16 = jnp.right_shift(r, 4) == jnp.right_shift(c, 4)
    same32 = jnp.right_shift(r, 5) == jnp.right_shift(c, 5)
    eye = jnp.where(r == c, 1.0, 0.0)
    n1 = jnp.where(same16, a, 0.0)
    n2 = _dot(n1, n1)
    n4 = _dot(n2, n2)
    n8 = _dot(n4, n4)
    t = eye + n1
    t = t + _dot(t, n2)
    t = t + _dot(t, n4)
    t = t + _dot(t, n8)
    a1 = jnp.where(same32 & jnp.logical_not(same16), a, 0.0)
    t = t + _dot(_dot(t, a1), t)
    a2 = jnp.where(same32, 0.0, a)
    t = t + _dot(_dot(t, a2), t)
    return t


def _rwkv_kernel(z_ref, mu_ref, w0_ref, w2_ref, a0_ref, a2_ref, g2_ref, kk_ref, ka_ref, rk_ref,
                 lng_ref, lnb_ref, y_ref, carry_ref, s_ref):
    tt = z_ref.shape[0]

    @pl.when(pl.program_id(1) == 0)
    def _():
        carry_ref[...] = jnp.zeros_like(carry_ref)
        s_ref[...] = jnp.zeros_like(s_ref)

    z = z_ref[...]
    zprev = _shift_rows(z, carry_ref[...], 1)
    carry_ref[...] = z[tt - CARRY_ROWS:, :]
    z = z + mu_ref[...] * (zprev - z)
    r, k, v, lora = z[:, 0:WIDTH], z[:, WIDTH:2 * WIDTH], z[:, 2 * WIDTH:3 * WIDTH], z[:, 3 * WIDTH:]

    logw = -_softplus(-(w0_ref[...] + _dot(jnp.tanh(lora), w2_ref[...]))) - 0.5
    ld = -jnp.exp(logw)
    a = _sigmoid(a0_ref[...] + _dot(lora, a2_ref[...]))
    g = _dot(_sigmoid(lora), g2_ref[...])

    hsum = _head_sum_matrix(WIDTH)
    kk = k * kk_ref[...]
    kk = kk * lax.rsqrt(_dot_sel(kk * kk, hsum) + 1e-12)
    k = k * (1.0 + (a - 1.0) * ka_ref[...])
    sa, sb = -kk, kk * a

    cs = _sel_dot(_chunk_cumsum_matrix(tt), ld)
    tot = _sel_dot(_chunk_total_matrix(tt), ld)
    g_inc, g_exc, g_inv, g_end = jnp.exp(cs), jnp.exp(cs - ld), jnp.exp(-cs), jnp.exp(tot - cs)
    g_tot = jnp.exp(tot)
    rt, at = r * g_inc, sa * g_exc
    bt, kt = sb * g_inv, k * g_inv
    bh, kh = sb * g_end, k * g_end

    n = CHUNK
    ri, ci = _iota((n, n), 0), _iota((n, n), 1)
    strict, incl = ci < ri, ci <= ri

    y_rows = []
    states = [s_ref[h] for h in range(N_HEADS)]
    for c in range(tt // n):
        rs = slice(c * n, (c + 1) * n)
        y_heads = []
        for h in range(N_HEADS):
            hs = slice(h * HEAD_DIM, (h + 1) * HEAD_DIM)
            at_, rt_, bt_, kt_ = at[rs, hs], rt[rs, hs], bt[rs, hs], kt[rs, hs]
            v_, bh_, kh_ = v[rs, hs], bh[rs, hs], kh[rs, hs]
            gm = _dot_nt(jnp.concatenate([at_, rt_], axis=0), jnp.concatenate([bt_, kt_], axis=0))
            a_ab = jnp.where(strict, gm[:n, :n], 0.0)
            a_ak = jnp.where(strict, gm[:n, n:], 0.0)
            a_rb = jnp.where(incl, gm[n:, :n], 0.0)
            a_rk = jnp.where(incl, gm[n:, n:], 0.0)
            tinv = _tri_inverse(a_ab)
            w = _dot(tinv, jnp.concatenate([at_, _dot(a_ak, v_)], axis=1))
            q = jnp.concatenate([rt_, _dot(a_rk, v_)], axis=1) + _dot(a_rb, w)
            pp = _dot_tn(w, bh_)
            s = states[h]
            y_heads.append(_dot_nt(q[:, :n], s) + q[:, n:])
            states[h] = (s * g_tot[c * n:c * n + 1, hs] + _dot(s, pp[:n])
                         + pp[n:] + _dot_tn(v_, kh_))
        y_rows.append(jnp.concatenate(y_heads, axis=1))
    for h in range(N_HEADS):
        s_ref[h] = states[h]
    y = jnp.concatenate(y_rows, axis=0)

    mean = _dot_sel(y, hsum) * (1.0 / HEAD_DIM)
    yc = y - mean
    var = _dot_sel(yc * yc, hsum) * (1.0 / HEAD_DIM)
    yn = yc * lax.rsqrt(var + RWKV_GN_EPS) * lng_ref[...] + lnb_ref[...]
    bonus = _dot_sel(r * k * rk_ref[...], hsum) * v
    y_ref[...] = (yn + bonus) * g


def _rwkv(z, p):
    b, t, _ = z.shape
    tt = SEQ_TILE
    vec = lambda a: a.reshape(1, -1)
    pad = lambda a, lo: jnp.zeros((RWKV_LORA, WIDTH), F32).at[lo:lo + a.shape[0]].set(a).astype(BF16)
    args = (z, vec(p["mu"]), vec(p["w0"]), pad(p["w2"], 0), vec(p["a0"]), pad(p["a2"], 32),
            pad(p["g2"], 64), vec(p["k_k"]), vec(p["k_a"]), vec(p["r_k"]), vec(p["ln_g"]), vec(p["ln_b"]))
    in_specs = [pl.BlockSpec((None, tt, RWKV_IN), lambda i, j: (i, j, 0))]
    in_specs += [_const_spec(a.shape) for a in args[1:]]
    return pl.pallas_call(
        _rwkv_kernel,
        grid=(b, t // tt),
        in_specs=in_specs,
        out_specs=pl.BlockSpec((None, tt, WIDTH), lambda i, j: (i, j, 0)),
        out_shape=jax.ShapeDtypeStruct((b, t, WIDTH), F32),
        scratch_shapes=[pltpu.VMEM((CARRY_ROWS, RWKV_IN), F32),
                        pltpu.VMEM((N_HEADS, HEAD_DIM, HEAD_DIM), F32)],
        compiler_params=_params("parallel", "arbitrary"),
        name="rwkv7",
    )(*args)


def _attn_kernel(q_ref, k_ref, v_ref, o_ref):
    tq = q_ref.shape[0]
    t = k_ref.shape[0]
    t0 = pl.program_id(1) * tq
    lane = _iota((tq, ATTN_OUT), 1)
    first = lane < HEAD_DIM
    outs = [[], []]
    lses = [[], []]
    for gi, (win, dil) in enumerate(ATTN_GROUPS):
        size = min(win + tq, t)
        cs = slice(gi * ATTN_OUT, (gi + 1) * ATTN_OUT)
        if size == t:
            start = 0
            k2, v2 = k_ref[:, cs], v_ref[:, cs]
        else:
            start = pl.multiple_of(jnp.clip(t0 - win, 0, t - size), 128)
            k2, v2 = k_ref[pl.ds(start, size), cs], v_ref[pl.ds(start, size), cs]
        diff = (t0 + _iota((tq, size), 0)) - (start + _iota((tq, size), 1))
        mask = (diff >= 0) & (diff <= win) & ((diff & (dil - 1)) == 0)
        q2 = q_ref[:, cs]
        k2 = k2.astype(BF16)
        v2 = v2.astype(BF16)
        for j in range(2):
            qj = jnp.where(first if j == 0 else jnp.logical_not(first), q2, 0.0)
            s = jnp.where(mask, _dot_nt(qj, k2), NEG_INF)
            m = jnp.max(s, axis=-1, keepdims=True)
            e = jnp.exp(s - m)
            l = jnp.sum(e, axis=-1, keepdims=True)
            outs[j].append(_dot(e, v2) / l)
            lses[j].append(m + jnp.log(l))
    res = []
    for j in range(2):
        mx = jnp.maximum(jnp.maximum(lses[j][0], lses[j][1]), lses[j][2])
        ws = [jnp.exp(x - mx) for x in lses[j]]
        den = ws[0] + ws[1] + ws[2]
        res.append((ws[0] * outs[j][0] + ws[1] * outs[j][1] + ws[2] * outs[j][2]) / den)
    o_ref[...] = jnp.where(first, res[0], res[1])


def _attention(zb):
    b, t, _ = zb.shape
    tq = SEQ_TILE
    full = lambda blk: pl.BlockSpec((None, t, ATTN_WIDTH), lambda i, j, blk=blk: (i, 0, blk))
    return pl.pallas_call(
        _attn_kernel,
        grid=(b, t // tq),
        in_specs=[pl.BlockSpec((None, tq, ATTN_WIDTH), lambda i, j: (i, j, 0)), full(1), full(2)],
        out_specs=pl.BlockSpec((None, tq, ATTN_OUT), lambda i, j: (i, j, 0)),
        out_shape=jax.ShapeDtypeStruct((b, t, ATTN_OUT), F32),
        compiler_params=_params("parallel", "arbitrary"),
        name="dilated_attention",
    )(zb, zb, zb)


def _mlstm_kernel(z_ref, cw_ref, cb_ref, gb_ref, y_ref, carry_ref, c_ref, m_ref):
    tt = z_ref.shape[0]

    @pl.when(pl.program_id(1) == 0)
    def _():
        carry_ref[...] = jnp.zeros_like(carry_ref)
        c_ref[...] = jnp.zeros_like(c_ref)
        m_ref[...] = jnp.zeros_like(m_ref)

    qk_in = z_ref[:, 0:2 * WIDTH]
    ext = jnp.concatenate([carry_ref[...], qk_in], axis=0)
    acc = cb_ref[...] + cw_ref[MLSTM_CONV - 1:MLSTM_CONV, :] * qk_in
    for j in range(1, MLSTM_CONV):
        acc = acc + cw_ref[MLSTM_CONV - 1 - j:MLSTM_CONV - j, :] * pltpu.roll(ext, j, axis=0)[CARRY_ROWS:]
    carry_ref[...] = qk_in[tt - CARRY_ROWS:, :]
    qk = _silu(acc)
    q, k = qk[:, :WIDTH], qk[:, WIDTH:] * (HEAD_DIM ** -0.5)
    v = z_ref[:, 2 * WIDTH:3 * WIDTH]
    og = _sigmoid(z_ref[:, 3 * WIDTH:4 * WIDTH])

    gates = z_ref[:, 4 * WIDTH:] + gb_ref[...]
    lf = -_softplus(-gates)
    bcs = _sel_dot(_chunk_cumsum_matrix(tt), lf)
    gates_t, bcs_t = gates.T, bcs.T

    n = CHUNK
    ri, ci = _iota((n, n), 0), _iota((n, n), 1)
    causal = ci <= ri
    ones_col = jnp.where(_iota((n, HEAD_DIM), 1) == 0, 1.0, 0.0)

    y_rows = []
    cms = [c_ref[h] for h in range(N_HEADS)]
    ms = [m_ref[h:h + 1, 0:1] for h in range(N_HEADS)]
    for c in range(tt // n):
        rs = slice(c * n, (c + 1) * n)
        y_heads = []
        for h in range(N_HEADS):
            hs = slice(h * HEAD_DIM, (h + 1) * HEAD_DIM)
            fcol = N_HEADS + h
            b_col, i_col = bcs[rs, fcol:fcol + 1], gates[rs, h:h + 1]
            b_row, i_row = bcs_t[fcol:fcol + 1, rs], gates_t[h:h + 1, rs]
            qh, kh = q[rs, hs], k[rs, hs]
            vext = jnp.concatenate([v[rs, hs], ones_col], axis=1)
            m_prev, cm = ms[h], cms[h]
            d_intra = jnp.where(causal, b_col - b_row + i_row, -jnp.inf)
            d_inter = b_col + m_prev
            m_t = jnp.maximum(d_inter, jnp.max(d_intra, axis=-1, keepdims=True))
            s = _dot_nt(qh, kh) * jnp.exp(d_intra - m_t)
            ne = jnp.exp(d_inter - m_t) * _dot(qh, cm) + _dot(s, vext)
            den = jnp.maximum(jnp.abs(ne[:, HEAD_DIM:HEAD_DIM + 1]), jnp.exp(-m_t))
            y_heads.append(ne[:, :HEAD_DIM] / den)
            b_end = b_col[n - 1:n, :]
            g_end = b_end - b_col + i_col
            m_new = jnp.maximum(b_end + m_prev, jnp.max(g_end, axis=0, keepdims=True))
            cms[h] = jnp.exp(b_end + m_prev - m_new) * cm + _dot_tn(kh * jnp.exp(g_end - m_new), vext)
            ms[h] = m_new
        y_rows.append(jnp.concatenate(y_heads, axis=1))
    for h in range(N_HEADS):
        c_ref[h] = cms[h]
        m_ref[h:h + 1, :] = jnp.broadcast_to(ms[h], (1, m_ref.shape[1]))
    y_ref[...] = og * jnp.concatenate(y_rows, axis=0)


def _mlstm(z, p):
    b, t, _ = z.shape
    tt = SEQ_TILE
    gate_bias = jnp.zeros((1, 128), F32).at[0, 0:N_HEADS].set(p["i_b"]).at[0, N_HEADS:2 * N_HEADS].set(p["f_b"])
    args = (z, p["conv_w"], p["conv_b"].reshape(1, -1), gate_bias)
    in_specs = [pl.BlockSpec((None, tt, MLSTM_IN_PAD), lambda i, j: (i, j, 0))]
    in_specs += [_const_spec(a.shape) for a in args[1:]]
    return pl.pallas_call(
        _mlstm_kernel,
        grid=(b, t // tt),
        in_specs=in_specs,
        out_specs=pl.BlockSpec((None, tt, WIDTH), lambda i, j: (i, j, 0)),
        out_shape=jax.ShapeDtypeStruct((b, t, WIDTH), F32),
        scratch_shapes=[pltpu.VMEM((CARRY_ROWS, 2 * WIDTH), F32),
                        pltpu.VMEM((N_HEADS, HEAD_DIM, 2 * HEAD_DIM), F32),
                        pltpu.VMEM((8, 128), F32)],
        compiler_params=_params("parallel", "arbitrary"),
        name="mlstm",
    )(*args)


def _hgrn_kernel(z_ref, lbl_ref, ng_ref, y_ref, s_ref, *, layer):
    tt = z_ref.shape[0]

    @pl.when(pl.program_id(1) == 0)
    def _():
        s_ref[...] = jnp.zeros_like(s_ref)

    logits = lbl_ref[...]
    pe = jnp.exp(logits - jnp.max(logits, axis=0, keepdims=True))
    pr = pe / jnp.sum(pe, axis=0, keepdims=True)
    lb = pr[0:1, :]
    for i in range(1, layer + 1):
        lb = lb + pr[i:i + 1, :]
    lb = lb - pr[0:1, :]

    q = _silu(z_ref[:, 0:WIDTH])
    f = lb + (1.0 - lb) * _sigmoid(z_ref[:, WIDTH:2 * WIDTH])
    k = 1.0 - f
    v = z_ref[:, 2 * WIDTH:3 * WIDTH]
    og = _sigmoid(z_ref[:, 3 * WIDTH:4 * WIDTH])
    gl = jnp.log(f)

    bsum = _sel_dot(_chunk_cumsum_matrix(tt), gl)
    r, c = _iota((tt, tt), 0), _iota((tt, tt), 1)
    b_end = _sel_dot(_onehot(c == (jnp.right_shift(r, 6) * CHUNK + CHUNK - 1)), bsum)
    q_in = q * jnp.exp(bsum)
    k_end = k * jnp.exp(b_end - bsum)
    g_tot = jnp.exp(b_end)

    n = CHUNK
    ri, ci = _iota((n, n), 0), _iota((n, n), 1)
    row_t = _iota((tt, 1), 0)
    levels = []
    for lv in range(1, 7):
        size, half = 1 << lv, 1 << (lv - 1)
        mid = jnp.right_shift(r, lv) * size + half - 1
        b_mid = _sel_dot(_onehot(c == mid), bsum)
        right = (row_t & (size - 1)) >= half
        q_l = jnp.where(right, q * jnp.exp(jnp.minimum(bsum - b_mid, 0.0)), 0.0)
        k_l = jnp.where(right, 0.0, k * jnp.exp(jnp.minimum(b_mid - bsum, 0.0)))
        levels.append((q_l, k_l, jnp.right_shift(ri, lv) == jnp.right_shift(ci, lv)))
    diag = ri == ci

    y_rows = []
    states = [s_ref[h] for h in range(N_HEADS)]
    for cidx in range(tt // n):
        rs = slice(cidx * n, (cidx + 1) * n)
        y_heads = []
        for h in range(N_HEADS):
            hs = slice(h * HEAD_DIM, (h + 1) * HEAD_DIM)
            attn = jnp.where(diag, _dot_nt(q[rs, hs], k[rs, hs]), 0.0)
            for q_l, k_l, same in levels:
                attn = attn + jnp.where(same, _dot_nt(q_l[rs, hs], k_l[rs, hs]), 0.0)
            vh = v[rs, hs]
            s = states[h]
            y_heads.append(_dot_nt(q_in[rs, hs], s) + _dot(attn, vh))
            states[h] = s * g_tot[cidx * n:cidx * n + 1, hs] + _dot_tn(vh, k_end[rs, hs])
        y_rows.append(jnp.concatenate(y_heads, axis=1))
    for h in range(N_HEADS):
        s_ref[h] = states[h]
    o = jnp.concatenate(y_rows, axis=0)
    ms = _dot_sel(o * o, _head_sum_matrix(WIDTH)) * (1.0 / HEAD_DIM)
    y_ref[...] = o * lax.rsqrt(ms + NORM_EPS) * ng_ref[...] * og


def _hgrn(z, lb_logits, norm_g, layer):
    b, t, _ = z.shape
    tt = SEQ_TILE
    return pl.pallas_call(
        functools.partial(_hgrn_kernel, layer=layer),
        grid=(b, t // tt),
        in_specs=[pl.BlockSpec((None, tt, HGRN_IN), lambda i, j: (i, j, 0)),
                  _const_spec(lb_logits.shape), _const_spec((1, WIDTH))],
        out_specs=pl.BlockSpec((None, tt, WIDTH), lambda i, j: (i, j, 0)),
        out_shape=jax.ShapeDtypeStruct((b, t, WIDTH), F32),
        scratch_shapes=[pltpu.VMEM((N_HEADS, HEAD_DIM, HEAD_DIM), F32)],
        compiler_params=_params("parallel", "arbitrary"),
        name="hgrn2",
    )(z, lb_logits, norm_g.reshape(1, -1))


def _merge_kernel(h_ref, g_ref, wg_ref, bg_ref, ya_ref, yb_ref, yc_ref, yd_ref,
                  pa_ref, pb_ref, pc_ref, pd_ref, wo_ref, o_ref):
    h = h_ref[...]
    d = h.shape[1]
    xb = _rms(h, g_ref[...]).astype(BF16)
    merged = None
    for i, (y_ref, p_ref) in enumerate(((ya_ref, pa_ref), (yb_ref, pb_ref), (yc_ref, pc_ref), (yd_ref, pd_ref))):
        gate = _sigmoid(jnp.dot(xb, wg_ref[:, i * d:(i + 1) * d], preferred_element_type=F32)
                        + bg_ref[:, i * d:(i + 1) * d])
        term = gate * jnp.dot(y_ref[...].astype(BF16), p_ref[...], preferred_element_type=F32)
        merged = term if merged is None else merged + term
    o_ref[...] = h + jnp.dot(merged.astype(BF16), wo_ref[...], preferred_element_type=F32)


def _merge(h, g, wg, bg, ys, ps, wo):
    n, d = h.shape
    tm = TOKEN_TILE
    row = lambda w: pl.BlockSpec((tm, w), lambda i: (i, 0))
    in_specs = [row(d), _const_spec((1, d)), _const_spec(wg.shape, True), _const_spec(bg.shape)]
    in_specs += [row(y.shape[1]) for y in ys]
    in_specs += [_const_spec(p.shape, True) for p in ps]
    in_specs += [_const_spec(wo.shape, True)]
    return pl.pallas_call(
        _merge_kernel,
        grid=(n // tm,),
        in_specs=in_specs,
        out_specs=row(d),
        out_shape=jax.ShapeDtypeStruct((n, d), F32),
        compiler_params=_params("parallel"),
        name="merge",
    )(h, g, wg, bg, *ys, *ps, wo)


def _ffn_kernel(h_ref, g_ref, wu_ref, cw_ref, cb_ref, wd_ref, gf_ref, o_ref, carry_ref, acc_ref, *, final):
    tm = h_ref.shape[0]
    dff = wd_ref.shape[0]

    @pl.when(pl.program_id(1) == 0)
    def _():
        carry_ref[...] = jnp.zeros_like(carry_ref)

    h = h_ref[...]
    xb = _rms(h, g_ref[...]).astype(BF16)

    def conv_cols(lo):
        up = jnp.dot(xb, wu_ref[:, lo:lo + FF_TILE], preferred_element_type=F32)
        ext = jnp.concatenate([carry_ref[:, lo:lo + FF_TILE], up], axis=0)
        carry_ref[:, lo:lo + FF_TILE] = up[tm - CARRY_ROWS:, :]
        out = cb_ref[:, lo:lo + FF_TILE] + cw_ref[FFN_CONV - 1:FFN_CONV, lo:lo + FF_TILE] * up
        for j in range(1, FFN_CONV):
            out = out + (cw_ref[FFN_CONV - 1 - j:FFN_CONV - j, lo:lo + FF_TILE]
                         * pltpu.roll(ext, j, axis=0)[CARRY_ROWS:])
        return out

    for ci in range(dff // FF_TILE):
        lo = ci * FF_TILE
        act = (_silu(conv_cols(dff + lo)) * conv_cols(lo)).astype(BF16)
        part = jnp.dot(act, wd_ref[lo:lo + FF_TILE, :], preferred_element_type=F32)
        if ci == 0:
            acc_ref[...] = h + part
        else:
            acc_ref[...] += part
    out = acc_ref[...]
    o_ref[...] = _rms(out, gf_ref[...]) if final else out


def _ffn(h, g, wu, cw, cb, wd, gf, final):
    b, t, d = h.shape
    tm = TOKEN_TILE
    blk = pl.BlockSpec((None, tm, d), lambda i, j: (i, j, 0))
    return pl.pallas_call(
        functools.partial(_ffn_kernel, final=final),
        grid=(b, t // tm),
        in_specs=[blk, _const_spec((1, d)), _const_spec(wu.shape, True), _const_spec(cw.shape),
                  _const_spec(cb.shape), _const_spec(wd.shape, True), _const_spec((1, d))],
        out_specs=blk,
        out_shape=jax.ShapeDtypeStruct((b, t, d), F32),
        scratch_shapes=[pltpu.VMEM((CARRY_ROWS, wu.shape[1]), F32), pltpu.VMEM((tm, d), F32)],
        compiler_params=_params("parallel", "arbitrary"),
        name="convffn",
    )(h, g, wu, cw, cb, wd, gf)


def _rope_tables(seq):
    half = HEAD_DIM // 2
    inv_freq = ROPE_THETA ** (-jnp.arange(half, dtype=F32) / half)
    ang = jnp.arange(seq).astype(F32)[:, None] * inv_freq[None, :]
    cos, sin = jnp.cos(ang), jnp.sin(ang)
    return jnp.concatenate([cos, cos, cos, cos], axis=1), jnp.concatenate([-sin, sin, -sin, sin], axis=1)


def _swap_head_halves(w):
    d, n = w.shape
    return w.reshape(d, n // HEAD_DIM, 2, HEAD_DIM // 2)[:, :, ::-1, :].reshape(d, n)


def _split_w_in(w):
    d = w.shape[0]
    o = 0
    wa = w[:, o:o + RWKV_IN]; o += RWKV_IN
    wb = w[:, o:o + 3 * ATTN_WIDTH]; o += 3 * ATTN_WIDTH
    wc = w[:, o:o + 4 * WIDTH + 2 * N_HEADS]; o += 4 * WIDTH + 2 * N_HEADS
    wd = w[:, o:o + HGRN_IN]; o += HGRN_IN
    wg = w[:, o:]
    wqk = wb[:, :2 * ATTN_WIDTH]
    wc_pad = jnp.concatenate([wc, jnp.zeros((d, MLSTM_IN_PAD - wc.shape[1]), w.dtype)], axis=1)
    w_cat = jnp.concatenate([wa, wb, _swap_head_halves(wqk), wc_pad, wd], axis=1)
    return w_cat.astype(BF16), wg.astype(BF16)


def kernel(x, norm_mix_g, w_in, b_gate, rwkv_mu, rwkv_w0, rwkv_w2, rwkv_a0, rwkv_a2, rwkv_g2, rwkv_k_k, rwkv_k_a, rwkv_r_k, rwkv_ln_g, rwkv_ln_b, mlstm_conv_w, mlstm_conv_b, mlstm_i_b, mlstm_f_b, hgrn_lb_logits, hgrn_norm_g, p_rwkv, p_attn, p_mlstm, p_hgrn, w_out, norm_ffn_g, w_up, ffn_conv_w, ffn_conv_b, w_down, final_norm_g):
    bsz, seq, d = x.shape
    depth = w_in.shape[0]
    n = bsz * seq
    cos_t, sin_t = _rope_tables(seq)
    h = x.reshape(n, d)
    for l in range(depth):
        w_cat, wg = _split_w_in(w_in[l])
        za, zb, zc, zd = _inproj(h, norm_mix_g[l].reshape(1, d), w_cat, cos_t, sin_t, seq)
        rwkv_p = dict(mu=rwkv_mu[l], w0=rwkv_w0[l], w2=rwkv_w2[l], a0=rwkv_a0[l], a2=rwkv_a2[l],
                      g2=rwkv_g2[l], k_k=rwkv_k_k[l], k_a=rwkv_k_a[l], r_k=rwkv_r_k[l],
                      ln_g=rwkv_ln_g[l], ln_b=rwkv_ln_b[l])
        ya = _rwkv(za.reshape(bsz, seq, -1), rwkv_p)
        yb = _attention(zb.reshape(bsz, seq, -1))
        yc = _mlstm(zc.reshape(bsz, seq, -1),
                    dict(conv_w=mlstm_conv_w[l], conv_b=mlstm_conv_b[l], i_b=mlstm_i_b[l], f_b=mlstm_f_b[l]))
        yd = _hgrn(zd.reshape(bsz, seq, -1), hgrn_lb_logits, hgrn_norm_g[l], l)
        ys = [y.reshape(n, -1) for y in (ya, yb, yc, yd)]
        ps = [p[l].astype(BF16) for p in (p_rwkv, p_attn, p_mlstm, p_hgrn)]
        h = _merge(h, norm_mix_g[l].reshape(1, d), wg, b_gate[l].reshape(1, -1), ys, ps,
                   w_out[l].astype(BF16))
        h = _ffn(h.reshape(bsz, seq, d), norm_ffn_g[l].reshape(1, d), w_up[l].astype(BF16),
                 ffn_conv_w[l], ffn_conv_b[l].reshape(1, -1), w_down[l].astype(BF16),
                 final_norm_g.reshape(1, d), l == depth - 1).reshape(n, d)
    return h.reshape(bsz, seq, d)
```

```python
import functools

import jax
import jax.numpy as jnp
from jax import lax
from jax.experimental import pallas as pl
from jax.experimental.pallas import tpu as pltpu

F32 = jnp.float32
BF16 = jnp.bfloat16

HEAD_DIM = 64
N_HEADS = 4
WIDTH = N_HEADS * HEAD_DIM
CHUNK = 64
RWKV_LORA = 128
RWKV_IN = 3 * WIDTH + RWKV_LORA
RWKV_GN_EPS = 64e-5
ATTN_GROUPS = ((128, 1), (512, 4), (2048, 16))
ATTN_HEADS = 6
ATTN_WIDTH = ATTN_HEADS * HEAD_DIM
ATTN_OUT = 2 * HEAD_DIM
ROPE_THETA = 10000.0
MLSTM_CONV = 4
MLSTM_IN_PAD = 4 * WIDTH + 128
HGRN_IN = 4 * WIDTH
N_BRANCHES = 4
FFN_CONV = 3
NORM_EPS = 1e-6
NEG_INF = -1e30

TOKEN_TILE = 256
SEQ_TILE = 256
FF_TILE = 256
CARRY_ROWS = 8
VMEM_LIMIT = 56 * 1024 * 1024


def _dot(a, b):
    return jnp.dot(a.astype(BF16), b.astype(BF16), preferred_element_type=F32)


def _dot_nt(a, b):
    return lax.dot_general(a.astype(BF16), b.astype(BF16), (((1,), (1,)), ((), ())),
                           preferred_element_type=F32)


def _dot_tn(a, b):
    return lax.dot_general(a.astype(BF16), b.astype(BF16), (((0,), (0,)), ((), ())),
                           preferred_element_type=F32)


def _split3(x):
    hi = x.astype(BF16)
    r1 = x - hi.astype(F32)
    mid = r1.astype(BF16)
    lo = (r1 - mid.astype(F32)).astype(BF16)
    return hi, mid, lo


def _sel_dot(m01, x):
    hi, mid, lo = _split3(x)
    return (jnp.dot(m01, hi, preferred_element_type=F32)
            + jnp.dot(m01, mid, preferred_element_type=F32)
            + jnp.dot(m01, lo, preferred_element_type=F32))


def _dot_sel(x, m01):
    hi, mid, lo = _split3(x)
    return (jnp.dot(hi, m01, preferred_element_type=F32)
            + jnp.dot(mid, m01, preferred_element_type=F32)
            + jnp.dot(lo, m01, preferred_element_type=F32))


def _iota(shape, dim):
    return lax.broadcasted_iota(jnp.int32, shape, dim)


def _onehot(cond):
    return jnp.where(cond, 1.0, 0.0).astype(BF16)


def _chunk_cumsum_matrix(n):
    r, c = _iota((n, n), 0), _iota((n, n), 1)
    return _onehot((jnp.right_shift(r, 6) == jnp.right_shift(c, 6)) & (c <= r))


def _chunk_total_matrix(n):
    r, c = _iota((n, n), 0), _iota((n, n), 1)
    return _onehot(jnp.right_shift(r, 6) == jnp.right_shift(c, 6))


def _head_sum_matrix(n):
    r, c = _iota((n, n), 0), _iota((n, n), 1)
    return _onehot(jnp.right_shift(r, 6) == jnp.right_shift(c, 6))


def _shift_rows(x, carry, j):
    ext = jnp.concatenate([carry, x], axis=0)
    return pltpu.roll(ext, j, axis=0)[CARRY_ROWS:]


def _softplus(x):
    return jnp.maximum(x, 0.0) + jnp.log1p(jnp.exp(-jnp.abs(x)))


def _sigmoid(x):
    return jax.nn.sigmoid(x)


def _silu(x):
    return x * _sigmoid(x)


def _rms(x, g):
    return x * lax.rsqrt(jnp.mean(x * x, axis=-1, keepdims=True) + NORM_EPS) * g


def _const_spec(shape, single_buffer=False):
    nd = len(shape)
    if single_buffer:
        return pl.BlockSpec(shape, lambda *_: (0,) * nd, pipeline_mode=pl.Buffered(1))
    return pl.BlockSpec(shape, lambda *_: (0,) * nd)


def _params(*sem):
    return pltpu.CompilerParams(dimension_semantics=sem, vmem_limit_bytes=VMEM_LIMIT)


def _inproj_kernel(x_ref, g_ref, w_ref, cos_ref, sin_ref, za_ref, zb_ref, zc_ref, zd_ref):
    xb = _rms(x_ref[...], g_ref[...]).astype(BF16)

    def mm(lo, width):
        return jnp.dot(xb, w_ref[:, lo:lo + width], preferred_element_type=F32)

    za_ref[...] = mm(0, RWKV_IN)
    o = RWKV_IN
    qk = mm(o, 2 * ATTN_WIDTH)
    vv = mm(o + 2 * ATTN_WIDTH, ATTN_WIDTH)
    qk_sw = mm(o + 3 * ATTN_WIDTH, 2 * ATTN_WIDTH)
    cos = jnp.concatenate([cos_ref[...]] * (2 * ATTN_WIDTH // 128), axis=1)
    sin = jnp.concatenate([sin_ref[...]] * (2 * ATTN_WIDTH // 128), axis=1)
    roped = qk * cos + qk_sw * sin
    zb_ref[:, 0:ATTN_WIDTH] = roped[:, 0:ATTN_WIDTH] * (HEAD_DIM ** -0.5)
    zb_ref[:, ATTN_WIDTH:2 * ATTN_WIDTH] = roped[:, ATTN_WIDTH:]
    zb_ref[:, 2 * ATTN_WIDTH:] = vv
    o += 5 * ATTN_WIDTH
    zc_ref[...] = mm(o, MLSTM_IN_PAD)
    o += MLSTM_IN_PAD
    zd_ref[...] = mm(o, HGRN_IN)


def _inproj(h, g, w_cat, cos_t, sin_t, seq):
    n, d = h.shape
    tm = TOKEN_TILE
    nt = seq // tm
    ncols = w_cat.shape[1]
    widths = (RWKV_IN, 3 * ATTN_WIDTH, MLSTM_IN_PAD, HGRN_IN)
    row = lambda w: pl.BlockSpec((tm, w), lambda i: (i, 0))
    return pl.pallas_call(
        _inproj_kernel,
        grid=(n // tm,),
        in_specs=[row(d), _const_spec((1, d)), _const_spec((d, ncols), True),
                  pl.BlockSpec((tm, 128), lambda i: (i % nt, 0)),
                  pl.BlockSpec((tm, 128), lambda i: (i % nt, 0))],
        out_specs=[row(w) for w in widths],
        out_shape=[jax.ShapeDtypeStruct((n, w), F32) for w in widths],
        compiler_params=_params("parallel"),
        name="inproj",
    )(h, g, w_cat, cos_t, sin_t)


def _blockdiag(x, bd_mask):
    xb = x.astype(BF16)
    return jnp.concatenate([xb] * N_HEADS, axis=0) * bd_mask


def _tri_inverse(a, bd_mask, same16, same32, eye):
    bd = lambda m: [_blockdiag(x, bd_mask) for x in m]
    mm = lambda xs, ys: [_dot(x, y) for x, y in zip(xs, ys)]
    n1 = [jnp.where(same16, x, 0.0) for x in a]
    n2 = mm(n1, bd(n1))
    n4 = mm(n2, bd(n2))
    n8 = mm(n4, bd(n4))
    t = [eye + x for x in n1]
    for p in (n2, n4, n8):
        t = [x + y for x, y in zip(t, mm(t, bd(p)))]
    for blk in (jnp.logical_and(same32, jnp.logical_not(same16)), jnp.logical_not(same32)):
        ab = bd([jnp.where(blk, x, 0.0) for x in a])
        t = [x + y for x, y in zip(t, mm(mm(t, ab), bd(t)))]
    return t


def _rwkv_kernel(z_ref, mu_ref, w0_ref, w2_ref, a0_ref, a2_ref, g2_ref, kk_ref, ka_ref, rk_ref,
                 lng_ref, lnb_ref, y_ref, carry_ref, s_ref):
    tt = z_ref.shape[0]

    @pl.when(pl.program_id(1) == 0)
    def _():
        carry_ref[...] = jnp.zeros_like(carry_ref)
        s_ref[...] = jnp.zeros_like(s_ref)

    z = z_ref[...]
    zprev = _shift_rows(z, carry_ref[...], 1)
    carry_ref[...] = z[tt - CARRY_ROWS:, :]
    z = z + mu_ref[...] * (zprev - z)
    r, k, v, lora = z[:, 0:WIDTH], z[:, WIDTH:2 * WIDTH], z[:, 2 * WIDTH:3 * WIDTH], z[:, 3 * WIDTH:]

    logw = -_softplus(-(w0_ref[...] + _dot(jnp.tanh(lora), w2_ref[...]))) - 0.5
    ld = -jnp.exp(logw)
    a = _sigmoid(a0_ref[...] + _dot(lora, a2_ref[...]))
    g = _dot(_sigmoid(lora), g2_ref[...])

    hsum = _head_sum_matrix(WIDTH)
    kk = k * kk_ref[...]
    kk = kk * lax.rsqrt(_dot_sel(kk * kk, hsum) + 1e-12)
    k = k * (1.0 + (a - 1.0) * ka_ref[...])
    sa, sb = -kk, kk * a

    cs = _sel_dot(_chunk_cumsum_matrix(tt), ld)
    tot = _sel_dot(_chunk_total_matrix(tt), ld)
    g_inc, g_exc, g_inv, g_end = jnp.exp(cs), jnp.exp(cs - ld), jnp.exp(-cs), jnp.exp(tot - cs)
    g_tot = jnp.exp(tot)
    rt, at = r * g_inc, sa * g_exc
    bt, kt = sb * g_inv, k * g_inv
    bh, kh = sb * g_end, k * g_end

    n = CHUNK
    nc = tt // n
    bd_mask = hsum
    bd_mask_f = bd_mask.astype(F32)
    ri, ci = _iota((n, WIDTH), 0), _iota((n, WIDTH), 1) & (n - 1)
    strict, incl, eye = ci < ri, ci <= ri, jnp.where(ci == ri, 1.0, 0.0)
    same16 = jnp.right_shift(ri, 4) == jnp.right_shift(ci, 4)
    same32 = jnp.right_shift(ri, 5) == jnp.right_shift(ci, 5)
    rows = lambda x: [x[c * n:(c + 1) * n] for c in range(nc)]
    bd = lambda xs: [_blockdiag(x, bd_mask) for x in xs]
    at_c, rt_c, v_c = rows(at), rows(rt), rows(v)

    lhs = [jnp.concatenate([x, y], axis=0) for x, y in zip(at_c, rt_c)]
    gb = [_dot_nt(x, y) for x, y in zip(lhs, bd(rows(bt)))]
    gk = [_dot_nt(x, y) for x, y in zip(lhs, bd(rows(kt)))]
    a_ab = [jnp.where(strict, x[:n], 0.0) for x in gb]
    a_rb = [jnp.where(incl, x[n:], 0.0) for x in gb]
    a_ak = [jnp.where(strict, x[:n], 0.0) for x in gk]
    a_rk = [jnp.where(incl, x[n:], 0.0) for x in gk]
    tinv = _tri_inverse(a_ab, bd_mask, same16, same32, eye)
    v_bd = bd(v_c)
    akv = [_dot(x, y) for x, y in zip(a_ak, v_bd)]
    w12 = [_dot(t, jnp.concatenate([x, y], axis=1)) for t, x, y in zip(tinv, bd(at_c), bd(akv))]
    w1 = [x[:, :WIDTH] for x in w12]
    w2 = [x[:, WIDTH:] for x in w12]
    q12 = [_dot(x, jnp.concatenate([y, u], axis=1)) for x, y, u in zip(a_rb, bd(w1), bd(w2))]
    q1 = [x + y[:, :WIDTH] for x, y in zip(rt_c, q12)]
    q2 = [_dot(x, y) + u[:, WIDTH:] for x, y, u in zip(a_rk, v_bd, q12)]
    bh_c, kh_c = rows(bh), rows(kh)
    p1 = [_dot_tn(x, y) * bd_mask_f for x, y in zip(w1, bh_c)]
    p2 = []
    for x, u, y, w in zip(w2, v_c, bh_c, kh_c):
        full = _dot_tn(jnp.concatenate([x, u], axis=0), jnp.concatenate([y, w], axis=0)) * bd_mask_f
        p2.append(full[0:n] + full[n:2 * n] + full[2 * n:3 * n] + full[3 * n:4 * n])

    s = s_ref[...]
    y_rows = []
    for c in range(nc):
        y_rows.append(_dot_nt(q1[c], _blockdiag(s, bd_mask)) + q2[c])
        s = s * g_tot[c * n:c * n + 1, :] + _dot(s, p1[c]) + p2[c]
    s_ref[...] = s
    y = jnp.concatenate(y_rows, axis=0)

    mean = _dot_sel(y, hsum) * (1.0 / HEAD_DIM)
    yc = y - mean
    var = _dot_sel(yc * yc, hsum) * (1.0 / HEAD_DIM)
    yn = yc * lax.rsqrt(var + RWKV_GN_EPS) * lng_ref[...] + lnb_ref[...]
    bonus = _dot_sel(r * k * rk_ref[...], hsum) * v
    y_ref[...] = (yn + bonus) * g


def _rwkv(z, p):
    b, t, _ = z.shape
    tt = SEQ_TILE
    vec = lambda a: a.reshape(1, -1)
    pad = lambda a, lo: jnp.zeros((RWKV_LORA, WIDTH), F32).at[lo:lo + a.shape[0]].set(a).astype(BF16)
    args = (z, vec(p["mu"]), vec(p["w0"]), pad(p["w2"], 0), vec(p["a0"]), pad(p["a2"], 32),
            pad(p["g2"], 64), vec(p["k_k"]), vec(p["k_a"]), vec(p["r_k"]), vec(p["ln_g"]), vec(p["ln_b"]))
    in_specs = [pl.BlockSpec((None, tt, RWKV_IN), lambda i, j: (i, j, 0))]
    in_specs += [_const_spec(a.shape) for a in args[1:]]
    return pl.pallas_call(
        _rwkv_kernel,
        grid=(b, t // tt),
        in_specs=in_specs,
        out_specs=pl.BlockSpec((None, tt, WIDTH), lambda i, j: (i, j, 0)),
        out_shape=jax.ShapeDtypeStruct((b, t, WIDTH), F32),
        scratch_shapes=[pltpu.VMEM((CARRY_ROWS, RWKV_IN), F32),
                        pltpu.VMEM((HEAD_DIM, WIDTH), F32)],
        compiler_params=_params("parallel", "arbitrary"),
        name="rwkv7",
    )(*args)


def _attn_kernel(q_ref, k_ref, v_ref, o_ref):
    tq = q_ref.shape[0]
    t = k_ref.shape[0]
    t0 = pl.program_id(1) * tq
    lane = _iota((tq, ATTN_OUT), 1)
    first = lane < HEAD_DIM
    outs = [[], []]
    lses = [[], []]
    for gi, (win, dil) in enumerate(ATTN_GROUPS):
        size = min(win + tq, t)
        cs = slice(gi * ATTN_OUT, (gi + 1) * ATTN_OUT)
        if size == t:
            start = 0
            k2, v2 = k_ref[:, cs], v_ref[:, cs]
        else:
            start = pl.multiple_of(jnp.clip(t0 - win, 0, t - size), 128)
            k2, v2 = k_ref[pl.ds(start, size), cs], v_ref[pl.ds(start, size), cs]
        diff = (t0 + _iota((tq, size), 0)) - (start + _iota((tq, size), 1))
        mask = (diff >= 0) & (diff <= win) & ((diff & (dil - 1)) == 0)
        q2 = q_ref[:, cs]
        k2 = k2.astype(BF16)
        v2 = v2.astype(BF16)
        for j in range(2):
            qj = jnp.where(first if j == 0 else jnp.logical_not(first), q2, 0.0)
            s = jnp.where(mask, _dot_nt(qj, k2), NEG_INF)
            m = jnp.max(s, axis=-1, keepdims=True)
            e = jnp.exp(s - m)
            l = jnp.sum(e, axis=-1, keepdims=True)
            outs[j].append(_dot(e, v2) / l)
            lses[j].append(m + jnp.log(l))
    res = []
    for j in range(2):
        mx = jnp.maximum(jnp.maximum(lses[j][0], lses[j][1]), lses[j][2])
        ws = [jnp.exp(x - mx) for x in lses[j]]
        den = ws[0] + ws[1] + ws[2]
        res.append((ws[0] * outs[j][0] + ws[1] * outs[j][1] + ws[2] * outs[j][2]) / den)
    o_ref[...] = jnp.where(first, res[0], res[1])


def _attention(zb):
    b, t, _ = zb.shape
    tq = SEQ_TILE
    full = lambda blk: pl.BlockSpec((None, t, ATTN_WIDTH), lambda i, j, blk=blk: (i, 0, blk))
    return pl.pallas_call(
        _attn_kernel,
        grid=(b, t // tq),
        in_specs=[pl.BlockSpec((None, tq, ATTN_WIDTH), lambda i, j: (i, j, 0)), full(1), full(2)],
        out_specs=pl.BlockSpec((None, tq, ATTN_OUT), lambda i, j: (i, j, 0)),
        out_shape=jax.ShapeDtypeStruct((b, t, ATTN_OUT), F32),
        compiler_params=_params("parallel", "arbitrary"),
        name="dilated_attention",
    )(zb, zb, zb)


def _mlstm_kernel(z_ref, cw_ref, cb_ref, gb_ref, y_ref, carry_ref, c_ref, m_ref):
    tt = z_ref.shape[0]

    @pl.when(pl.program_id(1) == 0)
    def _():
        carry_ref[...] = jnp.zeros_like(carry_ref)
        c_ref[...] = jnp.zeros_like(c_ref)
        m_ref[...] = jnp.zeros_like(m_ref)

    qk_in = z_ref[:, 0:2 * WIDTH]
    ext = jnp.concatenate([carry_ref[...], qk_in], axis=0)
    acc = cb_ref[...] + cw_ref[MLSTM_CONV - 1:MLSTM_CONV, :] * qk_in
    for j in range(1, MLSTM_CONV):
        acc = acc + cw_ref[MLSTM_CONV - 1 - j:MLSTM_CONV - j, :] * pltpu.roll(ext, j, axis=0)[CARRY_ROWS:]
    carry_ref[...] = qk_in[tt - CARRY_ROWS:, :]
    qk = _silu(acc)
    q, k = qk[:, :WIDTH], qk[:, WIDTH:] * (HEAD_DIM ** -0.5)
    v = z_ref[:, 2 * WIDTH:3 * WIDTH]
    og = _sigmoid(z_ref[:, 3 * WIDTH:4 * WIDTH])

    gates = z_ref[:, 4 * WIDTH:] + gb_ref[...]
    lf = -_softplus(-gates)
    bcs = _sel_dot(_chunk_cumsum_matrix(tt), lf)
    gates_t, bcs_t = gates.T, bcs.T

    n = CHUNK
    ri, ci = _iota((n, n), 0), _iota((n, n), 1)
    causal = ci <= ri
    ones_col = jnp.where(_iota((n, HEAD_DIM), 1) == 0, 1.0, 0.0)

    y_rows = []
    cms = [c_ref[h] for h in range(N_HEADS)]
    ms = [m_ref[h:h + 1, 0:1] for h in range(N_HEADS)]
    for c in range(tt // n):
        rs = slice(c * n, (c + 1) * n)
        y_heads = []
        for h in range(N_HEADS):
            hs = slice(h * HEAD_DIM, (h + 1) * HEAD_DIM)
            fcol = N_HEADS + h
            b_col, i_col = bcs[rs, fcol:fcol + 1], gates[rs, h:h + 1]
            b_row, i_row = bcs_t[fcol:fcol + 1, rs], gates_t[h:h + 1, rs]
            qh, kh = q[rs, hs], k[rs, hs]
            vext = jnp.concatenate([v[rs, hs], ones_col], axis=1)
            m_prev, cm = ms[h], cms[h]
            d_intra = jnp.where(causal, b_col - b_row + i_row, -jnp.inf)
            d_inter = b_col + m_prev
            m_t = jnp.maximum(d_inter, jnp.max(d_intra, axis=-1, keepdims=True))
            s = _dot_nt(qh, kh) * jnp.exp(d_intra - m_t)
            ne = jnp.exp(d_inter - m_t) * _dot(qh, cm) + _dot(s, vext)
            den = jnp.maximum(jnp.abs(ne[:, HEAD_DIM:HEAD_DIM + 1]), jnp.exp(-m_t))
            y_heads.append(ne[:, :HEAD_DIM] / den)
            b_end = b_col[n - 1:n, :]
            g_end = b_end - b_col + i_col
            m_new = jnp.maximum(b_end + m_prev, jnp.max(g_end, axis=0, keepdims=True))
            cms[h] = jnp.exp(b_end + m_prev - m_new) * cm + _dot_tn(kh * jnp.exp(g_end - m_new), vext)
            ms[h] = m_new
        y_rows.append(jnp.concatenate(y_heads, axis=1))
    for h in range(N_HEADS):
        c_ref[h] = cms[h]
        m_ref[h:h + 1, :] = jnp.broadcast_to(ms[h], (1, m_ref.shape[1]))
    y_ref[...] = og * jnp.concatenate(y_rows, axis=0)


def _mlstm(z, p):
    b, t, _ = z.shape
    tt = SEQ_TILE
    gate_bias = jnp.zeros((1, 128), F32).at[0, 0:N_HEADS].set(p["i_b"]).at[0, N_HEADS:2 * N_HEADS].set(p["f_b"])
    args = (z, p["conv_w"], p["conv_b"].reshape(1, -1), gate_bias)
    in_specs = [pl.BlockSpec((None, tt, MLSTM_IN_PAD), lambda i, j: (i, j, 0))]
    in_specs += [_const_spec(a.shape) for a in args[1:]]
    return pl.pallas_call(
        _mlstm_kernel,
        grid=(b, t // tt),
        in_specs=in_specs,
        out_specs=pl.BlockSpec((None, tt, WIDTH), lambda i, j: (i, j, 0)),
        out_shape=jax.ShapeDtypeStruct((b, t, WIDTH), F32),
        scratch_shapes=[pltpu.VMEM((CARRY_ROWS, 2 * WIDTH), F32),
                        pltpu.VMEM((N_HEADS, HEAD_DIM, 2 * HEAD_DIM), F32),
                        pltpu.VMEM((8, 128), F32)],
        compiler_params=_params("parallel", "arbitrary"),
        name="mlstm",
    )(*args)


def _hgrn_kernel(z_ref, lbl_ref, ng_ref, y_ref, s_ref, *, layer):
    tt = z_ref.shape[0]

    @pl.when(pl.program_id(1) == 0)
    def _():
        s_ref[...] = jnp.zeros_like(s_ref)

    logits = lbl_ref[...]
    pe = jnp.exp(logits - jnp.max(logits, axis=0, keepdims=True))
    pr = pe / jnp.sum(pe, axis=0, keepdims=True)
    lb = pr[0:1, :]
    for i in range(1, layer + 1):
        lb = lb + pr[i:i + 1, :]
    lb = lb - pr[0:1, :]

    q = _silu(z_ref[:, 0:WIDTH])
    f = lb + (1.0 - lb) * _sigmoid(z_ref[:, WIDTH:2 * WIDTH])
    k = 1.0 - f
    v = z_ref[:, 2 * WIDTH:3 * WIDTH]
    og = _sigmoid(z_ref[:, 3 * WIDTH:4 * WIDTH])
    gl = jnp.log(f)

    bsum = _sel_dot(_chunk_cumsum_matrix(tt), gl)
    r, c = _iota((tt, tt), 0), _iota((tt, tt), 1)
    b_end = _sel_dot(_onehot(c == (jnp.right_shift(r, 6) * CHUNK + CHUNK - 1)), bsum)
    q_in = q * jnp.exp(bsum)
    k_end = k * jnp.exp(b_end - bsum)
    g_tot = jnp.exp(b_end)

    n = CHUNK
    ri, ci = _iota((n, n), 0), _iota((n, n), 1)
    row_t = _iota((tt, 1), 0)
    levels = []
    for lv in range(1, 7):
        size, half = 1 << lv, 1 << (lv - 1)
        mid = jnp.right_shift(r, lv) * size + half - 1
        b_mid = _sel_dot(_onehot(c == mid), bsum)
        right = (row_t & (size - 1)) >= half
        q_l = jnp.where(right, q * jnp.exp(jnp.minimum(bsum - b_mid, 0.0)), 0.0)
        k_l = jnp.where(right, 0.0, k * jnp.exp(jnp.minimum(b_mid - bsum, 0.0)))
        levels.append((q_l, k_l, jnp.right_shift(ri, lv) == jnp.right_shift(ci, lv)))
    diag = ri == ci

    y_rows = []
    states = [s_ref[h] for h in range(N_HEADS)]
    for cidx in range(tt // n):
        rs = slice(cidx * n, (cidx + 1) * n)
        y_heads = []
        for h in range(N_HEADS):
            hs = slice(h * HEAD_DIM, (h + 1) * HEAD_DIM)
            attn = jnp.where(diag, _dot_nt(q[rs, hs], k[rs, hs]), 0.0)
            for q_l, k_l, same in levels:
                attn = attn + jnp.where(same, _dot_nt(q_l[rs, hs], k_l[rs, hs]), 0.0)
            vh = v[rs, hs]
            s = states[h]
            y_heads.append(_dot_nt(q_in[rs, hs], s) + _dot(attn, vh))
            states[h] = s * g_tot[cidx * n:cidx * n + 1, hs] + _dot_tn(vh, k_end[rs, hs])
        y_rows.append(jnp.concatenate(y_heads, axis=1))
    for h in range(N_HEADS):
        s_ref[h] = states[h]
    o = jnp.concatenate(y_rows, axis=0)
    ms = _dot_sel(o * o, _head_sum_matrix(WIDTH)) * (1.0 / HEAD_DIM)
    y_ref[...] = o * lax.rsqrt(ms + NORM_EPS) * ng_ref[...] * og


def _hgrn(z, lb_logits, norm_g, layer):
    b, t, _ = z.shape
    tt = SEQ_TILE
    return pl.pallas_call(
        functools.partial(_hgrn_kernel, layer=layer),
        grid=(b, t // tt),
        in_specs=[pl.BlockSpec((None, tt, HGRN_IN), lambda i, j: (i, j, 0)),
                  _const_spec(lb_logits.shape), _const_spec((1, WIDTH))],
        out_specs=pl.BlockSpec((None, tt, WIDTH), lambda i, j: (i, j, 0)),
        out_shape=jax.ShapeDtypeStruct((b, t, WIDTH), F32),
        scratch_shapes=[pltpu.VMEM((N_HEADS, HEAD_DIM, HEAD_DIM), F32)],
        compiler_params=_params("parallel", "arbitrary"),
        name="hgrn2",
    )(z, lb_logits, norm_g.reshape(1, -1))


def _merge_kernel(h_ref, g_ref, wg_ref, bg_ref, ya_ref, yb_ref, yc_ref, yd_ref,
                  pa_ref, pb_ref, pc_ref, pd_ref, wo_ref, o_ref):
    h = h_ref[...]
    d = h.shape[1]
    xb = _rms(h, g_ref[...]).astype(BF16)
    merged = None
    for i, (y_ref, p_ref) in enumerate(((ya_ref, pa_ref), (yb_ref, pb_ref), (yc_ref, pc_ref), (yd_ref, pd_ref))):
        gate = _sigmoid(jnp.dot(xb, wg_ref[:, i * d:(i + 1) * d], preferred_element_type=F32)
                        + bg_ref[:, i * d:(i + 1) * d])
        term = gate * jnp.dot(y_ref[...].astype(BF16), p_ref[...], preferred_element_type=F32)
        merged = term if merged is None else merged + term
    o_ref[...] = h + jnp.dot(merged.astype(BF16), wo_ref[...], preferred_element_type=F32)


def _merge(h, g, wg, bg, ys, ps, wo):
    n, d = h.shape
    tm = TOKEN_TILE
    row = lambda w: pl.BlockSpec((tm, w), lambda i: (i, 0))
    in_specs = [row(d), _const_spec((1, d)), _const_spec(wg.shape, True), _const_spec(bg.shape)]
    in_specs += [row(y.shape[1]) for y in ys]
    in_specs += [_const_spec(p.shape, True) for p in ps]
    in_specs += [_const_spec(wo.shape, True)]
    return pl.pallas_call(
        _merge_kernel,
        grid=(n // tm,),
        in_specs=in_specs,
        out_specs=row(d),
        out_shape=jax.ShapeDtypeStruct((n, d), F32),
        compiler_params=_params("parallel"),
        name="merge",
    )(h, g, wg, bg, *ys, *ps, wo)


def _ffn_kernel(h_ref, g_ref, wu_ref, cw_ref, cb_ref, wd_ref, gf_ref, o_ref, carry_ref, acc_ref, *, final):
    tm = h_ref.shape[0]
    dff = wd_ref.shape[0]

    @pl.when(pl.program_id(1) == 0)
    def _():
        carry_ref[...] = jnp.zeros_like(carry_ref)

    h = h_ref[...]
    xb = _rms(h, g_ref[...]).astype(BF16)

    def conv_cols(lo):
        up = jnp.dot(xb, wu_ref[:, lo:lo + FF_TILE], preferred_element_type=F32)
        ext = jnp.concatenate([carry_ref[:, lo:lo + FF_TILE], up], axis=0)
        carry_ref[:, lo:lo + FF_TILE] = up[tm - CARRY_ROWS:, :]
        out = cb_ref[:, lo:lo + FF_TILE] + cw_ref[FFN_CONV - 1:FFN_CONV, lo:lo + FF_TILE] * up
        for j in range(1, FFN_CONV):
            out = out + (cw_ref[FFN_CONV - 1 - j:FFN_CONV - j, lo:lo + FF_TILE]
                         * pltpu.roll(ext, j, axis=0)[CARRY_ROWS:])
        return out

    for ci in range(dff // FF_TILE):
        lo = ci * FF_TILE
        act = (_silu(conv_cols(dff + lo)) * conv_cols(lo)).astype(BF16)
        part = jnp.dot(act, wd_ref[lo:lo + FF_TILE, :], preferred_element_type=F32)
        if ci == 0:
            acc_ref[...] = h + part
        else:
            acc_ref[...] += part
    out = acc_ref[...]
    o_ref[...] = _rms(out, gf_ref[...]) if final else out


def _ffn(h, g, wu, cw, cb, wd, gf, final):
    b, t, d = h.shape
    tm = TOKEN_TILE
    blk = pl.BlockSpec((None, tm, d), lambda i, j: (i, j, 0))
    return pl.pallas_call(
        functools.partial(_ffn_kernel, final=final),
        grid=(b, t // tm),
        in_specs=[blk, _const_spec((1, d)), _const_spec(wu.shape, True), _const_spec(cw.shape),
                  _const_spec(cb.shape), _const_spec(wd.shape, True), _const_spec((1, d))],
        out_specs=blk,
        out_shape=jax.ShapeDtypeStruct((b, t, d), F32),
        scratch_shapes=[pltpu.VMEM((CARRY_ROWS, wu.shape[1]), F32), pltpu.VMEM((tm, d), F32)],
        compiler_params=_params("parallel", "arbitrary"),
        name="convffn",
    )(h, g, wu, cw, cb, wd, gf)


def _rope_tables(seq):
    half = HEAD_DIM // 2
    inv_freq = ROPE_THETA ** (-jnp.arange(half, dtype=F32) / half)
    ang = jnp.arange(seq).astype(F32)[:, None] * inv_freq[None, :]
    cos, sin = jnp.cos(ang), jnp.sin(ang)
    return jnp.concatenate([cos, cos, cos, cos], axis=1), jnp.concatenate([-sin, sin, -sin, sin], axis=1)


def _swap_head_halves(w):
    d, n = w.shape
    return w.reshape(d, n // HEAD_DIM, 2, HEAD_DIM // 2)[:, :, ::-1, :].reshape(d, n)


def _split_w_in(w):
    d = w.shape[0]
    o = 0
    wa = w[:, o:o + RWKV_IN]; o += RWKV_IN
    wb = w[:, o:o + 3 * ATTN_WIDTH]; o += 3 * ATTN_WIDTH
    wc = w[:, o:o + 4 * WIDTH + 2 * N_HEADS]; o += 4 * WIDTH + 2 * N_HEADS
    wd = w[:, o:o + HGRN_IN]; o += HGRN_IN
    wg = w[:, o:]
    wqk = wb[:, :2 * ATTN_WIDTH]
    wc_pad = jnp.concatenate([wc, jnp.zeros((d, MLSTM_IN_PAD - wc.shape[1]), w.dtype)], axis=1)
    w_cat = jnp.concatenate([wa, wb, _swap_head_halves(wqk), wc_pad, wd], axis=1)
    return w_cat.astype(BF16), wg.astype(BF16)


def kernel(x, norm_mix_g, w_in, b_gate, rwkv_mu, rwkv_w0, rwkv_w2, rwkv_a0, rwkv_a2, rwkv_g2, rwkv_k_k, rwkv_k_a, rwkv_r_k, rwkv_ln_g, rwkv_ln_b, mlstm_conv_w, mlstm_conv_b, mlstm_i_b, mlstm_f_b, hgrn_lb_logits, hgrn_norm_g, p_rwkv, p_attn, p_mlstm, p_hgrn, w_out, norm_ffn_g, w_up, ffn_conv_w, ffn_conv_b, w_down, final_norm_g):
    bsz, seq, d = x.shape
    depth = w_in.shape[0]
    n = bsz * seq
    cos_t, sin_t = _rope_tables(seq)
    h = x.reshape(n, d)
    for l in range(depth):
        w_cat, wg = _split_w_in(w_in[l])
        za, zb, zc, zd = _inproj(h, norm_mix_g[l].reshape(1, d), w_cat, cos_t, sin_t, seq)
        rwkv_p = dict(mu=rwkv_mu[l], w0=rwkv_w0[l], w2=rwkv_w2[l], a0=rwkv_a0[l], a2=rwkv_a2[l],
                      g2=rwkv_g2[l], k_k=rwkv_k_k[l], k_a=rwkv_k_a[l], r_k=rwkv_r_k[l],
                      ln_g=rwkv_ln_g[l], ln_b=rwkv_ln_b[l])
        ya = _rwkv(za.reshape(bsz, seq, -1), rwkv_p)
        yb = _attention(zb.reshape(bsz, seq, -1))
        yc = _mlstm(zc.reshape(bsz, seq, -1),
                    dict(conv_w=mlstm_conv_w[l], conv_b=mlstm_conv_b[l], i_b=mlstm_i_b[l], f_b=mlstm_f_b[l]))
        yd = _hgrn(zd.reshape(bsz, seq, -1), hgrn_lb_logits, hgrn_norm_g[l], l)
        ys = [y.reshape(n, -1) for y in (ya, yb, yc, yd)]
        ps = [p[l].astype(BF16) for p in (p_rwkv, p_attn, p_mlstm, p_hgrn)]
        h = _merge(h, norm_mix_g[l].reshape(1, d), wg, b_gate[l].reshape(1, -1), ys, ps,
                   w_out[l].astype(BF16))
        h = _ffn(h.reshape(bsz, seq, d), norm_ffn_g[l].reshape(1, d), w_up[l].astype(BF16),
                 ffn_conv_w[l], ffn_conv_b[l].reshape(1, -1), w_down[l].astype(BF16),
                 final_norm_g.reshape(1, d), l == depth - 1).reshape(n, d)
    return h.reshape(bsz, seq, d)
```

```python
import functools

import jax
import jax.numpy as jnp
from jax import lax
from jax.experimental import pallas as pl
from jax.experimental.pallas import tpu as pltpu

F32 = jnp.float32
BF16 = jnp.bfloat16

HEAD_DIM = 64
N_HEADS = 4
WIDTH = N_HEADS * HEAD_DIM
CHUNK = 64
RWKV_LORA = 128
RWKV_IN = 3 * WIDTH + RWKV_LORA
RWKV_GN_EPS = 64e-5
ATTN_GROUPS = ((128, 1), (512, 4), (2048, 16))
ATTN_HEADS = 6
ATTN_WIDTH = ATTN_HEADS * HEAD_DIM
ATTN_OUT = 2 * HEAD_DIM
ROPE_THETA = 10000.0
MLSTM_CONV = 4
MLSTM_IN_PAD = 4 * WIDTH + 256
HGRN_IN = 4 * WIDTH
N_BRANCHES = 4
FFN_CONV = 3
NORM_EPS = 1e-6
NEG_INF = -1e30

TOKEN_TILE = 512
SEQ_TILE = 256
FF_TILE = 256
CARRY_ROWS = 8
VMEM_LIMIT = 56 * 1024 * 1024


def _dot(a, b):
    return jnp.dot(a.astype(BF16), b.astype(BF16), preferred_element_type=F32)


def _dot_nt(a, b):
    return lax.dot_general(a.astype(BF16), b.astype(BF16), (((1,), (1,)), ((), ())),
                           preferred_element_type=F32)


def _dot_tn(a, b):
    return lax.dot_general(a.astype(BF16), b.astype(BF16), (((0,), (0,)), ((), ())),
                           preferred_element_type=F32)


def _split3(x):
    hi = x.astype(BF16)
    r1 = x - hi.astype(F32)
    mid = r1.astype(BF16)
    lo = (r1 - mid.astype(F32)).astype(BF16)
    return hi, mid, lo


def _sel_dot(m01, x):
    hi, mid, lo = _split3(x)
    return (jnp.dot(m01, hi, preferred_element_type=F32)
            + jnp.dot(m01, mid, preferred_element_type=F32)
            + jnp.dot(m01, lo, preferred_element_type=F32))


def _dot_sel(x, m01):
    hi, mid, lo = _split3(x)
    return (jnp.dot(hi, m01, preferred_element_type=F32)
            + jnp.dot(mid, m01, preferred_element_type=F32)
            + jnp.dot(lo, m01, preferred_element_type=F32))


def _iota(shape, dim):
    return lax.broadcasted_iota(jnp.int32, shape, dim)


def _onehot(cond):
    return jnp.where(cond, 1.0, 0.0).astype(BF16)


def _chunk_cumsum_matrix(n):
    r, c = _iota((n, n), 0), _iota((n, n), 1)
    return _onehot((jnp.right_shift(r, 6) == jnp.right_shift(c, 6)) & (c <= r))


def _chunk_total_matrix(n):
    r, c = _iota((n, n), 0), _iota((n, n), 1)
    return _onehot(jnp.right_shift(r, 6) == jnp.right_shift(c, 6))


def _head_sum_matrix(n):
    r, c = _iota((n, n), 0), _iota((n, n), 1)
    return _onehot(jnp.right_shift(r, 6) == jnp.right_shift(c, 6))


def _shift_rows(x, carry, j):
    ext = jnp.concatenate([carry, x], axis=0)
    return pltpu.roll(ext, j, axis=0)[CARRY_ROWS:]


def _softplus(x):
    return jnp.maximum(x, 0.0) + jnp.log1p(jnp.exp(-jnp.abs(x)))


def _sigmoid(x):
    return jax.nn.sigmoid(x)


def _silu(x):
    return x * _sigmoid(x)


def _rms(x, g):
    return x * lax.rsqrt(jnp.mean(x * x, axis=-1, keepdims=True) + NORM_EPS) * g


def _const_spec(shape, single_buffer=False):
    nd = len(shape)
    if single_buffer:
        return pl.BlockSpec(shape, lambda *_: (0,) * nd, pipeline_mode=pl.Buffered(1))
    return pl.BlockSpec(shape, lambda *_: (0,) * nd)


def _params(*sem):
    return pltpu.CompilerParams(dimension_semantics=sem, vmem_limit_bytes=VMEM_LIMIT)


def _inproj_kernel(x_ref, g_ref, w_ref, cos_ref, sin_ref, za_ref, zb_ref, zc_ref, zd_ref):
    xb = _rms(x_ref[...], g_ref[...]).astype(BF16)

    def mm(lo, width):
        return jnp.dot(xb, w_ref[:, lo:lo + width], preferred_element_type=F32)

    za_ref[...] = mm(0, RWKV_IN)
    o = RWKV_IN
    qk = mm(o, 2 * ATTN_WIDTH)
    vv = mm(o + 2 * ATTN_WIDTH, ATTN_WIDTH)
    qk_sw = mm(o + 3 * ATTN_WIDTH, 2 * ATTN_WIDTH)
    cos = jnp.concatenate([cos_ref[...]] * (2 * ATTN_WIDTH // 128), axis=1)
    sin = jnp.concatenate([sin_ref[...]] * (2 * ATTN_WIDTH // 128), axis=1)
    roped = qk * cos + qk_sw * sin
    zb_ref[:, 0:ATTN_WIDTH] = roped[:, 0:ATTN_WIDTH] * (HEAD_DIM ** -0.5)
    zb_ref[:, ATTN_WIDTH:2 * ATTN_WIDTH] = roped[:, ATTN_WIDTH:]
    zb_ref[:, 2 * ATTN_WIDTH:] = vv
    o += 5 * ATTN_WIDTH
    zc_ref[...] = mm(o, MLSTM_IN_PAD)
    o += MLSTM_IN_PAD
    zd_ref[...] = mm(o, HGRN_IN)


def _inproj(h, g, w_cat, cos_t, sin_t, seq):
    n, d = h.shape
    tm = TOKEN_TILE
    nt = seq // tm
    ncols = w_cat.shape[1]
    widths = (RWKV_IN, 3 * ATTN_WIDTH, MLSTM_IN_PAD, HGRN_IN)
    row = lambda w: pl.BlockSpec((tm, w), lambda i: (i, 0))
    return pl.pallas_call(
        _inproj_kernel,
        grid=(n // tm,),
        in_specs=[row(d), _const_spec((1, d)), _const_spec((d, ncols), True),
                  pl.BlockSpec((tm, 128), lambda i: (i % nt, 0)),
                  pl.BlockSpec((tm, 128), lambda i: (i % nt, 0))],
        out_specs=[row(w) for w in widths],
        out_shape=[jax.ShapeDtypeStruct((n, w), F32) for w in widths],
        compiler_params=_params("parallel"),
        name="inproj",
    )(h, g, w_cat, cos_t, sin_t)


def _blockdiag(x, bd_mask):
    xb = x.astype(BF16)
    return jnp.concatenate([xb] * N_HEADS, axis=0) * bd_mask


def _tri_inverse(a, bd_mask, same16, same32, eye):
    bd = lambda m: [_blockdiag(x, bd_mask) for x in m]
    mm = lambda xs, ys: [_dot(x, y) for x, y in zip(xs, ys)]
    n1 = [jnp.where(same16, x, 0.0) for x in a]
    n2 = mm(n1, bd(n1))
    n4 = mm(n2, bd(n2))
    n8 = mm(n4, bd(n4))
    t = [eye + x for x in n1]
    for p in (n2, n4, n8):
        t = [x + y for x, y in zip(t, mm(t, bd(p)))]
    for blk in (jnp.logical_and(same32, jnp.logical_not(same16)), jnp.logical_not(same32)):
        ab = bd([jnp.where(blk, x, 0.0) for x in a])
        t = [x + y for x, y in zip(t, mm(mm(t, ab), bd(t)))]
    return t


def _rwkv_kernel(z_ref, mu_ref, w0_ref, w2_ref, a0_ref, a2_ref, g2_ref, kk_ref, ka_ref, rk_ref,
                 lng_ref, lnb_ref, y_ref, carry_ref, s_ref):
    tt = z_ref.shape[0]

    @pl.when(pl.program_id(1) == 0)
    def _():
        carry_ref[...] = jnp.zeros_like(carry_ref)
        s_ref[...] = jnp.zeros_like(s_ref)

    z = z_ref[...]
    zprev = _shift_rows(z, carry_ref[...], 1)
    carry_ref[...] = z[tt - CARRY_ROWS:, :]
    z = z + mu_ref[...] * (zprev - z)
    r, k, v, lora = z[:, 0:WIDTH], z[:, WIDTH:2 * WIDTH], z[:, 2 * WIDTH:3 * WIDTH], z[:, 3 * WIDTH:]

    logw = -_softplus(-(w0_ref[...] + _dot(jnp.tanh(lora), w2_ref[...]))) - 0.5
    ld = -jnp.exp(logw)
    a = _sigmoid(a0_ref[...] + _dot(lora, a2_ref[...]))
    g = _dot(_sigmoid(lora), g2_ref[...])

    hsum = _head_sum_matrix(WIDTH)
    kk = k * kk_ref[...]
    kk = kk * lax.rsqrt(_dot_sel(kk * kk, hsum) + 1e-12)
    k = k * (1.0 + (a - 1.0) * ka_ref[...])
    sa, sb = -kk, kk * a

    cs = _sel_dot(_chunk_cumsum_matrix(tt), ld)
    tot = _sel_dot(_chunk_total_matrix(tt), ld)
    g_inc, g_exc, g_inv, g_end = jnp.exp(cs), jnp.exp(cs - ld), jnp.exp(-cs), jnp.exp(tot - cs)
    g_tot = jnp.exp(tot)
    rt, at = r * g_inc, sa * g_exc
    bt, kt = sb * g_inv, k * g_inv
    bh, kh = sb * g_end, k * g_end

    n = CHUNK
    nc = tt // n
    bd_mask = hsum
    bd_mask_f = bd_mask.astype(F32)
    ri, ci = _iota((n, WIDTH), 0), _iota((n, WIDTH), 1) & (n - 1)
    strict, incl, eye = ci < ri, ci <= ri, jnp.where(ci == ri, 1.0, 0.0)
    same16 = jnp.right_shift(ri, 4) == jnp.right_shift(ci, 4)
    same32 = jnp.right_shift(ri, 5) == jnp.right_shift(ci, 5)
    rows = lambda x: [x[c * n:(c + 1) * n] for c in range(nc)]
    bd = lambda xs: [_blockdiag(x, bd_mask) for x in xs]
    at_c, rt_c, v_c = rows(at), rows(rt), rows(v)

    lhs = [jnp.concatenate([x, y], axis=0) for x, y in zip(at_c, rt_c)]
    gb = [_dot_nt(x, y) for x, y in zip(lhs, bd(rows(bt)))]
    gk = [_dot_nt(x, y) for x, y in zip(lhs, bd(rows(kt)))]
    a_ab = [jnp.where(strict, x[:n], 0.0) for x in gb]
    a_rb = [jnp.where(incl, x[n:], 0.0) for x in gb]
    a_ak = [jnp.where(strict, x[:n], 0.0) for x in gk]
    a_rk = [jnp.where(incl, x[n:], 0.0) for x in gk]
    tinv = _tri_inverse(a_ab, bd_mask, same16, same32, eye)
    v_bd = bd(v_c)
    akv = [_dot(x, y) for x, y in zip(a_ak, v_bd)]
    w12 = [_dot(t, jnp.concatenate([x, y], axis=1)) for t, x, y in zip(tinv, bd(at_c), bd(akv))]
    w1 = [x[:, :WIDTH] for x in w12]
    w2 = [x[:, WIDTH:] for x in w12]
    q12 = [_dot(x, jnp.concatenate([y, u], axis=1)) for x, y, u in zip(a_rb, bd(w1), bd(w2))]
    q1 = [x + y[:, :WIDTH] for x, y in zip(rt_c, q12)]
    q2 = [_dot(x, y) + u[:, WIDTH:] for x, y, u in zip(a_rk, v_bd, q12)]
    bh_c, kh_c = rows(bh), rows(kh)
    p1 = [_dot_tn(x, y) * bd_mask_f for x, y in zip(w1, bh_c)]
    p2 = []
    for x, u, y, w in zip(w2, v_c, bh_c, kh_c):
        full = _dot_tn(jnp.concatenate([x, u], axis=0), jnp.concatenate([y, w], axis=0)) * bd_mask_f
        p2.append(full[0:n] + full[n:2 * n] + full[2 * n:3 * n] + full[3 * n:4 * n])

    s = s_ref[...]
    y_rows = []
    for c in range(nc):
        y_rows.append(_dot_nt(q1[c], _blockdiag(s, bd_mask)) + q2[c])
        s = s * g_tot[c * n:c * n + 1, :] + _dot(s, p1[c]) + p2[c]
    s_ref[...] = s
    y = jnp.concatenate(y_rows, axis=0)

    mean = _dot_sel(y, hsum) * (1.0 / HEAD_DIM)
    yc = y - mean
    var = _dot_sel(yc * yc, hsum) * (1.0 / HEAD_DIM)
    yn = yc * lax.rsqrt(var + RWKV_GN_EPS) * lng_ref[...] + lnb_ref[...]
    bonus = _dot_sel(r * k * rk_ref[...], hsum) * v
    y_ref[...] = (yn + bonus) * g


def _rwkv(z, p):
    b, t, _ = z.shape
    tt = SEQ_TILE
    vec = lambda a: a.reshape(1, -1)
    pad = lambda a, lo: jnp.zeros((RWKV_LORA, WIDTH), F32).at[lo:lo + a.shape[0]].set(a).astype(BF16)
    args = (z, vec(p["mu"]), vec(p["w0"]), pad(p["w2"], 0), vec(p["a0"]), pad(p["a2"], 32),
            pad(p["g2"], 64), vec(p["k_k"]), vec(p["k_a"]), vec(p["r_k"]), vec(p["ln_g"]), vec(p["ln_b"]))
    in_specs = [pl.BlockSpec((None, tt, RWKV_IN), lambda i, j: (i, j, 0))]
    in_specs += [_const_spec(a.shape) for a in args[1:]]
    return pl.pallas_call(
        _rwkv_kernel,
        grid=(b, t // tt),
        in_specs=in_specs,
        out_specs=pl.BlockSpec((None, tt, WIDTH), lambda i, j: (i, j, 0)),
        out_shape=jax.ShapeDtypeStruct((b, t, WIDTH), F32),
        scratch_shapes=[pltpu.VMEM((CARRY_ROWS, RWKV_IN), F32),
                        pltpu.VMEM((HEAD_DIM, WIDTH), F32)],
        compiler_params=_params("parallel", "arbitrary"),
        name="rwkv7",
    )(*args)


ATTN_BLOCK = 128
ATTN_UNITS = 8


def _attn_group(q_ref, k_ref, v_ref, og_ref, lg_ref, dil, first):
    t = q_ref.shape[0]
    nblk = t // dil // ATTN_BLOCK
    has_prev = nblk > 1
    qi, kj = _iota((ATTN_BLOCK, ATTN_BLOCK), 0), _iota((ATTN_BLOCK, ATTN_BLOCK), 1)
    cur_ok, prev_ok = kj <= qi, kj >= qi
    second = jnp.logical_not(first)

    def step(i, carry):
        rows, prevs, pmasks = [], [], []
        for x in range(ATTN_UNITS):
            u = i * ATTN_UNITS + x
            r, nb = u // nblk, u % nblk
            start = r + nb * (ATTN_BLOCK * dil)
            rows.append(pl.ds(start, ATTN_BLOCK, stride=dil))
            prevs.append(pl.ds(jnp.maximum(start - ATTN_BLOCK * dil, r), ATTN_BLOCK, stride=dil))
            pmasks.append(jnp.logical_and(prev_ok, nb > 0))
        q2 = [q_ref[rw, :] for rw in rows]
        kc = [k_ref[rw, :].astype(BF16) for rw in rows]
        vc = [v_ref[rw, :].astype(BF16) for rw in rows]
        if has_prev:
            kp = [k_ref[rw, :].astype(BF16) for rw in prevs]
            vc = [jnp.concatenate([v_ref[pw, :].astype(BF16), x], axis=0) for pw, x in zip(prevs, vc)]
        heads = []
        for sel in (first, second):
            qj = [jnp.where(sel, x, 0.0) for x in q2]
            s = [jnp.where(cur_ok, _dot_nt(x, y), NEG_INF) for x, y in zip(qj, kc)]
            if has_prev:
                sp = [jnp.where(pm, _dot_nt(x, y), NEG_INF) for pm, x, y in zip(pmasks, qj, kp)]
                s = [jnp.concatenate([x, y], axis=1) for x, y in zip(sp, s)]
            m = [jnp.max(x, axis=-1, keepdims=True) for x in s]
            e = [jnp.exp(x - y) for x, y in zip(s, m)]
            l = [jnp.sum(x, axis=-1, keepdims=True) for x in e]
            o = [_dot(x, y) / z for x, y, z in zip(e, vc, l)]
            lse = [jnp.broadcast_to(x + jnp.log(y), (ATTN_BLOCK, ATTN_OUT)) for x, y in zip(m, l)]
            heads.append((o, lse))
        for x in range(ATTN_UNITS):
            og_ref[rows[x], :] = jnp.where(first, heads[0][0][x], heads[1][0][x])
            lg_ref[rows[x], :] = jnp.where(first, heads[0][1][x], heads[1][1][x])
        return carry

    lax.fori_loop(0, t // ATTN_BLOCK // ATTN_UNITS, step, 0)


def _attn_kernel(q0, k0, v0, q1, k1, v1, q2, k2, v2, o_ref, og0, og1, og2, lg0, lg1, lg2):
    first = _iota((ATTN_BLOCK, ATTN_OUT), 1) < HEAD_DIM
    groups = ((q0, k0, v0, og0, lg0), (q1, k1, v1, og1, lg1), (q2, k2, v2, og2, lg2))
    for (win, dil), refs in zip(ATTN_GROUPS, groups):
        _attn_group(*refs, dil, first)
    l0, l1, l2 = lg0[...], lg1[...], lg2[...]
    mx = jnp.maximum(jnp.maximum(l0, l1), l2)
    w0, w1, w2 = jnp.exp(l0 - mx), jnp.exp(l1 - mx), jnp.exp(l2 - mx)
    o_ref[...] = (w0 * og0[...] + w1 * og1[...] + w2 * og2[...]) / (w0 + w1 + w2)


def _attention(zb):
    b, t, _ = zb.shape
    for win, dil in ATTN_GROUPS:
        assert win // dil == ATTN_BLOCK and t % (dil * ATTN_BLOCK) == 0 and t % (ATTN_BLOCK * ATTN_UNITS) == 0
    col = lambda blk: pl.BlockSpec((None, t, ATTN_OUT), lambda i, blk=blk: (i, 0, blk))
    ng = len(ATTN_GROUPS)
    in_specs = []
    for g in range(ng):
        in_specs += [col(g), col(ng + g), col(2 * ng + g)]
    return pl.pallas_call(
        _attn_kernel,
        grid=(b,),
        in_specs=in_specs,
        out_specs=pl.BlockSpec((None, t, ATTN_OUT), lambda i: (i, 0, 0)),
        out_shape=jax.ShapeDtypeStruct((b, t, ATTN_OUT), F32),
        scratch_shapes=[pltpu.VMEM((t, ATTN_OUT), F32)] * (2 * ng),
        compiler_params=_params("parallel"),
        name="dilated_attention",
    )(*([zb] * (3 * ng)))


def _mlstm_kernel(z_ref, cw_ref, cb_ref, ib_ref, fb_ref, y_ref, carry_ref, c_ref, m_ref):
    tt = z_ref.shape[0]

    @pl.when(pl.program_id(1) == 0)
    def _():
        carry_ref[...] = jnp.zeros_like(carry_ref)
        c_ref[...] = jnp.zeros_like(c_ref)
        m_ref[...] = jnp.zeros_like(m_ref)

    qk_in = z_ref[:, 0:2 * WIDTH]
    ext = jnp.concatenate([carry_ref[...], qk_in], axis=0)
    acc = cb_ref[...] + cw_ref[MLSTM_CONV - 1:MLSTM_CONV, :] * qk_in
    for j in range(1, MLSTM_CONV):
        acc = acc + cw_ref[MLSTM_CONV - 1 - j:MLSTM_CONV - j, :] * pltpu.roll(ext, j, axis=0)[CARRY_ROWS:]
    carry_ref[...] = qk_in[tt - CARRY_ROWS:, :]
    qk = _silu(acc)
    q, k = qk[:, :WIDTH], qk[:, WIDTH:] * (HEAD_DIM ** -0.5)
    v = z_ref[:, 2 * WIDTH:3 * WIDTH]
    og = _sigmoid(z_ref[:, 3 * WIDTH:4 * WIDTH])

    n = CHUNK
    nc = tt // n
    gi = z_ref[:, 4 * WIDTH:4 * WIDTH + 128] + ib_ref[...]
    lf = -_softplus(-(z_ref[:, 4 * WIDTH + 128:] + fb_ref[...]))
    bcs = _sel_dot(_chunk_cumsum_matrix(tt), lf)
    u = gi - bcs
    row_in_chunk = _iota((tt, 128), 0) & (n - 1)
    cmax = u
    for sh in (1, 2, 4, 8, 16, 32):
        cmax = jnp.where(row_in_chunk >= sh, jnp.maximum(cmax, pltpu.roll(cmax, sh, axis=0)), cmax)
    m_prev = m_ref[0:1, :]
    m_rows, scal_rows = [], []
    for c in range(nc):
        last = c * n + n - 1
        b_end = bcs[last:last + 1, :]
        m_new = b_end + jnp.maximum(m_prev, cmax[last:last + 1, :])
        m_rows.append(bcs[c * n:(c + 1) * n, :] + jnp.maximum(m_prev, cmax[c * n:(c + 1) * n, :]))
        scal_rows.append(jnp.concatenate([m_prev, b_end - m_new, jnp.exp(b_end + m_prev - m_new),
                                          jnp.zeros((CARRY_ROWS - 3, 128), F32)], axis=0))
        m_prev = m_new
    m_ref[0:1, :] = m_prev
    m_t = jnp.concatenate(m_rows, axis=0)

    expand = _onehot(_iota((128, WIDTH), 0) == jnp.right_shift(_iota((128, WIDTH), 1), 6))
    bt_x, mt_x, uc_x = _dot_sel(bcs, expand), _dot_sel(m_t, expand), _dot_sel(u, expand)
    scal_x = _dot_sel(jnp.concatenate(scal_rows, axis=0), expand)
    u_t = u.T

    hsum = _head_sum_matrix(WIDTH)
    hsum2 = jnp.concatenate([hsum, hsum], axis=1)
    bd_mask2_f = hsum2.astype(F32)
    ones = jnp.ones((n, WIDTH), BF16)
    ri, ci = _iota((n, WIDTH), 0), _iota((n, WIDTH), 1) & (n - 1)
    causal = ci <= ri
    rows = lambda x: [x[c * n:(c + 1) * n] for c in range(nc)]
    q_c, k_c, v_c, bt_c, mt_c, uc_c = rows(q), rows(k), rows(v), rows(bt_x), rows(mt_x), rows(uc_x)

    scores, upds = [], []
    for c in range(nc):
        u_row = jnp.concatenate([u_t[h:h + 1, c * n:(c + 1) * n] for h in range(N_HEADS)], axis=1)
        d_intra = jnp.where(causal, bt_c[c] + u_row, -jnp.inf)
        scores.append(_dot_nt(q_c[c], _blockdiag(k_c[c], hsum)) * jnp.exp(d_intra - mt_c[c]))
        w_k = jnp.exp(uc_c[c] + scal_x[8 * c + 1:8 * c + 2, :])
        upds.append(_dot_tn(k_c[c] * w_k, jnp.concatenate([v_c[c].astype(BF16), ones], axis=1)) * bd_mask2_f)
    intra = [_dot(s, jnp.concatenate([_blockdiag(x, hsum), hsum], axis=1)) for s, x in zip(scores, v_c)]

    state = c_ref[...]
    y_rows = []
    for c in range(nc):
        inter = _dot(q_c[c], state)
        w_inter = jnp.exp(bt_c[c] + scal_x[8 * c:8 * c + 1, :] - mt_c[c])
        num = w_inter * inter[:, :WIDTH] + intra[c][:, :WIDTH]
        den = w_inter * inter[:, WIDTH:] + intra[c][:, WIDTH:]
        y_rows.append(num / jnp.maximum(jnp.abs(den), jnp.exp(-mt_c[c])))
        dec = scal_x[8 * c + 2:8 * c + 3, :]
        state = jnp.concatenate([dec, dec], axis=1) * state + upds[c]
    c_ref[...] = state
    y_ref[...] = og * jnp.concatenate(y_rows, axis=0)


def _mlstm(z, p):
    b, t, _ = z.shape
    tt = SEQ_TILE
    lane_pad = lambda a: jnp.zeros((1, 128), F32).at[0, 0:N_HEADS].set(a)
    args = (z, p["conv_w"], p["conv_b"].reshape(1, -1), lane_pad(p["i_b"]), lane_pad(p["f_b"]))
    in_specs = [pl.BlockSpec((None, tt, MLSTM_IN_PAD), lambda i, j: (i, j, 0))]
    in_specs += [_const_spec(a.shape) for a in args[1:]]
    return pl.pallas_call(
        _mlstm_kernel,
        grid=(b, t // tt),
        in_specs=in_specs,
        out_specs=pl.BlockSpec((None, tt, WIDTH), lambda i, j: (i, j, 0)),
        out_shape=jax.ShapeDtypeStruct((b, t, WIDTH), F32),
        scratch_shapes=[pltpu.VMEM((CARRY_ROWS, 2 * WIDTH), F32),
                        pltpu.VMEM((WIDTH, 2 * WIDTH), F32),
                        pltpu.VMEM((8, 128), F32)],
        compiler_params=_params("parallel", "arbitrary"),
        name="mlstm",
    )(*args)


def _hgrn_kernel(z_ref, lbl_ref, ng_ref, y_ref, s_ref, sel_ref, *, layer):
    tt = z_ref.shape[0]

    @pl.when(pl.program_id(1) == 0)
    def _():
        s_ref[...] = jnp.zeros_like(s_ref)

    logits = lbl_ref[...]
    pe = jnp.exp(logits - jnp.max(logits, axis=0, keepdims=True))
    pr = pe / jnp.sum(pe, axis=0, keepdims=True)
    lb = pr[0:1, :]
    for i in range(1, layer + 1):
        lb = lb + pr[i:i + 1, :]
    lb = lb - pr[0:1, :]

    q = _silu(z_ref[:, 0:WIDTH])
    f = lb + (1.0 - lb) * _sigmoid(z_ref[:, WIDTH:2 * WIDTH])
    k = 1.0 - f
    v = z_ref[:, 2 * WIDTH:3 * WIDTH]
    og = _sigmoid(z_ref[:, 3 * WIDTH:4 * WIDTH])
    gl = jnp.log(f)

    @pl.when(pl.program_id(1) == 0)
    def _():
        r, c = _iota((tt, tt), 0), _iota((tt, tt), 1)
        sel_ref[0] = _chunk_cumsum_matrix(tt)
        sel_ref[1] = _onehot(c == (jnp.right_shift(r, 6) * CHUNK + CHUNK - 1))
        for lv in range(1, 7):
            sel_ref[lv + 1] = _onehot(c == jnp.right_shift(r, lv) * (1 << lv) + (1 << (lv - 1)) - 1)

    bsum = _sel_dot(sel_ref[0], gl)
    b_end = _sel_dot(sel_ref[1], bsum)
    q_in = q * jnp.exp(bsum)
    k_end = k * jnp.exp(b_end - bsum)
    g_tot = jnp.exp(b_end)

    n = CHUNK
    nc = tt // n
    hsum = _head_sum_matrix(WIDTH)
    bd_mask_f = hsum.astype(F32)
    ri, ci = _iota((n, WIDTH), 0), _iota((n, WIDTH), 1) & (n - 1)
    row_t = _iota((tt, 1), 0)
    rows = lambda x: [x[i * n:(i + 1) * n] for i in range(nc)]
    bd = lambda xs: [_blockdiag(x, hsum) for x in xs]
    attn = [jnp.where(ri == ci, _dot_nt(x, y), 0.0) for x, y in zip(rows(q), bd(rows(k)))]
    bsum_b = bsum.astype(BF16)
    for lv in range(1, 7):
        size, half = 1 << lv, 1 << (lv - 1)
        b_mid = jnp.dot(sel_ref[lv + 1], bsum_b, preferred_element_type=F32)
        right = (row_t & (size - 1)) >= half
        q_l = jnp.where(right, q * jnp.exp(bsum - b_mid), 0.0)
        k_l = jnp.where(right, 0.0, k * jnp.exp(b_mid - bsum))
        same = jnp.right_shift(ri, lv) == jnp.right_shift(ci, lv)
        attn = [a + jnp.where(same, _dot_nt(x, y), 0.0) for a, x, y in zip(attn, rows(q_l), bd(rows(k_l)))]
    v_c = rows(v)
    intra = [_dot(a, y) for a, y in zip(attn, bd(v_c))]
    upd = []
    for x, y in zip(v_c, rows(k_end)):
        full = _dot_tn(x, y) * bd_mask_f
        upd.append(full[0:n] + full[n:2 * n] + full[2 * n:3 * n] + full[3 * n:4 * n])

    s = s_ref[...]
    y_rows = []
    q_in_c = rows(q_in)
    for i in range(nc):
        y_rows.append(_dot_nt(q_in_c[i], _blockdiag(s, hsum)) + intra[i])
        s = s * g_tot[i * n:i * n + 1, :] + upd[i]
    s_ref[...] = s
    o = jnp.concatenate(y_rows, axis=0)
    ms = _dot_sel(o * o, hsum) * (1.0 / HEAD_DIM)
    y_ref[...] = o * lax.rsqrt(ms + NORM_EPS) * ng_ref[...] * og


def _hgrn(z, lb_logits, norm_g, layer):
    b, t, _ = z.shape
    tt = SEQ_TILE
    return pl.pallas_call(
        functools.partial(_hgrn_kernel, layer=layer),
        grid=(b, t // tt),
        in_specs=[pl.BlockSpec((None, tt, HGRN_IN), lambda i, j: (i, j, 0)),
                  _const_spec(lb_logits.shape), _const_spec((1, WIDTH))],
        out_specs=pl.BlockSpec((None, tt, WIDTH), lambda i, j: (i, j, 0)),
        out_shape=jax.ShapeDtypeStruct((b, t, WIDTH), F32),
        scratch_shapes=[pltpu.VMEM((HEAD_DIM, WIDTH), F32), pltpu.VMEM((8, tt, tt), BF16)],
        compiler_params=_params("parallel", "arbitrary"),
        name="hgrn2",
    )(z, lb_logits, norm_g.reshape(1, -1))


def _merge_kernel(h_ref, g_ref, wg_ref, bg_ref, ya_ref, yb_ref, yc_ref, yd_ref,
                  pa_ref, pb_ref, pc_ref, pd_ref, wo_ref, o_ref):
    h = h_ref[...]
    d = h.shape[1]
    xb = _rms(h, g_ref[...]).astype(BF16)
    merged = None
    for i, (y_ref, p_ref) in enumerate(((ya_ref, pa_ref), (yb_ref, pb_ref), (yc_ref, pc_ref), (yd_ref, pd_ref))):
        gate = _sigmoid(jnp.dot(xb, wg_ref[:, i * d:(i + 1) * d], preferred_element_type=F32)
                        + bg_ref[:, i * d:(i + 1) * d])
        term = gate * jnp.dot(y_ref[...].astype(BF16), p_ref[...], preferred_element_type=F32)
        merged = term if merged is None else merged + term
    o_ref[...] = h + jnp.dot(merged.astype(BF16), wo_ref[...], preferred_element_type=F32)


def _merge(h, g, wg, bg, ys, ps, wo):
    n, d = h.shape
    tm = TOKEN_TILE
    row = lambda w: pl.BlockSpec((tm, w), lambda i: (i, 0))
    in_specs = [row(d), _const_spec((1, d)), _const_spec(wg.shape, True), _const_spec(bg.shape)]
    in_specs += [row(y.shape[1]) for y in ys]
    in_specs += [_const_spec(p.shape, True) for p in ps]
    in_specs += [_const_spec(wo.shape, True)]
    return pl.pallas_call(
        _merge_kernel,
        grid=(n // tm,),
        in_specs=in_specs,
        out_specs=row(d),
        out_shape=jax.ShapeDtypeStruct((n, d), F32),
        compiler_params=_params("parallel"),
        name="merge",
    )(h, g, wg, bg, *ys, *ps, wo)


def _ffn_kernel(h_ref, g_ref, wu_ref, cw_ref, cb_ref, wd_ref, gf_ref, o_ref, carry_ref, acc_ref, *, final):
    tm = h_ref.shape[0]
    dff = wd_ref.shape[0]

    @pl.when(pl.program_id(1) == 0)
    def _():
        carry_ref[...] = jnp.zeros_like(carry_ref)

    h = h_ref[...]
    xb = _rms(h, g_ref[...]).astype(BF16)

    def conv_cols(lo):
        up = jnp.dot(xb, wu_ref[:, lo:lo + FF_TILE], preferred_element_type=F32)
        ext = jnp.concatenate([carry_ref[:, lo:lo + FF_TILE], up], axis=0)
        carry_ref[:, lo:lo + FF_TILE] = up[tm - CARRY_ROWS:, :]
        out = cb_ref[:, lo:lo + FF_TILE] + cw_ref[FFN_CONV - 1:FFN_CONV, lo:lo + FF_TILE] * up
        for j in range(1, FFN_CONV):
            out = out + (cw_ref[FFN_CONV - 1 - j:FFN_CONV - j, lo:lo + FF_TILE]
                         * pltpu.roll(ext, j, axis=0)[CARRY_ROWS:])
        return out

    for ci in range(dff // FF_TILE):
        lo = ci * FF_TILE
        act = (_silu(conv_cols(dff + lo)) * conv_cols(lo)).astype(BF16)
        part = jnp.dot(act, wd_ref[lo:lo + FF_TILE, :], preferred_element_type=F32)
        if ci == 0:
            acc_ref[...] = h + part
        else:
            acc_ref[...] += part
    out = acc_ref[...]
    o_ref[...] = _rms(out, gf_ref[...]) if final else out


def _ffn(h, g, wu, cw, cb, wd, gf, final):
    b, t, d = h.shape
    tm = TOKEN_TILE
    blk = pl.BlockSpec((None, tm, d), lambda i, j: (i, j, 0))
    return pl.pallas_call(
        functools.partial(_ffn_kernel, final=final),
        grid=(b, t // tm),
        in_specs=[blk, _const_spec((1, d)), _const_spec(wu.shape, True), _const_spec(cw.shape),
                  _const_spec(cb.shape), _const_spec(wd.shape, True), _const_spec((1, d))],
        out_specs=blk,
        out_shape=jax.ShapeDtypeStruct((b, t, d), F32),
        scratch_shapes=[pltpu.VMEM((CARRY_ROWS, wu.shape[1]), F32), pltpu.VMEM((tm, d), F32)],
        compiler_params=_params("parallel", "arbitrary"),
        name="convffn",
    )(h, g, wu, cw, cb, wd, gf)


def _rope_tables(seq):
    half = HEAD_DIM // 2
    inv_freq = ROPE_THETA ** (-jnp.arange(half, dtype=F32) / half)
    ang = jnp.arange(seq).astype(F32)[:, None] * inv_freq[None, :]
    cos, sin = jnp.cos(ang), jnp.sin(ang)
    return jnp.concatenate([cos, cos, cos, cos], axis=1), jnp.concatenate([-sin, sin, -sin, sin], axis=1)


def _swap_head_halves(w):
    d, n = w.shape
    return w.reshape(d, n // HEAD_DIM, 2, HEAD_DIM // 2)[:, :, ::-1, :].reshape(d, n)


def _split_w_in(w):
    d = w.shape[0]
    o = 0
    wa = w[:, o:o + RWKV_IN]; o += RWKV_IN
    wb = w[:, o:o + 3 * ATTN_WIDTH]; o += 3 * ATTN_WIDTH
    wc = w[:, o:o + 4 * WIDTH + 2 * N_HEADS]; o += 4 * WIDTH + 2 * N_HEADS
    wd = w[:, o:o + HGRN_IN]; o += HGRN_IN
    wg = w[:, o:]
    wqk = wb[:, :2 * ATTN_WIDTH]
    gate_pad = jnp.zeros((d, 128 - N_HEADS), w.dtype)
    wc_pad = jnp.concatenate([wc[:, :4 * WIDTH], wc[:, 4 * WIDTH:4 * WIDTH + N_HEADS], gate_pad,
                              wc[:, 4 * WIDTH + N_HEADS:], gate_pad], axis=1)
    w_cat = jnp.concatenate([wa, wb, _swap_head_halves(wqk), wc_pad, wd], axis=1)
    return w_cat.astype(BF16), wg.astype(BF16)


def kernel(x, norm_mix_g, w_in, b_gate, rwkv_mu, rwkv_w0, rwkv_w2, rwkv_a0, rwkv_a2, rwkv_g2, rwkv_k_k, rwkv_k_a, rwkv_r_k, rwkv_ln_g, rwkv_ln_b, mlstm_conv_w, mlstm_conv_b, mlstm_i_b, mlstm_f_b, hgrn_lb_logits, hgrn_norm_g, p_rwkv, p_attn, p_mlstm, p_hgrn, w_out, norm_ffn_g, w_up, ffn_conv_w, ffn_conv_b, w_down, final_norm_g):
    bsz, seq, d = x.shape
    depth = w_in.shape[0]
    n = bsz * seq
    cos_t, sin_t = _rope_tables(seq)
    h = x.reshape(n, d)
    for l in range(depth):
        w_cat, wg = _split_w_in(w_in[l])
        za, zb, zc, zd = _inproj(h, norm_mix_g[l].reshape(1, d), w_cat, cos_t, sin_t, seq)
        rwkv_p = dict(mu=rwkv_mu[l], w0=rwkv_w0[l], w2=rwkv_w2[l], a0=rwkv_a0[l], a2=rwkv_a2[l],
                      g2=rwkv_g2[l], k_k=rwkv_k_k[l], k_a=rwkv_k_a[l], r_k=rwkv_r_k[l],
                      ln_g=rwkv_ln_g[l], ln_b=rwkv_ln_b[l])
        ya = _rwkv(za.reshape(bsz, seq, -1), rwkv_p)
        yb = _attention(zb.reshape(bsz, seq, -1))
        yc = _mlstm(zc.reshape(bsz, seq, -1),
                    dict(conv_w=mlstm_conv_w[l], conv_b=mlstm_conv_b[l], i_b=mlstm_i_b[l], f_b=mlstm_f_b[l]))
        yd = _hgrn(zd.reshape(bsz, seq, -1), hgrn_lb_logits, hgrn_norm_g[l], l)
        ys = [y.reshape(n, -1) for y in (ya, yb, yc, yd)]
        ps = [p[l].astype(BF16) for p in (p_rwkv, p_attn, p_mlstm, p_hgrn)]
        h = _merge(h, norm_mix_g[l].reshape(1, d), wg, b_gate[l].reshape(1, -1), ys, ps,
                   w_out[l].astype(BF16))
        h = _ffn(h.reshape(bsz, seq, d), norm_ffn_g[l].reshape(1, d), w_up[l].astype(BF16),
                 ffn_conv_w[l], ffn_conv_b[l].reshape(1, -1), w_down[l].astype(BF16),
                 final_norm_g.reshape(1, d), l == depth - 1).reshape(n, d)
    return h.reshape(bsz, seq, d)
```

```python
import functools

import jax
import jax.numpy as jnp
from jax import lax
from jax.experimental import pallas as pl
from jax.experimental.pallas import tpu as pltpu

F32 = jnp.float32
BF16 = jnp.bfloat16

HEAD_DIM = 64
N_HEADS = 4
WIDTH = N_HEADS * HEAD_DIM
CHUNK = 64
BLOCK_HEADS = 2
BLOCK_W = BLOCK_HEADS * HEAD_DIM
N_BLOCKS = N_HEADS // BLOCK_HEADS
RWKV_LORA = 128
RWKV_IN = 3 * WIDTH + RWKV_LORA
RWKV_GN_EPS = 64e-5
ATTN_GROUPS = ((128, 1), (512, 4), (2048, 16))
ATTN_HEADS = 6
ATTN_WIDTH = ATTN_HEADS * HEAD_DIM
ATTN_OUT = 2 * HEAD_DIM
ROPE_THETA = 10000.0
MLSTM_CONV = 4
MLSTM_IN_PAD = 4 * WIDTH + 256
HGRN_IN = 4 * WIDTH
N_BRANCHES = 4
FFN_CONV = 3
NORM_EPS = 1e-6
NEG_INF = -1e30

TOKEN_TILE = 512
SEQ_TILE = 256
FF_TILE = 256
CARRY_ROWS = 8
VMEM_LIMIT = 56 * 1024 * 1024


def _dot(a, b):
    return jnp.dot(a.astype(BF16), b.astype(BF16), preferred_element_type=F32)


def _dot_nt(a, b):
    return lax.dot_general(a.astype(BF16), b.astype(BF16), (((1,), (1,)), ((), ())),
                           preferred_element_type=F32)


def _dot_tn(a, b):
    return lax.dot_general(a.astype(BF16), b.astype(BF16), (((0,), (0,)), ((), ())),
                           preferred_element_type=F32)


def _split3(x):
    hi = x.astype(BF16)
    r1 = x - hi.astype(F32)
    mid = r1.astype(BF16)
    lo = (r1 - mid.astype(F32)).astype(BF16)
    return hi, mid, lo


def _sel_dot(m01, x):
    hi, mid, lo = _split3(x)
    return (jnp.dot(m01, hi, preferred_element_type=F32)
            + jnp.dot(m01, mid, preferred_element_type=F32)
            + jnp.dot(m01, lo, preferred_element_type=F32))


def _dot_sel(x, m01):
    hi, mid, lo = _split3(x)
    return (jnp.dot(hi, m01, preferred_element_type=F32)
            + jnp.dot(mid, m01, preferred_element_type=F32)
            + jnp.dot(lo, m01, preferred_element_type=F32))


def _dot_sel2(x, m01):
    hi = x.astype(BF16)
    lo = (x - hi.astype(F32)).astype(BF16)
    return jnp.dot(hi, m01, preferred_element_type=F32) + jnp.dot(lo, m01, preferred_element_type=F32)


def _iota(shape, dim):
    return lax.broadcasted_iota(jnp.int32, shape, dim)


def _onehot(cond):
    return jnp.where(cond, 1.0, 0.0).astype(BF16)


def _chunk_cumsum_matrix(n):
    r, c = _iota((n, n), 0), _iota((n, n), 1)
    return _onehot((jnp.right_shift(r, 6) == jnp.right_shift(c, 6)) & (c <= r))


def _head_sum_matrix(n):
    r, c = _iota((n, n), 0), _iota((n, n), 1)
    return _onehot(jnp.right_shift(r, 6) == jnp.right_shift(c, 6))


def _shift_rows(x, carry, j):
    ext = jnp.concatenate([carry, x], axis=0)
    return pltpu.roll(ext, j, axis=0)[CARRY_ROWS:]


def _softplus(x):
    return jnp.maximum(x, 0.0) + jnp.log1p(jnp.exp(-jnp.abs(x)))


def _sigmoid(x):
    return jax.nn.sigmoid(x)


def _silu(x):
    return x * _sigmoid(x)


def _rms(x, g):
    return x * lax.rsqrt(jnp.mean(x * x, axis=-1, keepdims=True) + NORM_EPS) * g


def _const_spec(shape, single_buffer=False):
    nd = len(shape)
    if single_buffer:
        return pl.BlockSpec(shape, lambda *_: (0,) * nd, pipeline_mode=pl.Buffered(1))
    return pl.BlockSpec(shape, lambda *_: (0,) * nd)


def _params(*sem):
    return pltpu.CompilerParams(dimension_semantics=sem, vmem_limit_bytes=VMEM_LIMIT)


def _inproj_kernel(x_ref, g_ref, w_ref, cos_ref, sin_ref, za_ref, zb_ref, zc_ref, zd_ref):
    xb = _rms(x_ref[...], g_ref[...]).astype(BF16)

    def mm(lo, width):
        return jnp.dot(xb, w_ref[:, lo:lo + width], preferred_element_type=F32)

    za_ref[...] = mm(0, RWKV_IN)
    o = RWKV_IN
    qk = mm(o, 2 * ATTN_WIDTH)
    vv = mm(o + 2 * ATTN_WIDTH, ATTN_WIDTH)
    qk_sw = mm(o + 3 * ATTN_WIDTH, 2 * ATTN_WIDTH)
    cos = jnp.concatenate([cos_ref[...]] * (2 * ATTN_WIDTH // 128), axis=1)
    sin = jnp.concatenate([sin_ref[...]] * (2 * ATTN_WIDTH // 128), axis=1)
    roped = qk * cos + qk_sw * sin
    zb_ref[:, 0:ATTN_WIDTH] = roped[:, 0:ATTN_WIDTH] * (HEAD_DIM ** -0.5)
    zb_ref[:, ATTN_WIDTH:2 * ATTN_WIDTH] = roped[:, ATTN_WIDTH:]
    zb_ref[:, 2 * ATTN_WIDTH:] = vv
    o += 5 * ATTN_WIDTH
    zc_ref[...] = mm(o, MLSTM_IN_PAD)
    o += MLSTM_IN_PAD
    zd_ref[...] = mm(o, HGRN_IN)


def _inproj(h, g, w_cat, cos_t, sin_t, seq):
    n, d = h.shape
    tm = TOKEN_TILE
    nt = seq // tm
    ncols = w_cat.shape[1]
    widths = (RWKV_IN, 3 * ATTN_WIDTH, MLSTM_IN_PAD, HGRN_IN)
    row = lambda w: pl.BlockSpec((tm, w), lambda i: (i, 0))
    return pl.pallas_call(
        _inproj_kernel,
        grid=(n // tm,),
        in_specs=[row(d), _const_spec((1, d)), _const_spec((d, ncols), True),
                  pl.BlockSpec((tm, 128), lambda i: (i % nt, 0)),
                  pl.BlockSpec((tm, 128), lambda i: (i % nt, 0))],
        out_specs=[row(w) for w in widths],
        out_shape=[jax.ShapeDtypeStruct((n, w), F32) for w in widths],
        compiler_params=_params("parallel"),
        name="inproj",
    )(h, g, w_cat, cos_t, sin_t)


def _blockdiag(x, bd_mask):
    xb = x.astype(BF16)
    return jnp.concatenate([xb] * BLOCK_HEADS, axis=0) * bd_mask


def _chunk_blocks(x):
    return [x[c * CHUNK:(c + 1) * CHUNK, b * BLOCK_W:(b + 1) * BLOCK_W]
            for c in range(x.shape[0] // CHUNK) for b in range(N_BLOCKS)]


def _fold_diag_blocks(full):
    out = full[0:HEAD_DIM]
    for h in range(1, BLOCK_HEADS):
        out = out + full[h * HEAD_DIM:(h + 1) * HEAD_DIM]
    return out


def _tri_inverse(a, bd_mask, same16, same32, eye):
    bd = lambda m: [_blockdiag(x, bd_mask) for x in m]
    mm = lambda xs, ys: [_dot(x, y) for x, y in zip(xs, ys)]
    n1 = [jnp.where(same16, x, 0.0) for x in a]
    n2 = mm(n1, bd(n1))
    n4 = mm(n2, bd(n2))
    n8 = mm(n4, bd(n4))
    t = [eye + x for x in n1]
    for p in (n2, n4, n8):
        t = [x + y for x, y in zip(t, mm(t, bd(p)))]
    for blk in (jnp.logical_and(same32, jnp.logical_not(same16)), jnp.logical_not(same32)):
        ab = bd([jnp.where(blk, x, 0.0) for x in a])
        t = [x + y for x, y in zip(t, mm(mm(t, ab), bd(t)))]
    return t


def _rwkv_kernel(z_ref, mu_ref, w0_ref, w2_ref, a0_ref, a2_ref, g2_ref, kk_ref, ka_ref, rk_ref,
                 lng_ref, lnb_ref, y_ref, carry_ref, s_ref):
    tt = z_ref.shape[0]

    @pl.when(pl.program_id(1) == 0)
    def _():
        carry_ref[...] = jnp.zeros_like(carry_ref)
        s_ref[...] = jnp.zeros_like(s_ref)

    z = z_ref[...]
    zprev = _shift_rows(z, carry_ref[...], 1)
    carry_ref[...] = z[tt - CARRY_ROWS:, :]
    z = z + mu_ref[...] * (zprev - z)
    r, k, v, lora = z[:, 0:WIDTH], z[:, WIDTH:2 * WIDTH], z[:, 2 * WIDTH:3 * WIDTH], z[:, 3 * WIDTH:]

    logw = -_softplus(-(w0_ref[...] + _dot(jnp.tanh(lora), w2_ref[...]))) - 0.5
    ld = -jnp.exp(logw)
    a = _sigmoid(a0_ref[...] + _dot(lora, a2_ref[...]))
    g = _dot(_sigmoid(lora), g2_ref[...])

    hsum = _head_sum_matrix(WIDTH)
    kk = k * kk_ref[...]
    kk = kk * lax.rsqrt(_dot_sel2(kk * kk, hsum) + 1e-12)
    k = k * (1.0 + (a - 1.0) * ka_ref[...])
    sa, sb = -kk, kk * a

    cs = _sel_dot(_chunk_cumsum_matrix(tt), ld)
    tot = jnp.concatenate([jnp.broadcast_to(cs[c * CHUNK + CHUNK - 1:(c + 1) * CHUNK, :], (CHUNK, WIDTH))
                           for c in range(tt // CHUNK)], axis=0)
    g_inc, g_exc, g_inv, g_end = jnp.exp(cs), jnp.exp(cs - ld), jnp.exp(-cs), jnp.exp(tot - cs)
    g_tot = jnp.exp(tot)
    rt, at = r * g_inc, sa * g_exc
    bt, kt = sb * g_inv, k * g_inv
    bh, kh = sb * g_end, k * g_end

    n = CHUNK
    nc = tt // n
    bd_mask = _head_sum_matrix(BLOCK_W)
    bd_mask_f = bd_mask.astype(F32)
    ri, ci = _iota((n, BLOCK_W), 0), _iota((n, BLOCK_W), 1) & (n - 1)
    strict, incl, eye = ci < ri, ci <= ri, jnp.where(ci == ri, 1.0, 0.0)
    same16 = jnp.right_shift(ri, 4) == jnp.right_shift(ci, 4)
    same32 = jnp.right_shift(ri, 5) == jnp.right_shift(ci, 5)
    bd = lambda xs: [_blockdiag(x, bd_mask) for x in xs]
    at_c, rt_c, v_c = _chunk_blocks(at), _chunk_blocks(rt), _chunk_blocks(v)

    lhs = [jnp.concatenate([x, y], axis=0) for x, y in zip(at_c, rt_c)]
    gb = [_dot_nt(x, y) for x, y in zip(lhs, bd(_chunk_blocks(bt)))]
    gk = [_dot_nt(x, y) for x, y in zip(lhs, bd(_chunk_blocks(kt)))]
    a_ab = [jnp.where(strict, x[:n], 0.0) for x in gb]
    a_rb = [jnp.where(incl, x[n:], 0.0) for x in gb]
    a_ak = [jnp.where(strict, x[:n], 0.0) for x in gk]
    a_rk = [jnp.where(incl, x[n:], 0.0) for x in gk]
    tinv = _tri_inverse(a_ab, bd_mask, same16, same32, eye)
    v_bd = bd(v_c)
    akv = [_dot(x, y) for x, y in zip(a_ak, v_bd)]
    w12 = [_dot(t, jnp.concatenate([x, y], axis=1)) for t, x, y in zip(tinv, bd(at_c), bd(akv))]
    w1 = [x[:, :BLOCK_W] for x in w12]
    w2 = [x[:, BLOCK_W:] for x in w12]
    q12 = [_dot(x, jnp.concatenate([y, u], axis=1)) for x, y, u in zip(a_rb, bd(w1), bd(w2))]
    q1 = [x + y[:, :BLOCK_W] for x, y in zip(rt_c, q12)]
    q2 = [_dot(x, y) + u[:, BLOCK_W:] for x, y, u in zip(a_rk, v_bd, q12)]
    bh_c, kh_c = _chunk_blocks(bh), _chunk_blocks(kh)
    p1 = [_dot_tn(x, y) * bd_mask_f for x, y in zip(w1, bh_c)]
    p2 = [_fold_diag_blocks(_dot_tn(jnp.concatenate([x, u], axis=0), jnp.concatenate([y, w], axis=0)) * bd_mask_f)
          for x, u, y, w in zip(w2, v_c, bh_c, kh_c)]

    s = [s_ref[:, b * BLOCK_W:(b + 1) * BLOCK_W] for b in range(N_BLOCKS)]
    y_rows = []
    for c in range(nc):
        y_blocks = []
        for b in range(N_BLOCKS):
            i = c * N_BLOCKS + b
            y_blocks.append(_dot_nt(q1[i], _blockdiag(s[b], bd_mask)) + q2[i])
            s[b] = s[b] * g_tot[c * n:c * n + 1, b * BLOCK_W:(b + 1) * BLOCK_W] + _dot(s[b], p1[i]) + p2[i]
        y_rows.append(jnp.concatenate(y_blocks, axis=1))
    for b in range(N_BLOCKS):
        s_ref[:, b * BLOCK_W:(b + 1) * BLOCK_W] = s[b]
    y = jnp.concatenate(y_rows, axis=0)

    mean = _dot_sel2(y, hsum) * (1.0 / HEAD_DIM)
    yc = y - mean
    var = _dot_sel2(yc * yc, hsum) * (1.0 / HEAD_DIM)
    yn = yc * lax.rsqrt(var + RWKV_GN_EPS) * lng_ref[...] + lnb_ref[...]
    bonus = _dot_sel2(r * k * rk_ref[...], hsum) * v
    y_ref[...] = (yn + bonus) * g


def _rwkv(z, p):
    b, t, _ = z.shape
    tt = SEQ_TILE
    vec = lambda a: a.reshape(1, -1)
    pad = lambda a, lo: jnp.zeros((RWKV_LORA, WIDTH), F32).at[lo:lo + a.shape[0]].set(a).astype(BF16)
    args = (z, vec(p["mu"]), vec(p["w0"]), pad(p["w2"], 0), vec(p["a0"]), pad(p["a2"], 32),
            pad(p["g2"], 64), vec(p["k_k"]), vec(p["k_a"]), vec(p["r_k"]), vec(p["ln_g"]), vec(p["ln_b"]))
    in_specs = [pl.BlockSpec((None, tt, RWKV_IN), lambda i, j: (i, j, 0))]
    in_specs += [_const_spec(a.shape) for a in args[1:]]
    return pl.pallas_call(
        _rwkv_kernel,
        grid=(b, t // tt),
        in_specs=in_specs,
        out_specs=pl.BlockSpec((None, tt, WIDTH), lambda i, j: (i, j, 0)),
        out_shape=jax.ShapeDtypeStruct((b, t, WIDTH), F32),
        scratch_shapes=[pltpu.VMEM((CARRY_ROWS, RWKV_IN), F32),
                        pltpu.VMEM((HEAD_DIM, WIDTH), F32)],
        compiler_params=_params("parallel", "arbitrary"),
        name="rwkv7",
    )(*args)


ATTN_BLOCK = 128
ATTN_UNITS = 8


def _attn_group(q_ref, k_ref, v_ref, og_ref, lg_ref, dil, first):
    t = q_ref.shape[0]
    nblk = t // dil // ATTN_BLOCK
    has_prev = nblk > 1
    qi, kj = _iota((ATTN_BLOCK, ATTN_BLOCK), 0), _iota((ATTN_BLOCK, ATTN_BLOCK), 1)
    cur_ok, prev_ok = kj <= qi, kj >= qi
    second = jnp.logical_not(first)

    def step(i, carry):
        rows, prevs, pmasks = [], [], []
        for x in range(ATTN_UNITS):
            u = i * ATTN_UNITS + x
            r, nb = u // nblk, u % nblk
            start = r + nb * (ATTN_BLOCK * dil)
            rows.append(pl.ds(start, ATTN_BLOCK, stride=dil))
            prevs.append(pl.ds(jnp.maximum(start - ATTN_BLOCK * dil, r), ATTN_BLOCK, stride=dil))
            pmasks.append(jnp.logical_and(prev_ok, nb > 0))
        q2 = [q_ref[rw, :] for rw in rows]
        kc = [k_ref[rw, :].astype(BF16) for rw in rows]
        vc = [v_ref[rw, :].astype(BF16) for rw in rows]
        if has_prev:
            kp = [k_ref[rw, :].astype(BF16) for rw in prevs]
            vc = [jnp.concatenate([v_ref[pw, :].astype(BF16), x], axis=0) for pw, x in zip(prevs, vc)]
        heads = []
        for sel in (first, second):
            qj = [jnp.where(sel, x, 0.0) for x in q2]
            s = [jnp.where(cur_ok, _dot_nt(x, y), NEG_INF) for x, y in zip(qj, kc)]
            if has_prev:
                sp = [jnp.where(pm, _dot_nt(x, y), NEG_INF) for pm, x, y in zip(pmasks, qj, kp)]
                s = [jnp.concatenate([x, y], axis=1) for x, y in zip(sp, s)]
            m = [jnp.max(x, axis=-1, keepdims=True) for x in s]
            e = [jnp.exp(x - y) for x, y in zip(s, m)]
            l = [jnp.sum(x, axis=-1, keepdims=True) for x in e]
            o = [_dot(x, y) / z for x, y, z in zip(e, vc, l)]
            lse = [jnp.broadcast_to(x + jnp.log(y), (ATTN_BLOCK, ATTN_OUT)) for x, y in zip(m, l)]
            heads.append((o, lse))
        for x in range(ATTN_UNITS):
            og_ref[rows[x], :] = jnp.where(first, heads[0][0][x], heads[1][0][x])
            lg_ref[rows[x], :] = jnp.where(first, heads[0][1][x], heads[1][1][x])
        return carry

    lax.fori_loop(0, t // ATTN_BLOCK // ATTN_UNITS, step, 0)


def _attn_kernel(q0, k0, v0, q1, k1, v1, q2, k2, v2, o_ref, og0, og1, og2, lg0, lg1, lg2):
    first = _iota((ATTN_BLOCK, ATTN_OUT), 1) < HEAD_DIM
    groups = ((q0, k0, v0, og0, lg0), (q1, k1, v1, og1, lg1), (q2, k2, v2, og2, lg2))
    for (win, dil), refs in zip(ATTN_GROUPS, groups):
        _attn_group(*refs, dil, first)
    l0, l1, l2 = lg0[...], lg1[...], lg2[...]
    mx = jnp.maximum(jnp.maximum(l0, l1), l2)
    w0, w1, w2 = jnp.exp(l0 - mx), jnp.exp(l1 - mx), jnp.exp(l2 - mx)
    o_ref[...] = (w0 * og0[...] + w1 * og1[...] + w2 * og2[...]) / (w0 + w1 + w2)


def _attention(zb):
    b, t, _ = zb.shape
    for win, dil in ATTN_GROUPS:
        assert win // dil == ATTN_BLOCK and t % (dil * ATTN_BLOCK) == 0 and t % (ATTN_BLOCK * ATTN_UNITS) == 0
    col = lambda blk: pl.BlockSpec((None, t, ATTN_OUT), lambda i, blk=blk: (i, 0, blk))
    ng = len(ATTN_GROUPS)
    in_specs = []
    for g in range(ng):
        in_specs += [col(g), col(ng + g), col(2 * ng + g)]
    return pl.pallas_call(
        _attn_kernel,
        grid=(b,),
        in_specs=in_specs,
        out_specs=pl.BlockSpec((None, t, ATTN_OUT), lambda i: (i, 0, 0)),
        out_shape=jax.ShapeDtypeStruct((b, t, ATTN_OUT), F32),
        scratch_shapes=[pltpu.VMEM((t, ATTN_OUT), F32)] * (2 * ng),
        compiler_params=_params("parallel"),
        name="dilated_attention",
    )(*([zb] * (3 * ng)))


def _mlstm_kernel(z_ref, cw_ref, cb_ref, ib_ref, fb_ref, y_ref, carry_ref, c_ref, m_ref):
    tt = z_ref.shape[0]

    @pl.when(pl.program_id(1) == 0)
    def _():
        carry_ref[...] = jnp.zeros_like(carry_ref)
        c_ref[...] = jnp.zeros_like(c_ref)
        m_ref[...] = jnp.zeros_like(m_ref)

    qk_in = z_ref[:, 0:2 * WIDTH]
    ext = jnp.concatenate([carry_ref[...], qk_in], axis=0)
    acc = cb_ref[...] + cw_ref[MLSTM_CONV - 1:MLSTM_CONV, :] * qk_in
    for j in range(1, MLSTM_CONV):
        acc = acc + cw_ref[MLSTM_CONV - 1 - j:MLSTM_CONV - j, :] * pltpu.roll(ext, j, axis=0)[CARRY_ROWS:]
    carry_ref[...] = qk_in[tt - CARRY_ROWS:, :]
    qk = _silu(acc)
    q, k = qk[:, :WIDTH], qk[:, WIDTH:] * (HEAD_DIM ** -0.5)
    v = z_ref[:, 2 * WIDTH:3 * WIDTH]
    og = _sigmoid(z_ref[:, 3 * WIDTH:4 * WIDTH])

    n = CHUNK
    nc = tt // n
    gi = z_ref[:, 4 * WIDTH:4 * WIDTH + 128] + ib_ref[...]
    lf = -_softplus(-(z_ref[:, 4 * WIDTH + 128:] + fb_ref[...]))
    bcs = _sel_dot(_chunk_cumsum_matrix(tt), lf)
    u = gi - bcs
    row_in_chunk = _iota((tt, 128), 0) & (n - 1)
    cmax = u
    for sh in (1, 2, 4, 8, 16, 32):
        cmax = jnp.where(row_in_chunk >= sh, jnp.maximum(cmax, pltpu.roll(cmax, sh, axis=0)), cmax)
    m_prev = m_ref[0:1, :]
    m_rows, scal_rows = [], []
    for c in range(nc):
        last = c * n + n - 1
        b_end = bcs[last:last + 1, :]
        m_new = b_end + jnp.maximum(m_prev, cmax[last:last + 1, :])
        m_rows.append(bcs[c * n:(c + 1) * n, :] + jnp.maximum(m_prev, cmax[c * n:(c + 1) * n, :]))
        scal_rows.append(jnp.concatenate([m_prev, b_end - m_new, jnp.exp(b_end + m_prev - m_new),
                                          jnp.zeros((CARRY_ROWS - 3, 128), F32)], axis=0))
        m_prev = m_new
    m_ref[0:1, :] = m_prev
    m_t = jnp.concatenate(m_rows, axis=0)

    expand = _onehot(_iota((128, WIDTH), 0) == jnp.right_shift(_iota((128, WIDTH), 1), 6))
    bt_x, mt_x, uc_x = _dot_sel(bcs, expand), _dot_sel(m_t, expand), _dot_sel(u, expand)
    scal_x = _dot_sel(jnp.concatenate(scal_rows, axis=0), expand)
    u_t = u.T

    bd_mask = _head_sum_matrix(BLOCK_W)
    bd_mask2 = jnp.concatenate([bd_mask, bd_mask], axis=1)
    bd_mask2_f = bd_mask2.astype(F32)
    ones = jnp.ones((n, BLOCK_W), BF16)
    ri, ci = _iota((n, BLOCK_W), 0), _iota((n, BLOCK_W), 1) & (n - 1)
    causal = ci <= ri
    q_c, k_c, v_c = _chunk_blocks(q), _chunk_blocks(k), _chunk_blocks(v)
    bt_c, mt_c, uc_c = _chunk_blocks(bt_x), _chunk_blocks(mt_x), _chunk_blocks(uc_x)
    scal = lambda c, b, row: scal_x[8 * c + row:8 * c + row + 1, b * BLOCK_W:(b + 1) * BLOCK_W]

    scores, upds = [], []
    for c in range(nc):
        for b in range(N_BLOCKS):
            i = c * N_BLOCKS + b
            u_row = jnp.concatenate([u_t[h:h + 1, c * n:(c + 1) * n]
                                     for h in range(b * BLOCK_HEADS, (b + 1) * BLOCK_HEADS)], axis=1)
            d_intra = jnp.where(causal, bt_c[i] + u_row, -jnp.inf)
            scores.append(_dot_nt(q_c[i], _blockdiag(k_c[i], bd_mask)) * jnp.exp(d_intra - mt_c[i]))
            w_k = jnp.exp(uc_c[i] + scal(c, b, 1))
            upds.append(_dot_tn(k_c[i] * w_k, jnp.concatenate([v_c[i].astype(BF16), ones], axis=1)) * bd_mask2_f)
    intra = [_dot(s, jnp.concatenate([_blockdiag(x, bd_mask), bd_mask], axis=1)) for s, x in zip(scores, v_c)]

    state = [c_ref[b] for b in range(N_BLOCKS)]
    y_rows = []
    for c in range(nc):
        y_blocks = []
        for b in range(N_BLOCKS):
            i = c * N_BLOCKS + b
            inter = _dot(q_c[i], state[b])
            w_inter = jnp.exp(bt_c[i] + scal(c, b, 0) - mt_c[i])
            num = w_inter * inter[:, :BLOCK_W] + intra[i][:, :BLOCK_W]
            den = w_inter * inter[:, BLOCK_W:] + intra[i][:, BLOCK_W:]
            y_blocks.append(num / jnp.maximum(jnp.abs(den), jnp.exp(-mt_c[i])))
            dec = scal(c, b, 2)
            state[b] = jnp.concatenate([dec, dec], axis=1) * state[b] + upds[i]
        y_rows.append(jnp.concatenate(y_blocks, axis=1))
    for b in range(N_BLOCKS):
        c_ref[b] = state[b]
    y_ref[...] = og * jnp.concatenate(y_rows, axis=0)


def _mlstm(z, p):
    b, t, _ = z.shape
    tt = SEQ_TILE
    lane_pad = lambda a: jnp.zeros((1, 128), F32).at[0, 0:N_HEADS].set(a)
    args = (z, p["conv_w"], p["conv_b"].reshape(1, -1), lane_pad(p["i_b"]), lane_pad(p["f_b"]))
    in_specs = [pl.BlockSpec((None, tt, MLSTM_IN_PAD), lambda i, j: (i, j, 0))]
    in_specs += [_const_spec(a.shape) for a in args[1:]]
    return pl.pallas_call(
        _mlstm_kernel,
        grid=(b, t // tt),
        in_specs=in_specs,
        out_specs=pl.BlockSpec((None, tt, WIDTH), lambda i, j: (i, j, 0)),
        out_shape=jax.ShapeDtypeStruct((b, t, WIDTH), F32),
        scratch_shapes=[pltpu.VMEM((CARRY_ROWS, 2 * WIDTH), F32),
                        pltpu.VMEM((N_BLOCKS, BLOCK_W, 2 * BLOCK_W), F32),
                        pltpu.VMEM((8, 128), F32)],
        compiler_params=_params("parallel", "arbitrary"),
        name="mlstm",
    )(*args)


def _hgrn_kernel(z_ref, lbl_ref, ng_ref, y_ref, s_ref, sel_ref, *, layer):
    tt = z_ref.shape[0]

    @pl.when(pl.program_id(1) == 0)
    def _():
        s_ref[...] = jnp.zeros_like(s_ref)

    logits = lbl_ref[...]
    pe = jnp.exp(logits - jnp.max(logits, axis=0, keepdims=True))
    pr = pe / jnp.sum(pe, axis=0, keepdims=True)
    lb = pr[0:1, :]
    for i in range(1, layer + 1):
        lb = lb + pr[i:i + 1, :]
    lb = lb - pr[0:1, :]

    q = _silu(z_ref[:, 0:WIDTH])
    f = lb + (1.0 - lb) * _sigmoid(z_ref[:, WIDTH:2 * WIDTH])
    k = 1.0 - f
    v = z_ref[:, 2 * WIDTH:3 * WIDTH]
    og = _sigmoid(z_ref[:, 3 * WIDTH:4 * WIDTH])
    gl = jnp.log(f)

    @pl.when(pl.program_id(1) == 0)
    def _():
        r, c = _iota((tt, tt), 0), _iota((tt, tt), 1)
        sel_ref[0] = _chunk_cumsum_matrix(tt)
        for lv in range(1, 7):
            sel_ref[lv] = _onehot(c == jnp.right_shift(r, lv) * (1 << lv) + (1 << (lv - 1)) - 1)

    bsum = _sel_dot(sel_ref[0], gl)
    b_end = jnp.concatenate([jnp.broadcast_to(bsum[c * CHUNK + CHUNK - 1:(c + 1) * CHUNK, :], (CHUNK, WIDTH))
                             for c in range(tt // CHUNK)], axis=0)
    q_in = q * jnp.exp(bsum)
    k_end = k * jnp.exp(b_end - bsum)
    g_tot = jnp.exp(b_end)

    n = CHUNK
    nc = tt // n
    hsum = _head_sum_matrix(WIDTH)
    bd_mask = _head_sum_matrix(BLOCK_W)
    bd_mask_f = bd_mask.astype(F32)
    ri, ci = _iota((n, BLOCK_W), 0), _iota((n, BLOCK_W), 1) & (n - 1)
    row_t = _iota((tt, 1), 0)
    bd = lambda xs: [_blockdiag(x, bd_mask) for x in xs]
    attn = [jnp.where(ri == ci, _dot_nt(x, y), 0.0) for x, y in zip(_chunk_blocks(q), bd(_chunk_blocks(k)))]
    bsum_b = bsum.astype(BF16)
    for lv in range(1, 7):
        size, half = 1 << lv, 1 << (lv - 1)
        b_mid = jnp.dot(sel_ref[lv], bsum_b, preferred_element_type=F32)
        right = (row_t & (size - 1)) >= half
        q_l = jnp.where(right, q * jnp.exp(bsum - b_mid), 0.0)
        k_l = jnp.where(right, 0.0, k * jnp.exp(b_mid - bsum))
        same = jnp.right_shift(ri, lv) == jnp.right_shift(ci, lv)
        attn = [a + jnp.where(same, _dot_nt(x, y), 0.0)
                for a, x, y in zip(attn, _chunk_blocks(q_l), bd(_chunk_blocks(k_l)))]
    v_c = _chunk_blocks(v)
    intra = [_dot(a, y) for a, y in zip(attn, bd(v_c))]
    upd = [_fold_diag_blocks(_dot_tn(x, y) * bd_mask_f) for x, y in zip(v_c, _chunk_blocks(k_end))]

    s = [s_ref[:, b * BLOCK_W:(b + 1) * BLOCK_W] for b in range(N_BLOCKS)]
    y_rows = []
    q_in_c = _chunk_blocks(q_in)
    for c in range(nc):
        y_blocks = []
        for b in range(N_BLOCKS):
            i = c * N_BLOCKS + b
            y_blocks.append(_dot_nt(q_in_c[i], _blockdiag(s[b], bd_mask)) + intra[i])
            s[b] = s[b] * g_tot[c * n:c * n + 1, b * BLOCK_W:(b + 1) * BLOCK_W] + upd[i]
        y_rows.append(jnp.concatenate(y_blocks, axis=1))
    for b in range(N_BLOCKS):
        s_ref[:, b * BLOCK_W:(b + 1) * BLOCK_W] = s[b]
    o = jnp.concatenate(y_rows, axis=0)
    ms = _dot_sel2(o * o, hsum) * (1.0 / HEAD_DIM)
    y_ref[...] = o * lax.rsqrt(ms + NORM_EPS) * ng_ref[...] * og


def _hgrn(z, lb_logits, norm_g, layer):
    b, t, _ = z.shape
    tt = SEQ_TILE
    return pl.pallas_call(
        functools.partial(_hgrn_kernel, layer=layer),
        grid=(b, t // tt),
        in_specs=[pl.BlockSpec((None, tt, HGRN_IN), lambda i, j: (i, j, 0)),
                  _const_spec(lb_logits.shape), _const_spec((1, WIDTH))],
        out_specs=pl.BlockSpec((None, tt, WIDTH), lambda i, j: (i, j, 0)),
        out_shape=jax.ShapeDtypeStruct((b, t, WIDTH), F32),
        scratch_shapes=[pltpu.VMEM((HEAD_DIM, WIDTH), F32), pltpu.VMEM((7, tt, tt), BF16)],
        compiler_params=_params("parallel", "arbitrary"),
        name="hgrn2",
    )(z, lb_logits, norm_g.reshape(1, -1))


def _merge_kernel(h_ref, g_ref, wg_ref, bg_ref, ya_ref, yb_ref, yc_ref, yd_ref,
                  pa_ref, pb_ref, pc_ref, pd_ref, wo_ref, o_ref):
    h = h_ref[...]
    d = h.shape[1]
    xb = _rms(h, g_ref[...]).astype(BF16)
    merged = None
    for i, (y_ref, p_ref) in enumerate(((ya_ref, pa_ref), (yb_ref, pb_ref), (yc_ref, pc_ref), (yd_ref, pd_ref))):
        gate = _sigmoid(jnp.dot(xb, wg_ref[:, i * d:(i + 1) * d], preferred_element_type=F32)
                        + bg_ref[:, i * d:(i + 1) * d])
        term = gate * jnp.dot(y_ref[...].astype(BF16), p_ref[...], preferred_element_type=F32)
        merged = term if merged is None else merged + term
    o_ref[...] = h + jnp.dot(merged.astype(BF16), wo_ref[...], preferred_element_type=F32)


def _merge(h, g, wg, bg, ys, ps, wo):
    n, d = h.shape
    tm = TOKEN_TILE
    row = lambda w: pl.BlockSpec((tm, w), lambda i: (i, 0))
    in_specs = [row(d), _const_spec((1, d)), _const_spec(wg.shape, True), _const_spec(bg.shape)]
    in_specs += [row(y.shape[1]) for y in ys]
    in_specs += [_const_spec(p.shape, True) for p in ps]
    in_specs += [_const_spec(wo.shape, True)]
    return pl.pallas_call(
        _merge_kernel,
        grid=(n // tm,),
        in_specs=in_specs,
        out_specs=row(d),
        out_shape=jax.ShapeDtypeStruct((n, d), F32),
        compiler_params=_params("parallel"),
        name="merge",
    )(h, g, wg, bg, *ys, *ps, wo)


def _ffn_kernel(h_ref, g_ref, wu_ref, cw_ref, cb_ref, wd_ref, gf_ref, o_ref, carry_ref, act_ref, *, final):
    tm = h_ref.shape[0]
    dff = wd_ref.shape[0]

    @pl.when(pl.program_id(1) == 0)
    def _():
        carry_ref[...] = jnp.zeros_like(carry_ref)

    h = h_ref[...]
    xb = _rms(h, g_ref[...]).astype(BF16)

    def up_cols(lo):
        return jnp.dot(xb, wu_ref[:, lo:lo + FF_TILE], preferred_element_type=F32)

    def conv_cols(up, lo):
        ext = jnp.concatenate([carry_ref[:, lo:lo + FF_TILE], up], axis=0)
        carry_ref[:, lo:lo + FF_TILE] = up[tm - CARRY_ROWS:, :]
        out = cb_ref[:, lo:lo + FF_TILE] + cw_ref[FFN_CONV - 1:FFN_CONV, lo:lo + FF_TILE] * up
        for j in range(1, FFN_CONV):
            out = out + (cw_ref[FFN_CONV - 1 - j:FFN_CONV - j, lo:lo + FF_TILE]
                         * pltpu.roll(ext, j, axis=0)[CARRY_ROWS:])
        return out

    nslices = dff // FF_TILE
    half = (nslices // 2) * FF_TILE
    nxt = (up_cols(dff), up_cols(0))
    for ci in range(nslices):
        lo = ci * FF_TILE
        up_g, up_u = nxt
        if ci + 1 < nslices:
            nxt = (up_cols(dff + lo + FF_TILE), up_cols(lo + FF_TILE))
        act_ref[:, lo:lo + FF_TILE] = (_silu(conv_cols(up_g, dff + lo)) * conv_cols(up_u, lo)).astype(BF16)
        if lo + FF_TILE == half:
            out = h + jnp.dot(act_ref[:, :half], wd_ref[:half, :], preferred_element_type=F32)
    out = out + jnp.dot(act_ref[:, half:], wd_ref[half:, :], preferred_element_type=F32)
    o_ref[...] = _rms(out, gf_ref[...]) if final else out


def _ffn(h, g, wu, cw, cb, wd, gf, final):
    b, t, d = h.shape
    tm = TOKEN_TILE
    blk = pl.BlockSpec((None, tm, d), lambda i, j: (i, j, 0))
    return pl.pallas_call(
        functools.partial(_ffn_kernel, final=final),
        grid=(b, t // tm),
        in_specs=[blk, _const_spec((1, d)), _const_spec(wu.shape, True), _const_spec(cw.shape),
                  _const_spec(cb.shape), _const_spec(wd.shape, True), _const_spec((1, d))],
        out_specs=blk,
        out_shape=jax.ShapeDtypeStruct((b, t, d), F32),
        scratch_shapes=[pltpu.VMEM((CARRY_ROWS, wu.shape[1]), F32), pltpu.VMEM((tm, wd.shape[0]), BF16)],
        compiler_params=_params("parallel", "arbitrary"),
        name="convffn",
    )(h, g, wu, cw, cb, wd, gf)


def _rope_tables(seq):
    half = HEAD_DIM // 2
    inv_freq = ROPE_THETA ** (-jnp.arange(half, dtype=F32) / half)
    ang = jnp.arange(seq).astype(F32)[:, None] * inv_freq[None, :]
    cos, sin = jnp.cos(ang), jnp.sin(ang)
    return jnp.concatenate([cos, cos, cos, cos], axis=1), jnp.concatenate([-sin, sin, -sin, sin], axis=1)


def _swap_head_halves(w):
    d, n = w.shape
    return w.reshape(d, n // HEAD_DIM, 2, HEAD_DIM // 2)[:, :, ::-1, :].reshape(d, n)


def _split_w_in(w):
    d = w.shape[0]
    o = 0
    wa = w[:, o:o + RWKV_IN]; o += RWKV_IN
    wb = w[:, o:o + 3 * ATTN_WIDTH]; o += 3 * ATTN_WIDTH
    wc = w[:, o:o + 4 * WIDTH + 2 * N_HEADS]; o += 4 * WIDTH + 2 * N_HEADS
    wd = w[:, o:o + HGRN_IN]; o += HGRN_IN
    wg = w[:, o:]
    wqk = wb[:, :2 * ATTN_WIDTH]
    gate_pad = jnp.zeros((d, 128 - N_HEADS), w.dtype)
    wc_pad = jnp.concatenate([wc[:, :4 * WIDTH], wc[:, 4 * WIDTH:4 * WIDTH + N_HEADS], gate_pad,
                              wc[:, 4 * WIDTH + N_HEADS:], gate_pad], axis=1)
    w_cat = jnp.concatenate([wa, wb, _swap_head_halves(wqk), wc_pad, wd], axis=1)
    return w_cat.astype(BF16), wg.astype(BF16)


def kernel(x, norm_mix_g, w_in, b_gate, rwkv_mu, rwkv_w0, rwkv_w2, rwkv_a0, rwkv_a2, rwkv_g2, rwkv_k_k, rwkv_k_a, rwkv_r_k, rwkv_ln_g, rwkv_ln_b, mlstm_conv_w, mlstm_conv_b, mlstm_i_b, mlstm_f_b, hgrn_lb_logits, hgrn_norm_g, p_rwkv, p_attn, p_mlstm, p_hgrn, w_out, norm_ffn_g, w_up, ffn_conv_w, ffn_conv_b, w_down, final_norm_g):
    bsz, seq, d = x.shape
    depth = w_in.shape[0]
    n = bsz * seq
    cos_t, sin_t = _rope_tables(seq)
    h = x.reshape(n, d)
    for l in range(depth):
        w_cat, wg = _split_w_in(w_in[l])
        za, zb, zc, zd = _inproj(h, norm_mix_g[l].reshape(1, d), w_cat, cos_t, sin_t, seq)
        rwkv_p = dict(mu=rwkv_mu[l], w0=rwkv_w0[l], w2=rwkv_w2[l], a0=rwkv_a0[l], a2=rwkv_a2[l],
                      g2=rwkv_g2[l], k_k=rwkv_k_k[l], k_a=rwkv_k_a[l], r_k=rwkv_r_k[l],
                      ln_g=rwkv_ln_g[l], ln_b=rwkv_ln_b[l])
        ya = _rwkv(za.reshape(bsz, seq, -1), rwkv_p)
        yb = _attention(zb.reshape(bsz, seq, -1))
        yc = _mlstm(zc.reshape(bsz, seq, -1),
                    dict(conv_w=mlstm_conv_w[l], conv_b=mlstm_conv_b[l], i_b=mlstm_i_b[l], f_b=mlstm_f_b[l]))
        yd = _hgrn(zd.reshape(bsz, seq, -1), hgrn_lb_logits, hgrn_norm_g[l], l)
        ys = [y.reshape(n, -1) for y in (ya, yb, yc, yd)]
        ps = [p[l].astype(BF16) for p in (p_rwkv, p_attn, p_mlstm, p_hgrn)]
        h = _merge(h, norm_mix_g[l].reshape(1, d), wg, b_gate[l].reshape(1, -1), ys, ps,
                   w_out[l].astype(BF16))
        h = _ffn(h.reshape(bsz, seq, d), norm_ffn_g[l].reshape(1, d), w_up[l].astype(BF16),
                 ffn_conv_w[l], ffn_conv_b[l].reshape(1, -1), w_down[l].astype(BF16),
                 final_norm_g.reshape(1, d), l == depth - 1).reshape(n, d)
    return h.reshape(bsz, seq, d)
```

```python
import functools

import jax
import jax.numpy as jnp
from jax import lax
from jax.experimental import pallas as pl
from jax.experimental.pallas import tpu as pltpu

F32 = jnp.float32
BF16 = jnp.bfloat16

HEAD_DIM = 64
N_HEADS = 4
WIDTH = N_HEADS * HEAD_DIM
CHUNK = 64
BLOCK_HEADS = 2
BLOCK_W = BLOCK_HEADS * HEAD_DIM
N_BLOCKS = N_HEADS // BLOCK_HEADS
RWKV_LORA = 128
RWKV_IN = 3 * WIDTH + RWKV_LORA
RWKV_GN_EPS = 64e-5
ATTN_GROUPS = ((128, 1), (512, 4), (2048, 16))
ATTN_HEADS = 6
ATTN_WIDTH = ATTN_HEADS * HEAD_DIM
ATTN_OUT = 2 * HEAD_DIM
ROPE_THETA = 10000.0
MLSTM_CONV = 4
MLSTM_IN_PAD = 4 * WIDTH + 128
GATE_LANE = 64
HGRN_IN = 4 * WIDTH
N_BRANCHES = 4
FFN_CONV = 3
NORM_EPS = 1e-6
NEG_INF = -1e30

TOKEN_TILE = 512
SEQ_TILE = 256
FF_TILE = 256
CARRY_ROWS = 8
VMEM_LIMIT = 56 * 1024 * 1024


def _dot(a, b):
    return jnp.dot(a.astype(BF16), b.astype(BF16), preferred_element_type=F32)


def _dot_nt(a, b):
    return lax.dot_general(a.astype(BF16), b.astype(BF16), (((1,), (1,)), ((), ())),
                           preferred_element_type=F32)


def _dot_tn(a, b):
    return lax.dot_general(a.astype(BF16), b.astype(BF16), (((0,), (0,)), ((), ())),
                           preferred_element_type=F32)


def _split3(x):
    hi = x.astype(BF16)
    r1 = x - hi.astype(F32)
    mid = r1.astype(BF16)
    lo = (r1 - mid.astype(F32)).astype(BF16)
    return hi, mid, lo


def _sel_dot(m01, x):
    hi, mid, lo = _split3(x)
    return (jnp.dot(m01, hi, preferred_element_type=F32)
            + jnp.dot(m01, mid, preferred_element_type=F32)
            + jnp.dot(m01, lo, preferred_element_type=F32))


def _dot_sel(x, m01):
    hi, mid, lo = _split3(x)
    return (jnp.dot(hi, m01, preferred_element_type=F32)
            + jnp.dot(mid, m01, preferred_element_type=F32)
            + jnp.dot(lo, m01, preferred_element_type=F32))


def _dot_sel2(x, m01):
    hi = x.astype(BF16)
    lo = (x - hi.astype(F32)).astype(BF16)
    return jnp.dot(hi, m01, preferred_element_type=F32) + jnp.dot(lo, m01, preferred_element_type=F32)


def _iota(shape, dim):
    return lax.broadcasted_iota(jnp.int32, shape, dim)


def _onehot(cond):
    return jnp.where(cond, 1.0, 0.0).astype(BF16)


def _chunk_cumsum_matrix(n):
    r, c = _iota((n, n), 0), _iota((n, n), 1)
    return _onehot((jnp.right_shift(r, 6) == jnp.right_shift(c, 6)) & (c <= r))


def _head_sum_matrix(n):
    r, c = _iota((n, n), 0), _iota((n, n), 1)
    return _onehot(jnp.right_shift(r, 6) == jnp.right_shift(c, 6))


def _shift_rows(x, carry, j):
    ext = jnp.concatenate([carry, x], axis=0)
    return pltpu.roll(ext, j, axis=0)[CARRY_ROWS:]


def _softplus(x):
    return jnp.maximum(x, 0.0) + jnp.log1p(jnp.exp(-jnp.abs(x)))


def _sigmoid(x):
    return jax.nn.sigmoid(x)


def _silu(x):
    return x * _sigmoid(x)


def _rms(x, g):
    return x * lax.rsqrt(jnp.mean(x * x, axis=-1, keepdims=True) + NORM_EPS) * g


def _const_spec(shape, single_buffer=False):
    nd = len(shape)
    if single_buffer:
        return pl.BlockSpec(shape, lambda *_: (0,) * nd, pipeline_mode=pl.Buffered(1))
    return pl.BlockSpec(shape, lambda *_: (0,) * nd)


def _layer_spec(a, layer, single_buffer=False):
    shape = (None,) + a.shape[1:]
    nd = a.ndim
    idx = lambda *_: (layer,) + (0,) * (nd - 1)
    if single_buffer:
        return pl.BlockSpec(shape, idx, pipeline_mode=pl.Buffered(1))
    return pl.BlockSpec(shape, idx)


def _rows3(a):
    return a.reshape(a.shape[0], 1, -1)


def _params(*sem):
    return pltpu.CompilerParams(dimension_semantics=sem, vmem_limit_bytes=VMEM_LIMIT)


def _inproj_kernel(x_ref, g_ref, w_ref, wgd_ref, cos_ref, sin_ref, za_ref, zb_ref, zc_ref, zd_ref):
    xb = _rms(x_ref[...], g_ref[...]).astype(BF16)

    def mm(lo, width):
        return jnp.dot(xb, w_ref[:, lo:lo + width], preferred_element_type=F32)

    za_ref[...] = mm(0, RWKV_IN)
    o = RWKV_IN
    qk = mm(o, 2 * ATTN_WIDTH)
    w = 2 * ATTN_WIDTH
    low = (_iota(qk.shape, 1) & (HEAD_DIM - 1)) < HEAD_DIM // 2
    rot = jnp.where(low, pltpu.roll(qk, w - HEAD_DIM // 2, axis=1), pltpu.roll(qk, HEAD_DIM // 2, axis=1))
    cos = jnp.concatenate([cos_ref[...]] * (w // 128), axis=1)
    sin = jnp.concatenate([sin_ref[...]] * (w // 128), axis=1)
    roped = qk * cos + rot * sin
    zb_ref[:, 0:ATTN_WIDTH] = roped[:, 0:ATTN_WIDTH] * (HEAD_DIM ** -0.5)
    zb_ref[:, ATTN_WIDTH:w] = roped[:, ATTN_WIDTH:]
    zb_ref[:, w:] = mm(o + w, ATTN_WIDTH)
    o += 3 * ATTN_WIDTH
    zc_ref[:, 0:4 * WIDTH] = mm(o, 4 * WIDTH)
    gd = jnp.dot(xb, wgd_ref[...], preferred_element_type=F32)
    zc_ref[:, 4 * WIDTH:] = gd[:, 0:128]
    zd_ref[...] = gd[:, 128:]


def _inproj(h, g, w_abc, w_gd, cos_t, sin_t, seq, layer):
    n, d = h.shape
    tm = TOKEN_TILE
    nt = seq // tm
    widths = (RWKV_IN, 3 * ATTN_WIDTH, MLSTM_IN_PAD, HGRN_IN)
    row = lambda w: pl.BlockSpec((tm, w), lambda i: (i, 0))
    return pl.pallas_call(
        _inproj_kernel,
        grid=(n // tm,),
        in_specs=[row(d), _layer_spec(g, layer), _layer_spec(w_abc, layer, True), _layer_spec(w_gd, layer, True),
                  pl.BlockSpec((tm, 128), lambda i: (i % nt, 0)),
                  pl.BlockSpec((tm, 128), lambda i: (i % nt, 0))],
        out_specs=[row(w) for w in widths],
        out_shape=[jax.ShapeDtypeStruct((n, w), F32) for w in widths],
        compiler_params=_params("parallel"),
        name="inproj",
    )(h, g, w_abc, w_gd, cos_t, sin_t)


def _blockdiag(x, bd_mask):
    xb = x.astype(BF16)
    return jnp.concatenate([xb] * BLOCK_HEADS, axis=0) * bd_mask


def _chunk_blocks(x):
    return [x[c * CHUNK:(c + 1) * CHUNK, b * BLOCK_W:(b + 1) * BLOCK_W]
            for c in range(x.shape[0] // CHUNK) for b in range(N_BLOCKS)]


def _fold_diag_blocks(full):
    out = full[0:HEAD_DIM]
    for h in range(1, BLOCK_HEADS):
        out = out + full[h * HEAD_DIM:(h + 1) * HEAD_DIM]
    return out


def _tri_inverse(a, bd_mask, same16, same32, eye):
    bd = lambda m: [_blockdiag(x, bd_mask) for x in m]
    mm = lambda xs, ys: [_dot(x, y) for x, y in zip(xs, ys)]
    n1 = [jnp.where(same16, x, 0.0) for x in a]
    n2 = mm(n1, bd(n1))
    n4 = mm(n2, bd(n2))
    n8 = mm(n4, bd(n4))
    t = [eye + x for x in n1]
    for p in (n2, n4, n8):
        t = [x + y for x, y in zip(t, mm(t, bd(p)))]
    for blk in (jnp.logical_and(same32, jnp.logical_not(same16)), jnp.logical_not(same32)):
        ab = bd([jnp.where(blk, x, 0.0) for x in a])
        t = [x + y for x, y in zip(t, mm(mm(t, ab), bd(t)))]
    return t


def _rwkv_kernel(z_ref, mu_ref, w0_ref, w2_ref, a0_ref, a2_ref, g2_ref, kk_ref, ka_ref, rk_ref,
                 lng_ref, lnb_ref, y_ref, carry_ref, s_ref):
    tt = z_ref.shape[0]

    @pl.when(pl.program_id(1) == 0)
    def _():
        carry_ref[...] = jnp.zeros_like(carry_ref)
        s_ref[...] = jnp.zeros_like(s_ref)

    z = z_ref[...]
    zprev = _shift_rows(z, carry_ref[...], 1)
    carry_ref[...] = z[tt - CARRY_ROWS:, :]
    z = z + mu_ref[...] * (zprev - z)
    r, k, v, lora = z[:, 0:WIDTH], z[:, WIDTH:2 * WIDTH], z[:, 2 * WIDTH:3 * WIDTH], z[:, 3 * WIDTH:]

    def lora_rows(w_ref, lo):
        w = w_ref[...].astype(BF16)
        parts = [jnp.zeros((lo, WIDTH), BF16)] if lo else []
        parts.append(w)
        if lo + w.shape[0] < RWKV_LORA:
            parts.append(jnp.zeros((RWKV_LORA - lo - w.shape[0], WIDTH), BF16))
        return jnp.concatenate(parts, axis=0)

    w2p = lora_rows(w2_ref, 0)
    a2p = lora_rows(a2_ref, w2_ref.shape[0])
    g2p = lora_rows(g2_ref, w2_ref.shape[0] + a2_ref.shape[0])
    logw = -_softplus(-(w0_ref[...] + _dot(jnp.tanh(lora), w2p))) - 0.5
    ld = -jnp.exp(logw)
    a = _sigmoid(a0_ref[...] + _dot(lora, a2p))
    g = _dot(_sigmoid(lora), g2p)

    hsum = _head_sum_matrix(WIDTH)
    kk = k * kk_ref[...]
    kk = kk * lax.rsqrt(_dot_sel2(kk * kk, hsum) + 1e-12)
    k = k * (1.0 + (a - 1.0) * ka_ref[...])
    sa, sb = -kk, kk * a

    cs = _sel_dot(_chunk_cumsum_matrix(tt), ld)
    tot = jnp.concatenate([jnp.broadcast_to(cs[c * CHUNK + CHUNK - 1:(c + 1) * CHUNK, :], (CHUNK, WIDTH))
                           for c in range(tt // CHUNK)], axis=0)
    g_inc, g_exc, g_inv, g_end = jnp.exp(cs), jnp.exp(cs - ld), jnp.exp(-cs), jnp.exp(tot - cs)
    g_tot = jnp.exp(tot)
    rt, at = r * g_inc, sa * g_exc
    bt, kt = sb * g_inv, k * g_inv
    bh, kh = sb * g_end, k * g_end

    n = CHUNK
    nc = tt // n
    bd_mask = _head_sum_matrix(BLOCK_W)
    bd_mask_f = bd_mask.astype(F32)
    ri, ci = _iota((n, BLOCK_W), 0), _iota((n, BLOCK_W), 1) & (n - 1)
    strict, incl, eye = ci < ri, ci <= ri, jnp.where(ci == ri, 1.0, 0.0)
    same16 = jnp.right_shift(ri, 4) == jnp.right_shift(ci, 4)
    same32 = jnp.right_shift(ri, 5) == jnp.right_shift(ci, 5)
    bd = lambda xs: [_blockdiag(x, bd_mask) for x in xs]
    at_c, rt_c, v_c = _chunk_blocks(at), _chunk_blocks(rt), _chunk_blocks(v)

    lhs = [jnp.concatenate([x, y], axis=0) for x, y in zip(at_c, rt_c)]
    gb = [_dot_nt(x, y) for x, y in zip(lhs, bd(_chunk_blocks(bt)))]
    gk = [_dot_nt(x, y) for x, y in zip(lhs, bd(_chunk_blocks(kt)))]
    a_ab = [jnp.where(strict, x[:n], 0.0) for x in gb]
    a_rb = [jnp.where(incl, x[n:], 0.0) for x in gb]
    a_ak = [jnp.where(strict, x[:n], 0.0) for x in gk]
    a_rk = [jnp.where(incl, x[n:], 0.0) for x in gk]
    tinv = _tri_inverse(a_ab, bd_mask, same16, same32, eye)
    v_bd = bd(v_c)
    akv = [_dot(x, y) for x, y in zip(a_ak, v_bd)]
    w12 = [_dot(t, jnp.concatenate([x, y], axis=1)) for t, x, y in zip(tinv, bd(at_c), bd(akv))]
    w1 = [x[:, :BLOCK_W] for x in w12]
    w2 = [x[:, BLOCK_W:] for x in w12]
    q12 = [_dot(x, jnp.concatenate([y, u], axis=1)) for x, y, u in zip(a_rb, bd(w1), bd(w2))]
    q1 = [x + y[:, :BLOCK_W] for x, y in zip(rt_c, q12)]
    q2 = [_dot(x, y) + u[:, BLOCK_W:] for x, y, u in zip(a_rk, v_bd, q12)]
    bh_c, kh_c = _chunk_blocks(bh), _chunk_blocks(kh)
    p1 = [_dot_tn(x, y) * bd_mask_f for x, y in zip(w1, bh_c)]
    p2 = [_fold_diag_blocks(_dot_tn(jnp.concatenate([x, u], axis=0), jnp.concatenate([y, w], axis=0)) * bd_mask_f)
          for x, u, y, w in zip(w2, v_c, bh_c, kh_c)]

    s = [s_ref[:, b * BLOCK_W:(b + 1) * BLOCK_W] for b in range(N_BLOCKS)]
    y_rows = []
    for c in range(nc):
        y_blocks = []
        for b in range(N_BLOCKS):
            i = c * N_BLOCKS + b
            y_blocks.append(_dot_nt(q1[i], _blockdiag(s[b], bd_mask)) + q2[i])
            s[b] = s[b] * g_tot[c * n:c * n + 1, b * BLOCK_W:(b + 1) * BLOCK_W] + _dot(s[b], p1[i]) + p2[i]
        y_rows.append(jnp.concatenate(y_blocks, axis=1))
    for b in range(N_BLOCKS):
        s_ref[:, b * BLOCK_W:(b + 1) * BLOCK_W] = s[b]
    y = jnp.concatenate(y_rows, axis=0)

    mean = _dot_sel2(y, hsum) * (1.0 / HEAD_DIM)
    yc = y - mean
    var = _dot_sel2(yc * yc, hsum) * (1.0 / HEAD_DIM)
    yn = yc * lax.rsqrt(var + RWKV_GN_EPS) * lng_ref[...] + lnb_ref[...]
    bonus = _dot_sel2(r * k * rk_ref[...], hsum) * v
    y_ref[...] = (yn + bonus) * g


def _rwkv(z, p, layer):
    b, t, _ = z.shape
    tt = SEQ_TILE
    args = (z, p["mu"], p["w0"], p["w2"], p["a0"], p["a2"], p["g2"], p["k_k"], p["k_a"], p["r_k"],
            p["ln_g"], p["ln_b"])
    in_specs = [pl.BlockSpec((None, tt, RWKV_IN), lambda i, j: (i, j, 0))]
    in_specs += [_layer_spec(a, layer) for a in args[1:]]
    return pl.pallas_call(
        _rwkv_kernel,
        grid=(b, t // tt),
        in_specs=in_specs,
        out_specs=pl.BlockSpec((None, tt, WIDTH), lambda i, j: (i, j, 0)),
        out_shape=jax.ShapeDtypeStruct((b, t, WIDTH), F32),
        scratch_shapes=[pltpu.VMEM((CARRY_ROWS, RWKV_IN), F32),
                        pltpu.VMEM((HEAD_DIM, WIDTH), F32)],
        compiler_params=_params("parallel", "arbitrary"),
        name="rwkv7",
    )(*args)


ATTN_BLOCK = 128
ATTN_UNITS = 8


def _attn_group(q_ref, k_ref, v_ref, og_ref, lg_ref, dil, first):
    t = q_ref.shape[0]
    nblk = t // dil // ATTN_BLOCK
    has_prev = nblk > 1
    qi, kj = _iota((ATTN_BLOCK, ATTN_BLOCK), 0), _iota((ATTN_BLOCK, ATTN_BLOCK), 1)
    cur_ok, prev_ok = kj <= qi, kj >= qi
    second = jnp.logical_not(first)

    def step(i, carry):
        rows, prevs, pmasks = [], [], []
        for x in range(ATTN_UNITS):
            u = i * ATTN_UNITS + x
            r, nb = u // nblk, u % nblk
            start = r + nb * (ATTN_BLOCK * dil)
            rows.append(pl.ds(start, ATTN_BLOCK, stride=dil))
            prevs.append(pl.ds(jnp.maximum(start - ATTN_BLOCK * dil, r), ATTN_BLOCK, stride=dil))
            pmasks.append(jnp.logical_and(prev_ok, nb > 0))
        q2 = [q_ref[rw, :] for rw in rows]
        kc = [k_ref[rw, :].astype(BF16) for rw in rows]
        vc = [v_ref[rw, :].astype(BF16) for rw in rows]
        if has_prev:
            kp = [k_ref[rw, :].astype(BF16) for rw in prevs]
            vc = [jnp.concatenate([v_ref[pw, :].astype(BF16), x], axis=0) for pw, x in zip(prevs, vc)]
        heads = []
        for sel in (first, second):
            qj = [jnp.where(sel, x, 0.0) for x in q2]
            s = [jnp.where(cur_ok, _dot_nt(x, y), NEG_INF) for x, y in zip(qj, kc)]
            if has_prev:
                sp = [jnp.where(pm, _dot_nt(x, y), NEG_INF) for pm, x, y in zip(pmasks, qj, kp)]
                s = [jnp.concatenate([x, y], axis=1) for x, y in zip(sp, s)]
            m = [jnp.max(x, axis=-1, keepdims=True) for x in s]
            e = [jnp.exp(x - y) for x, y in zip(s, m)]
            l = [jnp.sum(x, axis=-1, keepdims=True) for x in e]
            o = [_dot(x, y) / z for x, y, z in zip(e, vc, l)]
            lse = [jnp.broadcast_to(x + jnp.log(y), (ATTN_BLOCK, ATTN_OUT)) for x, y in zip(m, l)]
            heads.append((o, lse))
        for x in range(ATTN_UNITS):
            og_ref[rows[x], :] = jnp.where(first, heads[0][0][x], heads[1][0][x])
            lg_ref[rows[x], :] = jnp.where(first, heads[0][1][x], heads[1][1][x])
        return carry

    lax.fori_loop(0, t // ATTN_BLOCK // ATTN_UNITS, step, 0)


def _attn_kernel(q0, k0, v0, q1, k1, v1, q2, k2, v2, o_ref, og0, og1, og2, lg0, lg1, lg2):
    first = _iota((ATTN_BLOCK, ATTN_OUT), 1) < HEAD_DIM
    groups = ((q0, k0, v0, og0, lg0), (q1, k1, v1, og1, lg1), (q2, k2, v2, og2, lg2))
    for (win, dil), refs in zip(ATTN_GROUPS, groups):
        _attn_group(*refs, dil, first)
    l0, l1, l2 = lg0[...], lg1[...], lg2[...]
    mx = jnp.maximum(jnp.maximum(l0, l1), l2)
    w0, w1, w2 = jnp.exp(l0 - mx), jnp.exp(l1 - mx), jnp.exp(l2 - mx)
    o_ref[...] = (w0 * og0[...] + w1 * og1[...] + w2 * og2[...]) / (w0 + w1 + w2)


def _attention(zb):
    b, t, _ = zb.shape
    for win, dil in ATTN_GROUPS:
        assert win // dil == ATTN_BLOCK and t % (dil * ATTN_BLOCK) == 0 and t % (ATTN_BLOCK * ATTN_UNITS) == 0
    col = lambda blk: pl.BlockSpec((None, t, ATTN_OUT), lambda i, blk=blk: (i, 0, blk))
    ng = len(ATTN_GROUPS)
    in_specs = []
    for g in range(ng):
        in_specs += [col(g), col(ng + g), col(2 * ng + g)]
    return pl.pallas_call(
        _attn_kernel,
        grid=(b,),
        in_specs=in_specs,
        out_specs=pl.BlockSpec((None, t, ATTN_OUT), lambda i: (i, 0, 0)),
        out_shape=jax.ShapeDtypeStruct((b, t, ATTN_OUT), F32),
        scratch_shapes=[pltpu.VMEM((t, ATTN_OUT), F32)] * (2 * ng),
        compiler_params=_params("parallel"),
        name="dilated_attention",
    )(*([zb] * (3 * ng)))


def _mlstm_kernel(z_ref, cw_ref, cb_ref, ib_ref, fb_ref, y_ref, carry_ref, c_ref, m_ref):
    tt = z_ref.shape[0]

    @pl.when(pl.program_id(1) == 0)
    def _():
        carry_ref[...] = jnp.zeros_like(carry_ref)
        c_ref[...] = jnp.zeros_like(c_ref)
        m_ref[...] = jnp.zeros_like(m_ref)

    qk_in = z_ref[:, 0:2 * WIDTH]
    ext = jnp.concatenate([carry_ref[...], qk_in], axis=0)
    acc = cb_ref[...] + cw_ref[MLSTM_CONV - 1:MLSTM_CONV, :] * qk_in
    for j in range(1, MLSTM_CONV):
        acc = acc + cw_ref[MLSTM_CONV - 1 - j:MLSTM_CONV - j, :] * pltpu.roll(ext, j, axis=0)[CARRY_ROWS:]
    carry_ref[...] = qk_in[tt - CARRY_ROWS:, :]
    qk = _silu(acc)
    q, k = qk[:, :WIDTH], qk[:, WIDTH:] * (HEAD_DIM ** -0.5)
    v = z_ref[:, 2 * WIDTH:3 * WIDTH]
    og = _sigmoid(z_ref[:, 3 * WIDTH:4 * WIDTH])

    n = CHUNK
    nc = tt // n
    gates = z_ref[:, 4 * WIDTH:]
    gi = gates + ib_ref[...]
    lf = -_softplus(-(pltpu.roll(gates, 128 - GATE_LANE, axis=1) + fb_ref[...]))
    bcs = _sel_dot(_chunk_cumsum_matrix(tt), lf)
    u = gi - bcs
    row_in_chunk = _iota((tt, 128), 0) & (n - 1)
    cmax = u
    for sh in (1, 2, 4, 8, 16, 32):
        cmax = jnp.where(row_in_chunk >= sh, jnp.maximum(cmax, pltpu.roll(cmax, sh, axis=0)), cmax)
    m_prev = m_ref[0:1, :]
    m_rows, scal_rows = [], []
    for c in range(nc):
        last = c * n + n - 1
        b_end = bcs[last:last + 1, :]
        m_new = b_end + jnp.maximum(m_prev, cmax[last:last + 1, :])
        m_rows.append(bcs[c * n:(c + 1) * n, :] + jnp.maximum(m_prev, cmax[c * n:(c + 1) * n, :]))
        scal_rows.append(jnp.concatenate([m_prev, b_end - m_new, jnp.exp(b_end + m_prev - m_new),
                                          jnp.zeros((CARRY_ROWS - 3, 128), F32)], axis=0))
        m_prev = m_new
    m_ref[0:1, :] = m_prev
    m_t = jnp.concatenate(m_rows, axis=0)

    expand = _onehot(_iota((128, WIDTH), 0) == jnp.right_shift(_iota((128, WIDTH), 1), 6))
    bt_x, mt_x, uc_x = _dot_sel(bcs, expand), _dot_sel(m_t, expand), _dot_sel(u, expand)
    scal_x = _dot_sel(jnp.concatenate(scal_rows, axis=0), expand)
    u_t = u.T

    bd_mask = _head_sum_matrix(BLOCK_W)
    bd_mask2 = jnp.concatenate([bd_mask, bd_mask], axis=1)
    bd_mask2_f = bd_mask2.astype(F32)
    ones = jnp.ones((n, BLOCK_W), BF16)
    ri, ci = _iota((n, BLOCK_W), 0), _iota((n, BLOCK_W), 1) & (n - 1)
    causal = ci <= ri
    q_c, k_c, v_c = _chunk_blocks(q), _chunk_blocks(k), _chunk_blocks(v)
    bt_c, mt_c, uc_c = _chunk_blocks(bt_x), _chunk_blocks(mt_x), _chunk_blocks(uc_x)
    scal = lambda c, b, row: scal_x[8 * c + row:8 * c + row + 1, b * BLOCK_W:(b + 1) * BLOCK_W]

    scores, upds = [], []
    for c in range(nc):
        for b in range(N_BLOCKS):
            i = c * N_BLOCKS + b
            u_row = jnp.concatenate([u_t[h:h + 1, c * n:(c + 1) * n]
                                     for h in range(b * BLOCK_HEADS, (b + 1) * BLOCK_HEADS)], axis=1)
            d_intra = jnp.where(causal, bt_c[i] + u_row, -jnp.inf)
            scores.append(_dot_nt(q_c[i], _blockdiag(k_c[i], bd_mask)) * jnp.exp(d_intra - mt_c[i]))
            w_k = jnp.exp(uc_c[i] + scal(c, b, 1))
            upds.append(_dot_tn(k_c[i] * w_k, jnp.concatenate([v_c[i].astype(BF16), ones], axis=1)) * bd_mask2_f)
    intra = [_dot(s, jnp.concatenate([_blockdiag(x, bd_mask), bd_mask], axis=1)) for s, x in zip(scores, v_c)]

    state = [c_ref[b] for b in range(N_BLOCKS)]
    y_rows = []
    for c in range(nc):
        y_blocks = []
        for b in range(N_BLOCKS):
            i = c * N_BLOCKS + b
            inter = _dot(q_c[i], state[b])
            w_inter = jnp.exp(bt_c[i] + scal(c, b, 0) - mt_c[i])
            num = w_inter * inter[:, :BLOCK_W] + intra[i][:, :BLOCK_W]
            den = w_inter * inter[:, BLOCK_W:] + intra[i][:, BLOCK_W:]
            y_blocks.append(num / jnp.maximum(jnp.abs(den), jnp.exp(-mt_c[i])))
            dec = scal(c, b, 2)
            state[b] = jnp.concatenate([dec, dec], axis=1) * state[b] + upds[i]
        y_rows.append(jnp.concatenate(y_blocks, axis=1))
    for b in range(N_BLOCKS):
        c_ref[b] = state[b]
    y_ref[...] = og * jnp.concatenate(y_rows, axis=0)


def _mlstm(z, p, layer):
    b, t, _ = z.shape
    tt = SEQ_TILE
    args = (z, p["conv_w"], p["conv_b"], p["i_b"], p["f_b"])
    in_specs = [pl.BlockSpec((None, tt, MLSTM_IN_PAD), lambda i, j: (i, j, 0))]
    in_specs += [_layer_spec(a, layer) for a in args[1:]]
    return pl.pallas_call(
        _mlstm_kernel,
        grid=(b, t // tt),
        in_specs=in_specs,
        out_specs=pl.BlockSpec((None, tt, WIDTH), lambda i, j: (i, j, 0)),
        out_shape=jax.ShapeDtypeStruct((b, t, WIDTH), F32),
        scratch_shapes=[pltpu.VMEM((CARRY_ROWS, 2 * WIDTH), F32),
                        pltpu.VMEM((N_BLOCKS, BLOCK_W, 2 * BLOCK_W), F32),
                        pltpu.VMEM((8, 128), F32)],
        compiler_params=_params("parallel", "arbitrary"),
        name="mlstm",
    )(*args)


def _hgrn_kernel(z_ref, lbl_ref, ng_ref, y_ref, s_ref, sel_ref, *, layer):
    tt = z_ref.shape[0]

    @pl.when(pl.program_id(1) == 0)
    def _():
        s_ref[...] = jnp.zeros_like(s_ref)

    logits = lbl_ref[...]
    pe = jnp.exp(logits - jnp.max(logits, axis=0, keepdims=True))
    pr = pe / jnp.sum(pe, axis=0, keepdims=True)
    lb = pr[0:1, :]
    for i in range(1, layer + 1):
        lb = lb + pr[i:i + 1, :]
    lb = lb - pr[0:1, :]

    q = _silu(z_ref[:, 0:WIDTH])
    f = lb + (1.0 - lb) * _sigmoid(z_ref[:, WIDTH:2 * WIDTH])
    k = 1.0 - f
    v = z_ref[:, 2 * WIDTH:3 * WIDTH]
    og = _sigmoid(z_ref[:, 3 * WIDTH:4 * WIDTH])
    gl = jnp.log(f)

    @pl.when(pl.program_id(1) == 0)
    def _():
        r, c = _iota((tt, tt), 0), _iota((tt, tt), 1)
        sel_ref[0] = _chunk_cumsum_matrix(tt)
        for lv in range(1, 7):
            sel_ref[lv] = _onehot(c == jnp.right_shift(r, lv) * (1 << lv) + (1 << (lv - 1)) - 1)

    bsum = _sel_dot(sel_ref[0], gl)
    b_end = jnp.concatenate([jnp.broadcast_to(bsum[c * CHUNK + CHUNK - 1:(c + 1) * CHUNK, :], (CHUNK, WIDTH))
                             for c in range(tt // CHUNK)], axis=0)
    q_in = q * jnp.exp(bsum)
    k_end = k * jnp.exp(b_end - bsum)
    g_tot = jnp.exp(b_end)

    n = CHUNK
    nc = tt // n
    hsum = _head_sum_matrix(WIDTH)
    bd_mask = _head_sum_matrix(BLOCK_W)
    bd_mask_f = bd_mask.astype(F32)
    ri, ci = _iota((n, BLOCK_W), 0), _iota((n, BLOCK_W), 1) & (n - 1)
    row_t = _iota((tt, 1), 0)
    bd = lambda xs: [_blockdiag(x, bd_mask) for x in xs]
    attn = [jnp.where(ri == ci, _dot_nt(x, y), 0.0) for x, y in zip(_chunk_blocks(q), bd(_chunk_blocks(k)))]
    bsum_b = bsum.astype(BF16)
    for lv in range(1, 7):
        size, half = 1 << lv, 1 << (lv - 1)
        b_mid = jnp.dot(sel_ref[lv], bsum_b, preferred_element_type=F32)
        right = (row_t & (size - 1)) >= half
        q_l = jnp.where(right, q * jnp.exp(bsum - b_mid), 0.0)
        k_l = jnp.where(right, 0.0, k * jnp.exp(b_mid - bsum))
        same = jnp.right_shift(ri, lv) == jnp.right_shift(ci, lv)
        attn = [a + jnp.where(same, _dot_nt(x, y), 0.0)
                for a, x, y in zip(attn, _chunk_blocks(q_l), bd(_chunk_blocks(k_l)))]
    v_c = _chunk_blocks(v)
    intra = [_dot(a, y) for a, y in zip(attn, bd(v_c))]
    upd = [_fold_diag_blocks(_dot_tn(x, y) * bd_mask_f) for x, y in zip(v_c, _chunk_blocks(k_end))]

    s = [s_ref[:, b * BLOCK_W:(b + 1) * BLOCK_W] for b in range(N_BLOCKS)]
    y_rows = []
    q_in_c = _chunk_blocks(q_in)
    for c in range(nc):
        y_blocks = []
        for b in range(N_BLOCKS):
            i = c * N_BLOCKS + b
            y_blocks.append(_dot_nt(q_in_c[i], _blockdiag(s[b], bd_mask)) + intra[i])
            s[b] = s[b] * g_tot[c * n:c * n + 1, b * BLOCK_W:(b + 1) * BLOCK_W] + upd[i]
        y_rows.append(jnp.concatenate(y_blocks, axis=1))
    for b in range(N_BLOCKS):
        s_ref[:, b * BLOCK_W:(b + 1) * BLOCK_W] = s[b]
    o = jnp.concatenate(y_rows, axis=0)
    ms = _dot_sel2(o * o, hsum) * (1.0 / HEAD_DIM)
    y_ref[...] = o * lax.rsqrt(ms + NORM_EPS) * ng_ref[...] * og


def _hgrn(z, lb_logits, norm_g, layer):
    b, t, _ = z.shape
    tt = SEQ_TILE
    return pl.pallas_call(
        functools.partial(_hgrn_kernel, layer=layer),
        grid=(b, t // tt),
        in_specs=[pl.BlockSpec((None, tt, HGRN_IN), lambda i, j: (i, j, 0)),
                  _const_spec(lb_logits.shape), _layer_spec(norm_g, layer)],
        out_specs=pl.BlockSpec((None, tt, WIDTH), lambda i, j: (i, j, 0)),
        out_shape=jax.ShapeDtypeStruct((b, t, WIDTH), F32),
        scratch_shapes=[pltpu.VMEM((HEAD_DIM, WIDTH), F32), pltpu.VMEM((7, tt, tt), BF16)],
        compiler_params=_params("parallel", "arbitrary"),
        name="hgrn2",
    )(z, lb_logits, norm_g)


def _merge_kernel(h_ref, g_ref, wg_ref, bg_ref, ya_ref, yb_ref, yc_ref, yd_ref,
                  pa_ref, pb_ref, pc_ref, pd_ref, wo_ref, o_ref):
    h = h_ref[...]
    d = h.shape[1]
    xb = _rms(h, g_ref[...]).astype(BF16)
    merged = None
    for i, (y_ref, p_ref) in enumerate(((ya_ref, pa_ref), (yb_ref, pb_ref), (yc_ref, pc_ref), (yd_ref, pd_ref))):
        gate = _sigmoid(jnp.dot(xb, wg_ref[:, i * d:(i + 1) * d], preferred_element_type=F32)
                        + bg_ref[:, i * d:(i + 1) * d])
        term = gate * jnp.dot(y_ref[...].astype(BF16), p_ref[...], preferred_element_type=F32)
        merged = term if merged is None else merged + term
    o_ref[...] = h + jnp.dot(merged.astype(BF16), wo_ref[...], preferred_element_type=F32)


def _merge(h, g, wg, bg, ys, ps, wo, layer):
    n, d = h.shape
    tm = TOKEN_TILE
    row = lambda w: pl.BlockSpec((tm, w), lambda i: (i, 0))
    in_specs = [row(d), _layer_spec(g, layer), _layer_spec(wg, layer, True), _layer_spec(bg, layer)]
    in_specs += [row(y.shape[1]) for y in ys]
    in_specs += [_layer_spec(p, layer, True) for p in ps]
    in_specs += [_layer_spec(wo, layer, True)]
    return pl.pallas_call(
        _merge_kernel,
        grid=(n // tm,),
        in_specs=in_specs,
        out_specs=row(d),
        out_shape=jax.ShapeDtypeStruct((n, d), F32),
        compiler_params=_params("parallel"),
        name="merge",
    )(h, g, wg, bg, *ys, *ps, wo)


def _ffn_kernel(h_ref, g_ref, wu_ref, cw_ref, cb_ref, wd_ref, gf_ref, o_ref, carry_ref, act_ref, *, final):
    tm = h_ref.shape[0]
    dff = wd_ref.shape[0]

    @pl.when(pl.program_id(1) == 0)
    def _():
        carry_ref[...] = jnp.zeros_like(carry_ref)

    h = h_ref[...]
    xb = _rms(h, g_ref[...]).astype(BF16)

    def up_cols(lo):
        return jnp.dot(xb, wu_ref[:, lo:lo + FF_TILE], preferred_element_type=F32)

    def conv_cols(up, lo):
        ext = jnp.concatenate([carry_ref[:, lo:lo + FF_TILE], up], axis=0)
        carry_ref[:, lo:lo + FF_TILE] = up[tm - CARRY_ROWS:, :]
        out = cb_ref[:, lo:lo + FF_TILE] + cw_ref[FFN_CONV - 1:FFN_CONV, lo:lo + FF_TILE] * up
        for j in range(1, FFN_CONV):
            out = out + (cw_ref[FFN_CONV - 1 - j:FFN_CONV - j, lo:lo + FF_TILE]
                         * pltpu.roll(ext, j, axis=0)[CARRY_ROWS:])
        return out

    nslices = dff // FF_TILE
    half = (nslices // 2) * FF_TILE
    nxt = (up_cols(dff), up_cols(0))
    for ci in range(nslices):
        lo = ci * FF_TILE
        up_g, up_u = nxt
        if ci + 1 < nslices:
            nxt = (up_cols(dff + lo + FF_TILE), up_cols(lo + FF_TILE))
        act_ref[:, lo:lo + FF_TILE] = (_silu(conv_cols(up_g, dff + lo)) * conv_cols(up_u, lo)).astype(BF16)
        if lo + FF_TILE == half:
            out = h + jnp.dot(act_ref[:, :half], wd_ref[:half, :], preferred_element_type=F32)
    out = out + jnp.dot(act_ref[:, half:], wd_ref[half:, :], preferred_element_type=F32)
    o_ref[...] = _rms(out, gf_ref[...]) if final else out


def _ffn(h, g, wu, cw, cb, wd, gf, layer, final):
    b, t, d = h.shape
    tm = TOKEN_TILE
    blk = pl.BlockSpec((None, tm, d), lambda i, j: (i, j, 0))
    return pl.pallas_call(
        functools.partial(_ffn_kernel, final=final),
        grid=(b, t // tm),
        in_specs=[blk, _layer_spec(g, layer), _layer_spec(wu, layer, True), _layer_spec(cw, layer),
                  _layer_spec(cb, layer), _layer_spec(wd, layer, True), _const_spec((1, d))],
        out_specs=blk,
        out_shape=jax.ShapeDtypeStruct((b, t, d), F32),
        scratch_shapes=[pltpu.VMEM((CARRY_ROWS, wu.shape[-1]), F32), pltpu.VMEM((tm, wd.shape[-2]), BF16)],
        compiler_params=_params("parallel", "arbitrary"),
        name="convffn",
    )(h, g, wu, cw, cb, wd, gf)


def _rope_tables(seq):
    half = HEAD_DIM // 2
    inv_freq = ROPE_THETA ** (-jnp.arange(half, dtype=F32) / half)
    ang = jnp.arange(seq).astype(F32)[:, None] * inv_freq[None, :]
    cos, sin = jnp.cos(ang), jnp.sin(ang)
    return jnp.concatenate([cos, cos, cos, cos], axis=1), jnp.concatenate([-sin, sin, -sin, sin], axis=1)


def _split_w_in(w):
    nl, d, _ = w.shape
    o_c = RWKV_IN + 3 * ATTN_WIDTH
    o_g = o_c + 4 * WIDTH
    o_d = o_g + 2 * N_HEADS
    o_m = o_d + HGRN_IN
    w_abc = w[:, :, :o_g].astype(BF16)
    pad = jnp.zeros((nl, d, GATE_LANE - N_HEADS), w.dtype)
    w_gd = jnp.concatenate([w[:, :, o_g:o_g + N_HEADS], pad, w[:, :, o_g + N_HEADS:o_d], pad,
                            w[:, :, o_d:o_m]], axis=2).astype(BF16)
    return w_abc, w_gd, w[:, :, o_m:].astype(BF16)


def _lane_pad(a):
    return jnp.pad(a, ((0, 0), (0, 128 - a.shape[1]))).reshape(a.shape[0], 1, 128)


def kernel(x, norm_mix_g, w_in, b_gate, rwkv_mu, rwkv_w0, rwkv_w2, rwkv_a0, rwkv_a2, rwkv_g2, rwkv_k_k, rwkv_k_a, rwkv_r_k, rwkv_ln_g, rwkv_ln_b, mlstm_conv_w, mlstm_conv_b, mlstm_i_b, mlstm_f_b, hgrn_lb_logits, hgrn_norm_g, p_rwkv, p_attn, p_mlstm, p_hgrn, w_out, norm_ffn_g, w_up, ffn_conv_w, ffn_conv_b, w_down, final_norm_g):
    bsz, seq, d = x.shape
    depth = w_in.shape[0]
    n = bsz * seq
    cos_t, sin_t = _rope_tables(seq)
    w_abc, w_gd, wg = _split_w_in(w_in)
    norm_mix, norm_ffn = _rows3(norm_mix_g), _rows3(norm_ffn_g)
    rwkv_p = dict(mu=_rows3(rwkv_mu), w0=_rows3(rwkv_w0), w2=rwkv_w2, a0=_rows3(rwkv_a0), a2=rwkv_a2,
                  g2=rwkv_g2, k_k=_rows3(rwkv_k_k), k_a=_rows3(rwkv_k_a), r_k=_rows3(rwkv_r_k),
                  ln_g=_rows3(rwkv_ln_g), ln_b=_rows3(rwkv_ln_b))
    mlstm_p = dict(conv_w=mlstm_conv_w, conv_b=_rows3(mlstm_conv_b), i_b=_lane_pad(mlstm_i_b),
                   f_b=_lane_pad(mlstm_f_b))
    hgrn_g = _rows3(hgrn_norm_g)
    bg = _rows3(b_gate)
    ps = [p.astype(BF16) for p in (p_rwkv, p_attn, p_mlstm, p_hgrn)]
    wo, wu, wd = w_out.astype(BF16), w_up.astype(BF16), w_down.astype(BF16)
    ffn_cb = _rows3(ffn_conv_b)
    gf = final_norm_g.reshape(1, d)
    h = x.reshape(n, d)
    for l in range(depth):
        za, zb, zc, zd = _inproj(h, norm_mix, w_abc, w_gd, cos_t, sin_t, seq, l)
        ya = _rwkv(za.reshape(bsz, seq, -1), rwkv_p, l)
        yb = _attention(zb.reshape(bsz, seq, -1))
        yc = _mlstm(zc.reshape(bsz, seq, -1), mlstm_p, l)
        yd = _hgrn(zd.reshape(bsz, seq, -1), hgrn_lb_logits, hgrn_g, l)
        ys = [y.reshape(n, -1) for y in (ya, yb, yc, yd)]
        h = _merge(h, norm_mix, wg, bg, ys, ps, wo, l)
        h = _ffn(h.reshape(bsz, seq, d), norm_ffn, wu, ffn_conv_w, ffn_cb, wd, gf, l, l == depth - 1).reshape(n, d)
    return h.reshape(bsz, seq, d)
```

```python
import functools

import jax
import jax.numpy as jnp
from jax import lax
from jax.experimental import pallas as pl
from jax.experimental.pallas import tpu as pltpu

F32 = jnp.float32
BF16 = jnp.bfloat16

HEAD_DIM = 64
N_HEADS = 4
WIDTH = N_HEADS * HEAD_DIM
CHUNK = 64
BLOCK_HEADS = 2
BLOCK_W = BLOCK_HEADS * HEAD_DIM
N_BLOCKS = N_HEADS // BLOCK_HEADS
RWKV_LORA = 128
RWKV_IN = 3 * WIDTH + RWKV_LORA
RWKV_GN_EPS = 64e-5
ATTN_GROUPS = ((128, 1), (512, 4), (2048, 16))
ATTN_HEADS = 6
ATTN_WIDTH = ATTN_HEADS * HEAD_DIM
ATTN_OUT = 2 * HEAD_DIM
ROPE_THETA = 10000.0
MLSTM_CONV = 4
MLSTM_IN_PAD = 4 * WIDTH + 128
GATE_LANE = 64
HGRN_IN = 4 * WIDTH
N_BRANCHES = 4
FFN_CONV = 3
NORM_EPS = 1e-6
NEG_INF = -1e30

TOKEN_TILE = 512
SEQ_TILE = 256
RWKV_TILE = 512
FF_TILE = 256
CARRY_ROWS = 8
VMEM_LIMIT = 56 * 1024 * 1024


def _dot(a, b):
    return jnp.dot(a.astype(BF16), b.astype(BF16), preferred_element_type=F32)


def _dot_nt(a, b):
    return lax.dot_general(a.astype(BF16), b.astype(BF16), (((1,), (1,)), ((), ())),
                           preferred_element_type=F32)


def _dot_tn(a, b):
    return lax.dot_general(a.astype(BF16), b.astype(BF16), (((0,), (0,)), ((), ())),
                           preferred_element_type=F32)


def _split3(x):
    hi = x.astype(BF16)
    r1 = x - hi.astype(F32)
    mid = r1.astype(BF16)
    lo = (r1 - mid.astype(F32)).astype(BF16)
    return hi, mid, lo


def _sel_dot(m01, x):
    hi, mid, lo = _split3(x)
    return (jnp.dot(m01, hi, preferred_element_type=F32)
            + jnp.dot(m01, mid, preferred_element_type=F32)
            + jnp.dot(m01, lo, preferred_element_type=F32))


def _dot_sel(x, m01):
    hi, mid, lo = _split3(x)
    return (jnp.dot(hi, m01, preferred_element_type=F32)
            + jnp.dot(mid, m01, preferred_element_type=F32)
            + jnp.dot(lo, m01, preferred_element_type=F32))


def _dot_sel2(x, m01):
    hi = x.astype(BF16)
    lo = (x - hi.astype(F32)).astype(BF16)
    return jnp.dot(hi, m01, preferred_element_type=F32) + jnp.dot(lo, m01, preferred_element_type=F32)


def _iota(shape, dim):
    return lax.broadcasted_iota(jnp.int32, shape, dim)


def _onehot(cond):
    return jnp.where(cond, 1.0, 0.0).astype(BF16)


def _chunk_cumsum_matrix(n):
    r, c = _iota((n, n), 0), _iota((n, n), 1)
    return _onehot((jnp.right_shift(r, 6) == jnp.right_shift(c, 6)) & (c <= r))


def _head_sum_matrix(n):
    r, c = _iota((n, n), 0), _iota((n, n), 1)
    return _onehot(jnp.right_shift(r, 6) == jnp.right_shift(c, 6))


def _shift_rows(x, carry, j):
    ext = jnp.concatenate([carry, x], axis=0)
    return pltpu.roll(ext, j, axis=0)[CARRY_ROWS:]


def _softplus(x):
    return jnp.maximum(x, 0.0) + jnp.log1p(jnp.exp(-jnp.abs(x)))


def _sigmoid(x):
    return jax.nn.sigmoid(x)


def _silu(x):
    return x * _sigmoid(x)


def _rms(x, g):
    return x * lax.rsqrt(jnp.mean(x * x, axis=-1, keepdims=True) + NORM_EPS) * g


def _const_spec(shape, single_buffer=False):
    nd = len(shape)
    if single_buffer:
        return pl.BlockSpec(shape, lambda *_: (0,) * nd, pipeline_mode=pl.Buffered(1))
    return pl.BlockSpec(shape, lambda *_: (0,) * nd)


def _layer_spec(a, layer, single_buffer=False):
    shape = (None,) + a.shape[1:]
    nd = a.ndim
    idx = lambda *_: (layer,) + (0,) * (nd - 1)
    if single_buffer:
        return pl.BlockSpec(shape, idx, pipeline_mode=pl.Buffered(1))
    return pl.BlockSpec(shape, idx)


def _rows3(a):
    return a.reshape(a.shape[0], 1, -1)


def _params(*sem):
    return pltpu.CompilerParams(dimension_semantics=sem, vmem_limit_bytes=VMEM_LIMIT)


def _inproj_kernel(x_ref, g_ref, w_ref, wgd_ref, cos_ref, sin_ref, za_ref, zb_ref, zc_ref, zd_ref):
    xb = _rms(x_ref[...], g_ref[...]).astype(BF16)

    def mm(lo, width):
        return jnp.dot(xb, w_ref[:, lo:lo + width], preferred_element_type=F32)

    za_ref[...] = mm(0, RWKV_IN)
    o = RWKV_IN
    qk = mm(o, 2 * ATTN_WIDTH)
    w = 2 * ATTN_WIDTH
    low = (_iota(qk.shape, 1) & (HEAD_DIM - 1)) < HEAD_DIM // 2
    rot = jnp.where(low, pltpu.roll(qk, w - HEAD_DIM // 2, axis=1), pltpu.roll(qk, HEAD_DIM // 2, axis=1))
    cos = jnp.concatenate([cos_ref[...]] * (w // 128), axis=1)
    sin = jnp.concatenate([sin_ref[...]] * (w // 128), axis=1)
    roped = qk * cos + rot * sin
    zb_ref[:, 0:ATTN_WIDTH] = roped[:, 0:ATTN_WIDTH] * (HEAD_DIM ** -0.5)
    zb_ref[:, ATTN_WIDTH:w] = roped[:, ATTN_WIDTH:]
    zb_ref[:, w:] = mm(o + w, ATTN_WIDTH)
    o += 3 * ATTN_WIDTH
    zc_ref[:, 0:4 * WIDTH] = mm(o, 4 * WIDTH)
    gd = jnp.dot(xb, wgd_ref[...], preferred_element_type=F32)
    zc_ref[:, 4 * WIDTH:] = gd[:, 0:128]
    zd_ref[...] = gd[:, 128:]


def _inproj(h, g, w_all, n_abc, w_gd, cos_t, sin_t, seq, layer):
    n, d = h.shape
    tm = TOKEN_TILE
    nt = seq // tm
    widths = (RWKV_IN, 3 * ATTN_WIDTH, MLSTM_IN_PAD, HGRN_IN)
    row = lambda w: pl.BlockSpec((tm, w), lambda i: (i, 0))
    return pl.pallas_call(
        _inproj_kernel,
        grid=(n // tm,),
        in_specs=[row(d), _layer_spec(g, layer),
                  pl.BlockSpec((None, d, n_abc), lambda i: (layer, 0, 0), pipeline_mode=pl.Buffered(1)),
                  _layer_spec(w_gd, layer, True),
                  pl.BlockSpec((tm, 128), lambda i: (i % nt, 0)),
                  pl.BlockSpec((tm, 128), lambda i: (i % nt, 0))],
        out_specs=[row(w) for w in widths],
        out_shape=[jax.ShapeDtypeStruct((n, w), F32) for w in widths],
        compiler_params=_params("parallel"),
        name="inproj",
    )(h, g, w_all, w_gd, cos_t, sin_t)


def _blockdiag(x, bd_mask):
    xb = x.astype(BF16)
    return jnp.concatenate([xb] * BLOCK_HEADS, axis=0) * bd_mask


def _chunk_blocks(x):
    return [x[c * CHUNK:(c + 1) * CHUNK, b * BLOCK_W:(b + 1) * BLOCK_W]
            for c in range(x.shape[0] // CHUNK) for b in range(N_BLOCKS)]


def _fold_diag_blocks(full):
    out = full[0:HEAD_DIM]
    for h in range(1, BLOCK_HEADS):
        out = out + full[h * HEAD_DIM:(h + 1) * HEAD_DIM]
    return out


def _tri_inverse(a, bd_mask, same16, same32, eye):
    bd = lambda m: [_blockdiag(x, bd_mask) for x in m]
    mm = lambda xs, ys: [_dot(x, y) for x, y in zip(xs, ys)]
    n1 = [jnp.where(same16, x, 0.0) for x in a]
    n2 = mm(n1, bd(n1))
    n4 = mm(n2, bd(n2))
    n8 = mm(n4, bd(n4))
    t = [eye + x for x in n1]
    for p in (n2, n4, n8):
        t = [x + y for x, y in zip(t, mm(t, bd(p)))]
    for blk in (jnp.logical_and(same32, jnp.logical_not(same16)), jnp.logical_not(same32)):
        ab = bd([jnp.where(blk, x, 0.0) for x in a])
        t = [x + y for x, y in zip(t, mm(mm(t, ab), bd(t)))]
    return t


def _rwkv_kernel(z_ref, mu_ref, w0_ref, w2_ref, a0_ref, a2_ref, g2_ref, kk_ref, ka_ref, rk_ref,
                 lng_ref, lnb_ref, y_ref, carry_ref, s_ref):
    tt = z_ref.shape[0]

    @pl.when(pl.program_id(1) == 0)
    def _():
        carry_ref[...] = jnp.zeros_like(carry_ref)
        s_ref[...] = jnp.zeros_like(s_ref)

    z = z_ref[...]
    zprev = _shift_rows(z, carry_ref[...], 1)
    carry_ref[...] = z[tt - CARRY_ROWS:, :]
    z = z + mu_ref[...] * (zprev - z)
    r, k, v, lora = z[:, 0:WIDTH], z[:, WIDTH:2 * WIDTH], z[:, 2 * WIDTH:3 * WIDTH], z[:, 3 * WIDTH:]

    def lora_rows(w_ref, lo):
        w = w_ref[...].astype(BF16)
        parts = [jnp.zeros((lo, WIDTH), BF16)] if lo else []
        parts.append(w)
        if lo + w.shape[0] < RWKV_LORA:
            parts.append(jnp.zeros((RWKV_LORA - lo - w.shape[0], WIDTH), BF16))
        return jnp.concatenate(parts, axis=0)

    w2p = lora_rows(w2_ref, 0)
    a2p = lora_rows(a2_ref, w2_ref.shape[0])
    g2p = lora_rows(g2_ref, w2_ref.shape[0] + a2_ref.shape[0])
    logw = -_softplus(-(w0_ref[...] + _dot(jnp.tanh(lora), w2p))) - 0.5
    ld = -jnp.exp(logw)
    a = _sigmoid(a0_ref[...] + _dot(lora, a2p))
    g = _dot(_sigmoid(lora), g2p)

    hsum = _head_sum_matrix(WIDTH)
    kk = k * kk_ref[...]
    kk = kk * lax.rsqrt(_dot_sel2(kk * kk, hsum) + 1e-12)
    k = k * (1.0 + (a - 1.0) * ka_ref[...])
    sa, sb = -kk, kk * a

    cs = _sel_dot(_chunk_cumsum_matrix(tt), ld)
    tot = jnp.concatenate([jnp.broadcast_to(cs[c * CHUNK + CHUNK - 1:(c + 1) * CHUNK, :], (CHUNK, WIDTH))
                           for c in range(tt // CHUNK)], axis=0)
    g_inc, g_exc, g_inv, g_end = jnp.exp(cs), jnp.exp(cs - ld), jnp.exp(-cs), jnp.exp(tot - cs)
    g_tot = jnp.exp(tot)
    rt, at = r * g_inc, sa * g_exc
    bt, kt = sb * g_inv, k * g_inv
    bh, kh = sb * g_end, k * g_end

    n = CHUNK
    nc = tt // n
    bd_mask = _head_sum_matrix(BLOCK_W)
    bd_mask_f = bd_mask.astype(F32)
    ri, ci = _iota((n, BLOCK_W), 0), _iota((n, BLOCK_W), 1) & (n - 1)
    strict, incl, eye = ci < ri, ci <= ri, jnp.where(ci == ri, 1.0, 0.0)
    same16 = jnp.right_shift(ri, 4) == jnp.right_shift(ci, 4)
    same32 = jnp.right_shift(ri, 5) == jnp.right_shift(ci, 5)
    bd = lambda xs: [_blockdiag(x, bd_mask) for x in xs]
    at_c, rt_c, v_c = _chunk_blocks(at), _chunk_blocks(rt), _chunk_blocks(v)

    lhs = [jnp.concatenate([x, y], axis=0) for x, y in zip(at_c, rt_c)]
    rhs = [jnp.concatenate([x, y], axis=0) for x, y in zip(bd(_chunk_blocks(bt)), bd(_chunk_blocks(kt)))]
    gm = [_dot_nt(x, y) for x, y in zip(lhs, rhs)]
    a_ab = [jnp.where(strict, x[:n, :BLOCK_W], 0.0) for x in gm]
    a_rb = [jnp.where(incl, x[n:, :BLOCK_W], 0.0) for x in gm]
    a_ak = [jnp.where(strict, x[:n, BLOCK_W:], 0.0) for x in gm]
    a_rk = [jnp.where(incl, x[n:, BLOCK_W:], 0.0) for x in gm]
    v_bd = bd(v_c)
    av = [_dot(jnp.concatenate([x, y], axis=0), z) for x, y, z in zip(a_ak, a_rk, v_bd)]
    tinv = _tri_inverse(a_ab, bd_mask, same16, same32, eye)
    w12 = [_dot(t, jnp.concatenate([x, y], axis=1)) for t, x, y in zip(tinv, bd(at_c), bd([x[:n] for x in av]))]
    w1 = [x[:, :BLOCK_W] for x in w12]
    w2 = [x[:, BLOCK_W:] for x in w12]
    q12 = [_dot(x, jnp.concatenate([y, u], axis=1)) for x, y, u in zip(a_rb, bd(w1), bd(w2))]
    q1 = [x + y[:, :BLOCK_W] for x, y in zip(rt_c, q12)]
    q2 = [x[n:] + u[:, BLOCK_W:] for x, u in zip(av, q12)]
    bh_c, kh_c = _chunk_blocks(bh), _chunk_blocks(kh)
    p1 = [_dot_tn(x, y) * bd_mask_f for x, y in zip(w1, bh_c)]
    p2 = [_fold_diag_blocks(_dot_tn(jnp.concatenate([x, u], axis=0), jnp.concatenate([y, w], axis=0)) * bd_mask_f)
          for x, u, y, w in zip(w2, v_c, bh_c, kh_c)]

    s = [s_ref[:, b * BLOCK_W:(b + 1) * BLOCK_W] for b in range(N_BLOCKS)]
    y_rows = []
    for c in range(nc):
        y_blocks = []
        for b in range(N_BLOCKS):
            i = c * N_BLOCKS + b
            y_blocks.append(_dot_nt(q1[i], _blockdiag(s[b], bd_mask)) + q2[i])
            s[b] = s[b] * g_tot[c * n:c * n + 1, b * BLOCK_W:(b + 1) * BLOCK_W] + _dot(s[b], p1[i]) + p2[i]
        y_rows.append(jnp.concatenate(y_blocks, axis=1))
    for b in range(N_BLOCKS):
        s_ref[:, b * BLOCK_W:(b + 1) * BLOCK_W] = s[b]
    y = jnp.concatenate(y_rows, axis=0)

    mean = _dot_sel2(y, hsum) * (1.0 / HEAD_DIM)
    yc = y - mean
    var = _dot_sel2(yc * yc, hsum) * (1.0 / HEAD_DIM)
    yn = yc * lax.rsqrt(var + RWKV_GN_EPS) * lng_ref[...] + lnb_ref[...]
    bonus = _dot_sel2(r * k * rk_ref[...], hsum) * v
    y_ref[...] = (yn + bonus) * g


def _rwkv(z, p, layer):
    b, t, _ = z.shape
    tt = RWKV_TILE
    args = (z, p["mu"], p["w0"], p["w2"], p["a0"], p["a2"], p["g2"], p["k_k"], p["k_a"], p["r_k"],
            p["ln_g"], p["ln_b"])
    in_specs = [pl.BlockSpec((None, tt, RWKV_IN), lambda i, j: (i, j, 0))]
    in_specs += [_layer_spec(a, layer) for a in args[1:]]
    return pl.pallas_call(
        _rwkv_kernel,
        grid=(b, t // tt),
        in_specs=in_specs,
        out_specs=pl.BlockSpec((None, tt, WIDTH), lambda i, j: (i, j, 0)),
        out_shape=jax.ShapeDtypeStruct((b, t, WIDTH), F32),
        scratch_shapes=[pltpu.VMEM((CARRY_ROWS, RWKV_IN), F32),
                        pltpu.VMEM((HEAD_DIM, WIDTH), F32)],
        compiler_params=_params("parallel", "arbitrary"),
        name="rwkv7",
    )(*args)


ATTN_BLOCK = 128
ATTN_UNITS = 8


def _attn_group(q_ref, k_ref, v_ref, og_ref, lg_ref, dil, first):
    t = q_ref.shape[0]
    nblk = t // dil // ATTN_BLOCK
    has_prev = nblk > 1
    qi, kj = _iota((ATTN_BLOCK, ATTN_BLOCK), 0), _iota((ATTN_BLOCK, ATTN_BLOCK), 1)
    cur_ok, prev_ok = kj <= qi, kj >= qi
    second = jnp.logical_not(first)

    def step(i, carry):
        rows, prevs, pmasks = [], [], []
        for x in range(ATTN_UNITS):
            u = i * ATTN_UNITS + x
            r, nb = u // nblk, u % nblk
            start = r + nb * (ATTN_BLOCK * dil)
            rows.append(pl.ds(start, ATTN_BLOCK, stride=dil))
            prevs.append(pl.ds(jnp.maximum(start - ATTN_BLOCK * dil, r), ATTN_BLOCK, stride=dil))
            pmasks.append(jnp.logical_and(prev_ok, nb > 0))
        q2 = [q_ref[rw, :] for rw in rows]
        kc = [k_ref[rw, :].astype(BF16) for rw in rows]
        vc = [v_ref[rw, :].astype(BF16) for rw in rows]
        if has_prev:
            kp = [k_ref[rw, :].astype(BF16) for rw in prevs]
            vc = [jnp.concatenate([v_ref[pw, :].astype(BF16), x], axis=0) for pw, x in zip(prevs, vc)]
        heads = []
        for sel in (first, second):
            qj = [jnp.where(sel, x, 0.0) for x in q2]
            s = [jnp.where(cur_ok, _dot_nt(x, y), NEG_INF) for x, y in zip(qj, kc)]
            if has_prev:
                sp = [jnp.where(pm, _dot_nt(x, y), NEG_INF) for pm, x, y in zip(pmasks, qj, kp)]
                s = [jnp.concatenate([x, y], axis=1) for x, y in zip(sp, s)]
            m = [jnp.max(x, axis=-1, keepdims=True) for x in s]
            e = [jnp.exp(x - y) for x, y in zip(s, m)]
            l = [jnp.sum(x, axis=-1, keepdims=True) for x in e]
            o = [_dot(x, y) / z for x, y, z in zip(e, vc, l)]
            lse = [jnp.broadcast_to(x + jnp.log(y), (ATTN_BLOCK, ATTN_OUT)) for x, y in zip(m, l)]
            heads.append((o, lse))
        for x in range(ATTN_UNITS):
            og_ref[rows[x], :] = jnp.where(first, heads[0][0][x], heads[1][0][x])
            lg_ref[rows[x], :] = jnp.where(first, heads[0][1][x], heads[1][1][x])
        return carry

    lax.fori_loop(0, t // ATTN_BLOCK // ATTN_UNITS, step, 0)


def _attn_kernel(q0, k0, v0, q1, k1, v1, q2, k2, v2, o_ref, og0, og1, og2, lg0, lg1, lg2):
    first = _iota((ATTN_BLOCK, ATTN_OUT), 1) < HEAD_DIM
    groups = ((q0, k0, v0, og0, lg0), (q1, k1, v1, og1, lg1), (q2, k2, v2, og2, lg2))
    for (win, dil), refs in zip(ATTN_GROUPS, groups):
        _attn_group(*refs, dil, first)
    l0, l1, l2 = lg0[...], lg1[...], lg2[...]
    mx = jnp.maximum(jnp.maximum(l0, l1), l2)
    w0, w1, w2 = jnp.exp(l0 - mx), jnp.exp(l1 - mx), jnp.exp(l2 - mx)
    o_ref[...] = (w0 * og0[...] + w1 * og1[...] + w2 * og2[...]) / (w0 + w1 + w2)


def _attention(zb):
    b, t, _ = zb.shape
    for win, dil in ATTN_GROUPS:
        assert win // dil == ATTN_BLOCK and t % (dil * ATTN_BLOCK) == 0 and t % (ATTN_BLOCK * ATTN_UNITS) == 0
    col = lambda blk: pl.BlockSpec((None, t, ATTN_OUT), lambda i, blk=blk: (i, 0, blk))
    ng = len(ATTN_GROUPS)
    in_specs = []
    for g in range(ng):
        in_specs += [col(g), col(ng + g), col(2 * ng + g)]
    return pl.pallas_call(
        _attn_kernel,
        grid=(b,),
        in_specs=in_specs,
        out_specs=pl.BlockSpec((None, t, ATTN_OUT), lambda i: (i, 0, 0)),
        out_shape=jax.ShapeDtypeStruct((b, t, ATTN_OUT), F32),
        scratch_shapes=[pltpu.VMEM((t, ATTN_OUT), F32)] * (2 * ng),
        compiler_params=_params("parallel"),
        name="dilated_attention",
    )(*([zb] * (3 * ng)))


def _mlstm_kernel(z_ref, cw_ref, cb_ref, ib_ref, fb_ref, y_ref, carry_ref, c_ref, m_ref):
    tt = z_ref.shape[0]

    @pl.when(pl.program_id(1) == 0)
    def _():
        carry_ref[...] = jnp.zeros_like(carry_ref)
        c_ref[...] = jnp.zeros_like(c_ref)
        m_ref[...] = jnp.zeros_like(m_ref)

    qk_in = z_ref[:, 0:2 * WIDTH]
    ext = jnp.concatenate([carry_ref[...], qk_in], axis=0)
    acc = cb_ref[...] + cw_ref[MLSTM_CONV - 1:MLSTM_CONV, :] * qk_in
    for j in range(1, MLSTM_CONV):
        acc = acc + cw_ref[MLSTM_CONV - 1 - j:MLSTM_CONV - j, :] * pltpu.roll(ext, j, axis=0)[CARRY_ROWS:]
    carry_ref[...] = qk_in[tt - CARRY_ROWS:, :]
    qk = _silu(acc)
    q, k = qk[:, :WIDTH], qk[:, WIDTH:] * (HEAD_DIM ** -0.5)
    v = z_ref[:, 2 * WIDTH:3 * WIDTH]
    og = _sigmoid(z_ref[:, 3 * WIDTH:4 * WIDTH])

    n = CHUNK
    nc = tt // n
    gates = z_ref[:, 4 * WIDTH:]
    gi = gates + ib_ref[...]
    lf = -_softplus(-(pltpu.roll(gates, 128 - GATE_LANE, axis=1) + fb_ref[...]))
    bcs = _sel_dot(_chunk_cumsum_matrix(tt), lf)
    u = gi - bcs
    row_in_chunk = _iota((tt, 128), 0) & (n - 1)
    cmax = u
    for sh in (1, 2, 4, 8, 16, 32):
        cmax = jnp.where(row_in_chunk >= sh, jnp.maximum(cmax, pltpu.roll(cmax, sh, axis=0)), cmax)
    m_prev = m_ref[0:1, :]
    m_rows, scal_rows = [], []
    for c in range(nc):
        last = c * n + n - 1
        b_end = bcs[last:last + 1, :]
        m_new = b_end + jnp.maximum(m_prev, cmax[last:last + 1, :])
        m_rows.append(bcs[c * n:(c + 1) * n, :] + jnp.maximum(m_prev, cmax[c * n:(c + 1) * n, :]))
        scal_rows.append(jnp.concatenate([m_prev, b_end - m_new, jnp.exp(b_end + m_prev - m_new),
                                          jnp.zeros((CARRY_ROWS - 3, 128), F32)], axis=0))
        m_prev = m_new
    m_ref[0:1, :] = m_prev
    m_t = jnp.concatenate(m_rows, axis=0)

    expand = _onehot(_iota((128, WIDTH), 0) == jnp.right_shift(_iota((128, WIDTH), 1), 6))
    bt_x, mt_x, uc_x = _dot_sel(bcs, expand), _dot_sel(m_t, expand), _dot_sel(u, expand)
    scal_x = _dot_sel(jnp.concatenate(scal_rows, axis=0), expand)
    u_t = u.T

    bd_mask = _head_sum_matrix(BLOCK_W)
    bd_mask2 = jnp.concatenate([bd_mask, bd_mask], axis=1)
    bd_mask2_f = bd_mask2.astype(F32)
    ones = jnp.ones((n, BLOCK_W), BF16)
    ri, ci = _iota((n, BLOCK_W), 0), _iota((n, BLOCK_W), 1) & (n - 1)
    causal = ci <= ri
    q_c, k_c, v_c = _chunk_blocks(q), _chunk_blocks(k), _chunk_blocks(v)
    bt_c, mt_c, uc_c = _chunk_blocks(bt_x), _chunk_blocks(mt_x), _chunk_blocks(uc_x)
    scal = lambda c, b, row: scal_x[8 * c + row:8 * c + row + 1, b * BLOCK_W:(b + 1) * BLOCK_W]

    scores, upds = [], []
    for c in range(nc):
        for b in range(N_BLOCKS):
            i = c * N_BLOCKS + b
            u_row = jnp.concatenate([u_t[h:h + 1, c * n:(c + 1) * n]
                                     for h in range(b * BLOCK_HEADS, (b + 1) * BLOCK_HEADS)], axis=1)
            d_intra = jnp.where(causal, bt_c[i] + u_row, -jnp.inf)
            scores.append(_dot_nt(q_c[i], _blockdiag(k_c[i], bd_mask)) * jnp.exp(d_intra - mt_c[i]))
            w_k = jnp.exp(uc_c[i] + scal(c, b, 1))
            upds.append(_dot_tn(k_c[i] * w_k, jnp.concatenate([v_c[i].astype(BF16), ones], axis=1)) * bd_mask2_f)
    intra = [_dot(s, jnp.concatenate([_blockdiag(x, bd_mask), bd_mask], axis=1)) for s, x in zip(scores, v_c)]

    state = [c_ref[b] for b in range(N_BLOCKS)]
    y_rows = []
    for c in range(nc):
        y_blocks = []
        for b in range(N_BLOCKS):
            i = c * N_BLOCKS + b
            inter = _dot(q_c[i], state[b])
            w_inter = jnp.exp(bt_c[i] + scal(c, b, 0) - mt_c[i])
            num = w_inter * inter[:, :BLOCK_W] + intra[i][:, :BLOCK_W]
            den = w_inter * inter[:, BLOCK_W:] + intra[i][:, BLOCK_W:]
            y_blocks.append(num / jnp.maximum(jnp.abs(den), jnp.exp(-mt_c[i])))
            dec = scal(c, b, 2)
            state[b] = jnp.concatenate([dec, dec], axis=1) * state[b] + upds[i]
        y_rows.append(jnp.concatenate(y_blocks, axis=1))
    for b in range(N_BLOCKS):
        c_ref[b] = state[b]
    y_ref[...] = og * jnp.concatenate(y_rows, axis=0)


def _mlstm(z, p, layer):
    b, t, _ = z.shape
    tt = SEQ_TILE
    args = (z, p["conv_w"], p["conv_b"], p["i_b"], p["f_b"])
    in_specs = [pl.BlockSpec((None, tt, MLSTM_IN_PAD), lambda i, j: (i, j, 0))]
    in_specs += [_layer_spec(a, layer) for a in args[1:]]
    return pl.pallas_call(
        _mlstm_kernel,
        grid=(b, t // tt),
        in_specs=in_specs,
        out_specs=pl.BlockSpec((None, tt, WIDTH), lambda i, j: (i, j, 0)),
        out_shape=jax.ShapeDtypeStruct((b, t, WIDTH), F32),
        scratch_shapes=[pltpu.VMEM((CARRY_ROWS, 2 * WIDTH), F32),
                        pltpu.VMEM((N_BLOCKS, BLOCK_W, 2 * BLOCK_W), F32),
                        pltpu.VMEM((8, 128), F32)],
        compiler_params=_params("parallel", "arbitrary"),
        name="mlstm",
    )(*args)


def _hgrn_kernel(z_ref, lbl_ref, ng_ref, y_ref, s_ref, sel_ref, *, layer):
    tt = z_ref.shape[0]

    @pl.when(pl.program_id(1) == 0)
    def _():
        s_ref[...] = jnp.zeros_like(s_ref)

    logits = lbl_ref[...]
    pe = jnp.exp(logits - jnp.max(logits, axis=0, keepdims=True))
    pr = pe / jnp.sum(pe, axis=0, keepdims=True)
    lb = pr[0:1, :]
    for i in range(1, layer + 1):
        lb = lb + pr[i:i + 1, :]
    lb = lb - pr[0:1, :]

    q = _silu(z_ref[:, 0:WIDTH])
    f = lb + (1.0 - lb) * _sigmoid(z_ref[:, WIDTH:2 * WIDTH])
    k = 1.0 - f
    v = z_ref[:, 2 * WIDTH:3 * WIDTH]
    og = _sigmoid(z_ref[:, 3 * WIDTH:4 * WIDTH])
    gl = jnp.log(f)

    @pl.when(pl.program_id(1) == 0)
    def _():
        r, c = _iota((tt, tt), 0), _iota((tt, tt), 1)
        sel_ref[0] = _chunk_cumsum_matrix(tt)
        for lv in range(1, 7):
            sel_ref[lv] = _onehot(c == jnp.right_shift(r, lv) * (1 << lv) + (1 << (lv - 1)) - 1)

    bsum = _sel_dot(sel_ref[0], gl)
    b_end = jnp.concatenate([jnp.broadcast_to(bsum[c * CHUNK + CHUNK - 1:(c + 1) * CHUNK, :], (CHUNK, WIDTH))
                             for c in range(tt // CHUNK)], axis=0)
    q_in = q * jnp.exp(bsum)
    k_end = k * jnp.exp(b_end - bsum)
    g_tot = jnp.exp(b_end)

    n = CHUNK
    nc = tt // n
    hsum = _head_sum_matrix(WIDTH)
    bd_mask = _head_sum_matrix(BLOCK_W)
    bd_mask_f = bd_mask.astype(F32)
    ri, ci = _iota((n, BLOCK_W), 0), _iota((n, BLOCK_W), 1) & (n - 1)
    row_t = _iota((tt, 1), 0)
    bd = lambda xs: [_blockdiag(x, bd_mask) for x in xs]
    b_mids = jnp.dot(sel_ref[1:7].reshape(6 * tt, tt), bsum.astype(BF16), preferred_element_type=F32)

    def level_operands(lv):
        size, half = 1 << lv, 1 << (lv - 1)
        b_mid = b_mids[(lv - 1) * tt:lv * tt]
        right = (row_t & (size - 1)) >= half
        e = jnp.exp(jnp.where(right, bsum - b_mid, b_mid - bsum))
        return _chunk_blocks(jnp.where(right, q * e, 0.0)), bd(_chunk_blocks(jnp.where(right, 0.0, k * e)))

    nxt = level_operands(1)
    attn = [jnp.where(ri == ci, _dot_nt(x, y), 0.0) for x, y in zip(_chunk_blocks(q), bd(_chunk_blocks(k)))]
    for lv in range(1, 7):
        q_l, k_l = nxt
        if lv < 6:
            nxt = level_operands(lv + 1)
        same = jnp.right_shift(ri, lv) == jnp.right_shift(ci, lv)
        attn = [a + jnp.where(same, _dot_nt(x, y), 0.0) for a, x, y in zip(attn, q_l, k_l)]
    v_c = _chunk_blocks(v)
    intra = [_dot(a, y) for a, y in zip(attn, bd(v_c))]
    upd = [_fold_diag_blocks(_dot_tn(x, y) * bd_mask_f) for x, y in zip(v_c, _chunk_blocks(k_end))]

    s = [s_ref[:, b * BLOCK_W:(b + 1) * BLOCK_W] for b in range(N_BLOCKS)]
    y_rows = []
    q_in_c = _chunk_blocks(q_in)
    for c in range(nc):
        y_blocks = []
        for b in range(N_BLOCKS):
            i = c * N_BLOCKS + b
            y_blocks.append(_dot_nt(q_in_c[i], _blockdiag(s[b], bd_mask)) + intra[i])
            s[b] = s[b] * g_tot[c * n:c * n + 1, b * BLOCK_W:(b + 1) * BLOCK_W] + upd[i]
        y_rows.append(jnp.concatenate(y_blocks, axis=1))
    for b in range(N_BLOCKS):
        s_ref[:, b * BLOCK_W:(b + 1) * BLOCK_W] = s[b]
    o = jnp.concatenate(y_rows, axis=0)
    ms = _dot_sel2(o * o, hsum) * (1.0 / HEAD_DIM)
    y_ref[...] = o * lax.rsqrt(ms + NORM_EPS) * ng_ref[...] * og


def _hgrn(z, lb_logits, norm_g, layer):
    b, t, _ = z.shape
    tt = SEQ_TILE
    return pl.pallas_call(
        functools.partial(_hgrn_kernel, layer=layer),
        grid=(b, t // tt),
        in_specs=[pl.BlockSpec((None, tt, HGRN_IN), lambda i, j: (i, j, 0)),
                  _const_spec(lb_logits.shape), _layer_spec(norm_g, layer)],
        out_specs=pl.BlockSpec((None, tt, WIDTH), lambda i, j: (i, j, 0)),
        out_shape=jax.ShapeDtypeStruct((b, t, WIDTH), F32),
        scratch_shapes=[pltpu.VMEM((HEAD_DIM, WIDTH), F32), pltpu.VMEM((7, tt, tt), BF16)],
        compiler_params=_params("parallel", "arbitrary"),
        name="hgrn2",
    )(z, lb_logits, norm_g)


def _merge_kernel(h_ref, g_ref, wg_ref, bg_ref, ya_ref, yb_ref, yc_ref, yd_ref,
                  pa_ref, pb_ref, pc_ref, pd_ref, wo_ref, o_ref):
    h = h_ref[...]
    d = h.shape[1]
    xb = _rms(h, g_ref[...]).astype(BF16)
    merged = None
    for i, (y_ref, p_ref) in enumerate(((ya_ref, pa_ref), (yb_ref, pb_ref), (yc_ref, pc_ref), (yd_ref, pd_ref))):
        gate = _sigmoid(jnp.dot(xb, wg_ref[:, i * d:(i + 1) * d], preferred_element_type=F32)
                        + bg_ref[:, i * d:(i + 1) * d])
        term = gate * jnp.dot(y_ref[...].astype(BF16), p_ref[...], preferred_element_type=F32)
        merged = term if merged is None else merged + term
    o_ref[...] = h + jnp.dot(merged.astype(BF16), wo_ref[...], preferred_element_type=F32)


def _merge(h, g, wg, bg, ys, ps, wo, layer):
    n, d = h.shape
    tm = TOKEN_TILE
    row = lambda w: pl.BlockSpec((tm, w), lambda i: (i, 0))
    in_specs = [row(d), _layer_spec(g, layer), _layer_spec(wg, layer, True), _layer_spec(bg, layer)]
    in_specs += [row(y.shape[1]) for y in ys]
    in_specs += [_layer_spec(p, layer, True) for p in ps]
    in_specs += [_layer_spec(wo, layer, True)]
    return pl.pallas_call(
        _merge_kernel,
        grid=(n // tm,),
        in_specs=in_specs,
        out_specs=row(d),
        out_shape=jax.ShapeDtypeStruct((n, d), F32),
        compiler_params=_params("parallel"),
        name="merge",
    )(h, g, wg, bg, *ys, *ps, wo)


def _ffn_kernel(h_ref, g_ref, wu_ref, cw_ref, cb_ref, wd_ref, gf_ref, o_ref, carry_ref, act_ref, *, final):
    tm = h_ref.shape[0]
    dff = wd_ref.shape[0]

    @pl.when(pl.program_id(1) == 0)
    def _():
        carry_ref[...] = jnp.zeros_like(carry_ref)

    h = h_ref[...]
    xb = _rms(h, g_ref[...]).astype(BF16)

    def up_cols(lo):
        return jnp.dot(xb, wu_ref[:, lo:lo + FF_TILE], preferred_element_type=F32)

    def conv_cols(up, lo):
        ext = jnp.concatenate([carry_ref[:, lo:lo + FF_TILE], up], axis=0)
        carry_ref[:, lo:lo + FF_TILE] = up[tm - CARRY_ROWS:, :]
        out = cb_ref[:, lo:lo + FF_TILE] + cw_ref[FFN_CONV - 1:FFN_CONV, lo:lo + FF_TILE] * up
        for j in range(1, FFN_CONV):
            out = out + (cw_ref[FFN_CONV - 1 - j:FFN_CONV - j, lo:lo + FF_TILE]
                         * pltpu.roll(ext, j, axis=0)[CARRY_ROWS:])
        return out

    nslices = dff // FF_TILE
    half = (nslices // 2) * FF_TILE
    nxt = (up_cols(dff), up_cols(0))
    for ci in range(nslices):
        lo = ci * FF_TILE
        up_g, up_u = nxt
        if ci + 1 < nslices:
            nxt = (up_cols(dff + lo + FF_TILE), up_cols(lo + FF_TILE))
        act_ref[:, lo:lo + FF_TILE] = (_silu(conv_cols(up_g, dff + lo)) * conv_cols(up_u, lo)).astype(BF16)
        if lo + FF_TILE == half:
            out = h + jnp.dot(act_ref[:, :half], wd_ref[:half, :], preferred_element_type=F32)
    out = out + jnp.dot(act_ref[:, half:], wd_ref[half:, :], preferred_element_type=F32)
    o_ref[...] = _rms(out, gf_ref[...]) if final else out


def _ffn(h, g, wu, cw, cb, wd, gf, layer, final):
    b, t, d = h.shape
    tm = TOKEN_TILE
    blk = pl.BlockSpec((None, tm, d), lambda i, j: (i, j, 0))
    return pl.pallas_call(
        functools.partial(_ffn_kernel, final=final),
        grid=(b, t // tm),
        in_specs=[blk, _layer_spec(g, layer), _layer_spec(wu, layer, True), _layer_spec(cw, layer),
                  _layer_spec(cb, layer), _layer_spec(wd, layer, True), _const_spec((1, d))],
        out_specs=blk,
        out_shape=jax.ShapeDtypeStruct((b, t, d), F32),
        scratch_shapes=[pltpu.VMEM((CARRY_ROWS, wu.shape[-1]), F32), pltpu.VMEM((tm, wd.shape[-2]), BF16)],
        compiler_params=_params("parallel", "arbitrary"),
        name="convffn",
    )(h, g, wu, cw, cb, wd, gf)


def _rope_tables(seq):
    half = HEAD_DIM // 2
    inv_freq = ROPE_THETA ** (-jnp.arange(half, dtype=F32) / half)
    ang = jnp.arange(seq).astype(F32)[:, None] * inv_freq[None, :]
    cos, sin = jnp.cos(ang), jnp.sin(ang)
    return jnp.concatenate([cos, cos, cos, cos], axis=1), jnp.concatenate([-sin, sin, -sin, sin], axis=1)


def _split_w_in(w):
    nl, d, _ = w.shape
    o_c = RWKV_IN + 3 * ATTN_WIDTH
    o_g = o_c + 4 * WIDTH
    o_d = o_g + 2 * N_HEADS
    o_m = o_d + HGRN_IN
    wb = w.astype(BF16)
    pad = jnp.zeros((nl, d, GATE_LANE - N_HEADS), BF16)
    w_gd = jnp.concatenate([wb[:, :, o_g:o_g + N_HEADS], pad, wb[:, :, o_g + N_HEADS:o_d], pad,
                            wb[:, :, o_d:o_m]], axis=2)
    return wb, o_g, w_gd, wb[:, :, o_m:]


def _lane_pad(a):
    return jnp.pad(a, ((0, 0), (0, 128 - a.shape[1]))).reshape(a.shape[0], 1, 128)


def kernel(x, norm_mix_g, w_in, b_gate, rwkv_mu, rwkv_w0, rwkv_w2, rwkv_a0, rwkv_a2, rwkv_g2, rwkv_k_k, rwkv_k_a, rwkv_r_k, rwkv_ln_g, rwkv_ln_b, mlstm_conv_w, mlstm_conv_b, mlstm_i_b, mlstm_f_b, hgrn_lb_logits, hgrn_norm_g, p_rwkv, p_attn, p_mlstm, p_hgrn, w_out, norm_ffn_g, w_up, ffn_conv_w, ffn_conv_b, w_down, final_norm_g):
    bsz, seq, d = x.shape
    depth = w_in.shape[0]
    n = bsz * seq
    cos_t, sin_t = _rope_tables(seq)
    w_all, n_abc, w_gd, wg = _split_w_in(w_in)
    norm_mix, norm_ffn = _rows3(norm_mix_g), _rows3(norm_ffn_g)
    rwkv_p = dict(mu=_rows3(rwkv_mu), w0=_rows3(rwkv_w0), w2=rwkv_w2, a0=_rows3(rwkv_a0), a2=rwkv_a2,
                  g2=rwkv_g2, k_k=_rows3(rwkv_k_k), k_a=_rows3(rwkv_k_a), r_k=_rows3(rwkv_r_k),
                  ln_g=_rows3(rwkv_ln_g), ln_b=_rows3(rwkv_ln_b))
    mlstm_p = dict(conv_w=mlstm_conv_w, conv_b=_rows3(mlstm_conv_b), i_b=_lane_pad(mlstm_i_b),
                   f_b=_lane_pad(mlstm_f_b))
    hgrn_g = _rows3(hgrn_norm_g)
    bg = _rows3(b_gate)
    ps = [p.astype(BF16) for p in (p_rwkv, p_attn, p_mlstm, p_hgrn)]
    wo, wu, wd = w_out.astype(BF16), w_up.astype(BF16), w_down.astype(BF16)
    ffn_cb = _rows3(ffn_conv_b)
    gf = final_norm_g.reshape(1, d)
    h = x.reshape(n, d)
    for l in range(depth):
        za, zb, zc, zd = _inproj(h, norm_mix, w_all, n_abc, w_gd, cos_t, sin_t, seq, l)
        ya = _rwkv(za.reshape(bsz, seq, -1), rwkv_p, l)
        yb = _attention(zb.reshape(bsz, seq, -1))
        yc = _mlstm(zc.reshape(bsz, seq, -1), mlstm_p, l)
        yd = _hgrn(zd.reshape(bsz, seq, -1), hgrn_lb_logits, hgrn_g, l)
        ys = [y.reshape(n, -1) for y in (ya, yb, yc, yd)]
        h = _merge(h, norm_mix, wg, bg, ys, ps, wo, l)
        h = _ffn(h.reshape(bsz, seq, d), norm_ffn, wu, ffn_conv_w, ffn_cb, wd, gf, l, l == depth - 1).reshape(n, d)
    return h.reshape(bsz, seq, d)
```

```python
import functools

import jax
import jax.numpy as jnp
from jax import lax
from jax.experimental import pallas as pl
from jax.experimental.pallas import tpu as pltpu

F32 = jnp.float32
BF16 = jnp.bfloat16

HEAD_DIM = 64
N_HEADS = 4
WIDTH = N_HEADS * HEAD_DIM
CHUNK = 64
BLOCK_HEADS = 2
BLOCK_W = BLOCK_HEADS * HEAD_DIM
N_BLOCKS = N_HEADS // BLOCK_HEADS
RWKV_LORA = 128
RWKV_IN = 3 * WIDTH + RWKV_LORA
RWKV_GN_EPS = 64e-5
ATTN_GROUPS = ((128, 1), (512, 4), (2048, 16))
ATTN_HEADS = 6
ATTN_WIDTH = ATTN_HEADS * HEAD_DIM
ATTN_OUT = 2 * HEAD_DIM
ROPE_THETA = 10000.0
MLSTM_CONV = 4
MLSTM_IN_PAD = 4 * WIDTH + 128
GATE_LANE = 64
HGRN_IN = 4 * WIDTH
N_BRANCHES = 4
FFN_CONV = 3
NORM_EPS = 1e-6
NEG_INF = -1e30

TOKEN_TILE = 512
SEQ_TILE = 256
RWKV_TILE = 512
FF_TILE = 256
CARRY_ROWS = 8
VMEM_LIMIT = 56 * 1024 * 1024


def _dot(a, b):
    return jnp.dot(a.astype(BF16), b.astype(BF16), preferred_element_type=F32)


def _dot_nt(a, b):
    return lax.dot_general(a.astype(BF16), b.astype(BF16), (((1,), (1,)), ((), ())),
                           preferred_element_type=F32)


def _dot_tn(a, b):
    return lax.dot_general(a.astype(BF16), b.astype(BF16), (((0,), (0,)), ((), ())),
                           preferred_element_type=F32)


def _split3(x):
    hi = x.astype(BF16)
    r1 = x - hi.astype(F32)
    mid = r1.astype(BF16)
    lo = (r1 - mid.astype(F32)).astype(BF16)
    return hi, mid, lo


def _sel_dot(m01, x):
    hi, mid, lo = _split3(x)
    return (jnp.dot(m01, hi, preferred_element_type=F32)
            + jnp.dot(m01, mid, preferred_element_type=F32)
            + jnp.dot(m01, lo, preferred_element_type=F32))


def _dot_sel(x, m01):
    hi, mid, lo = _split3(x)
    return (jnp.dot(hi, m01, preferred_element_type=F32)
            + jnp.dot(mid, m01, preferred_element_type=F32)
            + jnp.dot(lo, m01, preferred_element_type=F32))


def _dot_sel2(x, m01):
    hi = x.astype(BF16)
    lo = (x - hi.astype(F32)).astype(BF16)
    return jnp.dot(hi, m01, preferred_element_type=F32) + jnp.dot(lo, m01, preferred_element_type=F32)


def _iota(shape, dim):
    return lax.broadcasted_iota(jnp.int32, shape, dim)


def _onehot(cond):
    return jnp.where(cond, 1.0, 0.0).astype(BF16)


def _chunk_cumsum_matrix(n):
    r, c = _iota((n, n), 0), _iota((n, n), 1)
    return _onehot((jnp.right_shift(r, 6) == jnp.right_shift(c, 6)) & (c <= r))


def _head_sum_matrix(n):
    r, c = _iota((n, n), 0), _iota((n, n), 1)
    return _onehot(jnp.right_shift(r, 6) == jnp.right_shift(c, 6))


def _shift_rows(x, carry, j):
    ext = jnp.concatenate([carry, x], axis=0)
    return pltpu.roll(ext, j, axis=0)[CARRY_ROWS:]


def _softplus(x):
    return jnp.maximum(x, 0.0) + jnp.log1p(jnp.exp(-jnp.abs(x)))


def _sigmoid(x):
    return jax.nn.sigmoid(x)


def _silu(x):
    return x * _sigmoid(x)


def _rms(x, g):
    return x * lax.rsqrt(jnp.mean(x * x, axis=-1, keepdims=True) + NORM_EPS) * g


def _const_spec(shape, single_buffer=False):
    nd = len(shape)
    if single_buffer:
        return pl.BlockSpec(shape, lambda *_: (0,) * nd, pipeline_mode=pl.Buffered(1))
    return pl.BlockSpec(shape, lambda *_: (0,) * nd)


def _layer_spec(a, layer, single_buffer=False):
    shape = (None,) + a.shape[1:]
    nd = a.ndim
    idx = lambda *_: (layer,) + (0,) * (nd - 1)
    if single_buffer:
        return pl.BlockSpec(shape, idx, pipeline_mode=pl.Buffered(1))
    return pl.BlockSpec(shape, idx)


def _rows3(a):
    return a.reshape(a.shape[0], 1, -1)


def _params(*sem):
    return pltpu.CompilerParams(dimension_semantics=sem, vmem_limit_bytes=VMEM_LIMIT)


def _inproj_kernel(x_ref, g_ref, w_ref, wgd_ref, cos_ref, sin_ref, za_ref, zb_ref, zc_ref, zd_ref):
    xb = _rms(x_ref[...], g_ref[...]).astype(BF16)

    def mm(lo, width):
        return _dot_nt(xb, w_ref[lo:lo + width, :])

    za_ref[...] = mm(0, RWKV_IN)
    o = RWKV_IN
    qk = mm(o, 2 * ATTN_WIDTH)
    w = 2 * ATTN_WIDTH
    low = (_iota(qk.shape, 1) & (HEAD_DIM - 1)) < HEAD_DIM // 2
    rot = jnp.where(low, pltpu.roll(qk, w - HEAD_DIM // 2, axis=1), pltpu.roll(qk, HEAD_DIM // 2, axis=1))
    cos = jnp.concatenate([cos_ref[...]] * (w // 128), axis=1)
    sin = jnp.concatenate([sin_ref[...]] * (w // 128), axis=1)
    roped = qk * cos + rot * sin
    zb_ref[:, 0:ATTN_WIDTH] = roped[:, 0:ATTN_WIDTH] * (HEAD_DIM ** -0.5)
    zb_ref[:, ATTN_WIDTH:w] = roped[:, ATTN_WIDTH:]
    zb_ref[:, w:] = mm(o + w, ATTN_WIDTH)
    o += 3 * ATTN_WIDTH
    zc_ref[:, 0:4 * WIDTH] = mm(o, 4 * WIDTH)
    gd = _dot_nt(xb, wgd_ref[...])
    zc_ref[:, 4 * WIDTH:] = gd[:, 0:128]
    zd_ref[...] = gd[:, 128:]


def _inproj(h, g, w_all, n_abc, w_gd, cos_t, sin_t, seq, layer):
    n, d = h.shape
    tm = TOKEN_TILE
    nt = seq // tm
    widths = (RWKV_IN, 3 * ATTN_WIDTH, MLSTM_IN_PAD, HGRN_IN)
    row = lambda w: pl.BlockSpec((tm, w), lambda i: (i, 0))
    return pl.pallas_call(
        _inproj_kernel,
        grid=(n // tm,),
        in_specs=[row(d), _layer_spec(g, layer),
                  pl.BlockSpec((None, n_abc, d), lambda i: (layer, 0, 0), pipeline_mode=pl.Buffered(1)),
                  _layer_spec(w_gd, layer, True),
                  pl.BlockSpec((tm, 128), lambda i: (i % nt, 0)),
                  pl.BlockSpec((tm, 128), lambda i: (i % nt, 0))],
        out_specs=[row(w) for w in widths],
        out_shape=[jax.ShapeDtypeStruct((n, w), F32) for w in widths],
        compiler_params=_params("parallel"),
        name="inproj",
    )(h, g, w_all, w_gd, cos_t, sin_t)


def _blockdiag(x, bd_mask):
    xb = x.astype(BF16)
    return jnp.concatenate([xb] * BLOCK_HEADS, axis=0) * bd_mask


def _chunk_blocks(x):
    return [x[c * CHUNK:(c + 1) * CHUNK, b * BLOCK_W:(b + 1) * BLOCK_W]
            for c in range(x.shape[0] // CHUNK) for b in range(N_BLOCKS)]


def _fold_diag_blocks(full):
    out = full[0:HEAD_DIM]
    for h in range(1, BLOCK_HEADS):
        out = out + full[h * HEAD_DIM:(h + 1) * HEAD_DIM]
    return out


def _tri_inverse(a, bd_mask, same16, same32, eye):
    bd = lambda m: [_blockdiag(x, bd_mask) for x in m]
    mm = lambda xs, ys: [_dot(x, y) for x, y in zip(xs, ys)]
    n1 = [jnp.where(same16, x, 0.0) for x in a]
    n2 = mm(n1, bd(n1))
    n4 = mm(n2, bd(n2))
    n8 = mm(n4, bd(n4))
    t = [eye + x for x in n1]
    for p in (n2, n4, n8):
        t = [x + y for x, y in zip(t, mm(t, bd(p)))]
    for blk in (jnp.logical_and(same32, jnp.logical_not(same16)), jnp.logical_not(same32)):
        ab = bd([jnp.where(blk, x, 0.0) for x in a])
        t = [x + y for x, y in zip(t, mm(mm(t, ab), bd(t)))]
    return t


def _rwkv_kernel(z_ref, mu_ref, w0_ref, w2_ref, a0_ref, a2_ref, g2_ref, kk_ref, ka_ref, rk_ref,
                 lng_ref, lnb_ref, y_ref, carry_ref, s_ref):
    tt = z_ref.shape[0]

    @pl.when(pl.program_id(1) == 0)
    def _():
        carry_ref[...] = jnp.zeros_like(carry_ref)
        s_ref[...] = jnp.zeros_like(s_ref)

    z = z_ref[...]
    zprev = _shift_rows(z, carry_ref[...], 1)
    carry_ref[...] = z[tt - CARRY_ROWS:, :]
    z = z + mu_ref[...] * (zprev - z)
    r, k, v, lora = z[:, 0:WIDTH], z[:, WIDTH:2 * WIDTH], z[:, 2 * WIDTH:3 * WIDTH], z[:, 3 * WIDTH:]

    def lora_rows(w_ref, lo):
        w = w_ref[...].astype(BF16)
        parts = [jnp.zeros((lo, WIDTH), BF16)] if lo else []
        parts.append(w)
        if lo + w.shape[0] < RWKV_LORA:
            parts.append(jnp.zeros((RWKV_LORA - lo - w.shape[0], WIDTH), BF16))
        return jnp.concatenate(parts, axis=0)

    w2p = lora_rows(w2_ref, 0)
    a2p = lora_rows(a2_ref, w2_ref.shape[0])
    g2p = lora_rows(g2_ref, w2_ref.shape[0] + a2_ref.shape[0])
    logw = -_softplus(-(w0_ref[...] + _dot(jnp.tanh(lora), w2p))) - 0.5
    ld = -jnp.exp(logw)
    a = _sigmoid(a0_ref[...] + _dot(lora, a2p))
    g = _dot(_sigmoid(lora), g2p)

    hsum = _head_sum_matrix(WIDTH)
    kk = k * kk_ref[...]
    kk = kk * lax.rsqrt(_dot_sel2(kk * kk, hsum) + 1e-12)
    k = k * (1.0 + (a - 1.0) * ka_ref[...])
    sa, sb = -kk, kk * a

    cs = _sel_dot(_chunk_cumsum_matrix(tt), ld)
    tot = jnp.concatenate([jnp.broadcast_to(cs[c * CHUNK + CHUNK - 1:(c + 1) * CHUNK, :], (CHUNK, WIDTH))
                           for c in range(tt // CHUNK)], axis=0)
    g_inc, g_exc, g_inv, g_end = jnp.exp(cs), jnp.exp(cs - ld), jnp.exp(-cs), jnp.exp(tot - cs)
    g_tot = jnp.exp(tot)
    rt, at = r * g_inc, sa * g_exc
    bt, kt = sb * g_inv, k * g_inv
    bh, kh = sb * g_end, k * g_end

    n = CHUNK
    nc = tt // n
    bd_mask = _head_sum_matrix(BLOCK_W)
    bd_mask_f = bd_mask.astype(F32)
    ri, ci = _iota((n, BLOCK_W), 0), _iota((n, BLOCK_W), 1) & (n - 1)
    strict, incl, eye = ci < ri, ci <= ri, jnp.where(ci == ri, 1.0, 0.0)
    same16 = jnp.right_shift(ri, 4) == jnp.right_shift(ci, 4)
    same32 = jnp.right_shift(ri, 5) == jnp.right_shift(ci, 5)
    bd = lambda xs: [_blockdiag(x, bd_mask) for x in xs]
    at_c, rt_c, v_c = _chunk_blocks(at), _chunk_blocks(rt), _chunk_blocks(v)

    lhs = [jnp.concatenate([x, y], axis=0) for x, y in zip(at_c, rt_c)]
    rhs = [jnp.concatenate([x, y], axis=0) for x, y in zip(bd(_chunk_blocks(bt)), bd(_chunk_blocks(kt)))]
    gm = [_dot_nt(x, y) for x, y in zip(lhs, rhs)]
    a_ab = [jnp.where(strict, x[:n, :BLOCK_W], 0.0) for x in gm]
    a_rb = [jnp.where(incl, x[n:, :BLOCK_W], 0.0) for x in gm]
    a_ak = [jnp.where(strict, x[:n, BLOCK_W:], 0.0) for x in gm]
    a_rk = [jnp.where(incl, x[n:, BLOCK_W:], 0.0) for x in gm]
    v_bd = bd(v_c)
    av = [_dot(jnp.concatenate([x, y], axis=0), z) for x, y, z in zip(a_ak, a_rk, v_bd)]
    tinv = _tri_inverse(a_ab, bd_mask, same16, same32, eye)
    w12 = [_dot(t, jnp.concatenate([x, y], axis=1)) for t, x, y in zip(tinv, bd(at_c), bd([x[:n] for x in av]))]
    w1 = [x[:, :BLOCK_W] for x in w12]
    w2 = [x[:, BLOCK_W:] for x in w12]
    q12 = [_dot(x, jnp.concatenate([y, u], axis=1)) for x, y, u in zip(a_rb, bd(w1), bd(w2))]
    q1 = [x + y[:, :BLOCK_W] for x, y in zip(rt_c, q12)]
    q2 = [x[n:] + u[:, BLOCK_W:] for x, u in zip(av, q12)]
    bh_c, kh_c = _chunk_blocks(bh), _chunk_blocks(kh)
    p1 = [_dot_tn(x, y) * bd_mask_f for x, y in zip(w1, bh_c)]
    p2 = [_fold_diag_blocks(_dot_tn(jnp.concatenate([x, u], axis=0), jnp.concatenate([y, w], axis=0)) * bd_mask_f)
          for x, u, y, w in zip(w2, v_c, bh_c, kh_c)]

    s = [s_ref[:, b * BLOCK_W:(b + 1) * BLOCK_W] for b in range(N_BLOCKS)]
    y_rows = []
    for c in range(nc):
        y_blocks = []
        for b in range(N_BLOCKS):
            i = c * N_BLOCKS + b
            y_blocks.append(_dot_nt(q1[i], _blockdiag(s[b], bd_mask)) + q2[i])
            s[b] = s[b] * g_tot[c * n:c * n + 1, b * BLOCK_W:(b + 1) * BLOCK_W] + _dot(s[b], p1[i]) + p2[i]
        y_rows.append(jnp.concatenate(y_blocks, axis=1))
    for b in range(N_BLOCKS):
        s_ref[:, b * BLOCK_W:(b + 1) * BLOCK_W] = s[b]
    y = jnp.concatenate(y_rows, axis=0)

    mean = _dot_sel2(y, hsum) * (1.0 / HEAD_DIM)
    yc = y - mean
    var = _dot_sel2(yc * yc, hsum) * (1.0 / HEAD_DIM)
    yn = yc * lax.rsqrt(var + RWKV_GN_EPS) * lng_ref[...] + lnb_ref[...]
    bonus = _dot_sel2(r * k * rk_ref[...], hsum) * v
    y_ref[...] = (yn + bonus) * g


def _rwkv(z, p, layer):
    b, t, _ = z.shape
    tt = RWKV_TILE
    args = (z, p["mu"], p["w0"], p["w2"], p["a0"], p["a2"], p["g2"], p["k_k"], p["k_a"], p["r_k"],
            p["ln_g"], p["ln_b"])
    in_specs = [pl.BlockSpec((None, tt, RWKV_IN), lambda i, j: (i, j, 0))]
    in_specs += [_layer_spec(a, layer) for a in args[1:]]
    return pl.pallas_call(
        _rwkv_kernel,
        grid=(b, t // tt),
        in_specs=in_specs,
        out_specs=pl.BlockSpec((None, tt, WIDTH), lambda i, j: (i, j, 0)),
        out_shape=jax.ShapeDtypeStruct((b, t, WIDTH), F32),
        scratch_shapes=[pltpu.VMEM((CARRY_ROWS, RWKV_IN), F32),
                        pltpu.VMEM((HEAD_DIM, WIDTH), F32)],
        compiler_params=_params("parallel", "arbitrary"),
        name="rwkv7",
    )(*args)


ATTN_BLOCK = 128
ATTN_UNITS = 16


def _attn_group(q_ref, k_ref, v_ref, og_ref, lg_ref, dil, first):
    t = q_ref.shape[0]
    nblk = t // dil // ATTN_BLOCK
    has_prev = nblk > 1
    qi, kj = _iota((ATTN_BLOCK, ATTN_BLOCK), 0), _iota((ATTN_BLOCK, ATTN_BLOCK), 1)
    cur_ok, prev_ok = kj <= qi, kj >= qi
    second = jnp.logical_not(first)

    def step(i, carry):
        rows, prevs, pmasks = [], [], []
        for x in range(ATTN_UNITS):
            u = i * ATTN_UNITS + x
            r, nb = u // nblk, u % nblk
            start = r + nb * (ATTN_BLOCK * dil)
            rows.append(pl.ds(start, ATTN_BLOCK, stride=dil))
            prevs.append(pl.ds(jnp.maximum(start - ATTN_BLOCK * dil, r), ATTN_BLOCK, stride=dil))
            pmasks.append(jnp.logical_and(prev_ok, nb > 0))
        q2 = [q_ref[rw, :] for rw in rows]
        kc = [k_ref[rw, :].astype(BF16) for rw in rows]
        vc = [v_ref[rw, :].astype(BF16) for rw in rows]
        if has_prev:
            kp = [k_ref[rw, :].astype(BF16) for rw in prevs]
            vc = [jnp.concatenate([v_ref[pw, :].astype(BF16), x], axis=0) for pw, x in zip(prevs, vc)]
        heads = []
        for sel in (first, second):
            qj = [jnp.where(sel, x, 0.0) for x in q2]
            s = [jnp.where(cur_ok, _dot_nt(x, y), NEG_INF) for x, y in zip(qj, kc)]
            if has_prev:
                sp = [jnp.where(pm, _dot_nt(x, y), NEG_INF) for pm, x, y in zip(pmasks, qj, kp)]
                s = [jnp.concatenate([x, y], axis=1) for x, y in zip(sp, s)]
            m = [jnp.max(x, axis=-1, keepdims=True) for x in s]
            e = [jnp.exp(x - y) for x, y in zip(s, m)]
            l = [jnp.sum(x, axis=-1, keepdims=True) for x in e]
            o = [_dot(x, y) / z for x, y, z in zip(e, vc, l)]
            lse = [jnp.broadcast_to(x + jnp.log(y), (ATTN_BLOCK, ATTN_OUT)) for x, y in zip(m, l)]
            heads.append((o, lse))
        for x in range(ATTN_UNITS):
            og_ref[rows[x], :] = jnp.where(first, heads[0][0][x], heads[1][0][x])
            lg_ref[rows[x], :] = jnp.where(first, heads[0][1][x], heads[1][1][x])
        return carry

    lax.fori_loop(0, t // ATTN_BLOCK // ATTN_UNITS, step, 0)


def _attn_kernel(q0, k0, v0, q1, k1, v1, q2, k2, v2, o_ref, og0, og1, og2, lg0, lg1, lg2):
    first = _iota((ATTN_BLOCK, ATTN_OUT), 1) < HEAD_DIM
    groups = ((q0, k0, v0, og0, lg0), (q1, k1, v1, og1, lg1), (q2, k2, v2, og2, lg2))
    for (win, dil), refs in zip(ATTN_GROUPS, groups):
        _attn_group(*refs, dil, first)
    l0, l1, l2 = lg0[...], lg1[...], lg2[...]
    mx = jnp.maximum(jnp.maximum(l0, l1), l2)
    w0, w1, w2 = jnp.exp(l0 - mx), jnp.exp(l1 - mx), jnp.exp(l2 - mx)
    o_ref[...] = (w0 * og0[...] + w1 * og1[...] + w2 * og2[...]) / (w0 + w1 + w2)


def _attention(zb):
    b, t, _ = zb.shape
    for win, dil in ATTN_GROUPS:
        assert win // dil == ATTN_BLOCK and t % (dil * ATTN_BLOCK) == 0 and t % (ATTN_BLOCK * ATTN_UNITS) == 0
    col = lambda blk: pl.BlockSpec((None, t, ATTN_OUT), lambda i, blk=blk: (i, 0, blk))
    ng = len(ATTN_GROUPS)
    in_specs = []
    for g in range(ng):
        in_specs += [col(g), col(ng + g), col(2 * ng + g)]
    return pl.pallas_call(
        _attn_kernel,
        grid=(b,),
        in_specs=in_specs,
        out_specs=pl.BlockSpec((None, t, ATTN_OUT), lambda i: (i, 0, 0)),
        out_shape=jax.ShapeDtypeStruct((b, t, ATTN_OUT), F32),
        scratch_shapes=[pltpu.VMEM((t, ATTN_OUT), F32)] * (2 * ng),
        compiler_params=_params("parallel"),
        name="dilated_attention",
    )(*([zb] * (3 * ng)))


def _mlstm_kernel(z_ref, cw_ref, cb_ref, ib_ref, fb_ref, y_ref, carry_ref, c_ref, m_ref):
    tt = z_ref.shape[0]

    @pl.when(pl.program_id(1) == 0)
    def _():
        carry_ref[...] = jnp.zeros_like(carry_ref)
        c_ref[...] = jnp.zeros_like(c_ref)
        m_ref[...] = jnp.zeros_like(m_ref)

    qk_in = z_ref[:, 0:2 * WIDTH]
    ext = jnp.concatenate([carry_ref[...], qk_in], axis=0)
    acc = cb_ref[...] + cw_ref[MLSTM_CONV - 1:MLSTM_CONV, :] * qk_in
    for j in range(1, MLSTM_CONV):
        acc = acc + cw_ref[MLSTM_CONV - 1 - j:MLSTM_CONV - j, :] * pltpu.roll(ext, j, axis=0)[CARRY_ROWS:]
    carry_ref[...] = qk_in[tt - CARRY_ROWS:, :]
    qk = _silu(acc)
    q, k = qk[:, :WIDTH], qk[:, WIDTH:] * (HEAD_DIM ** -0.5)
    v = z_ref[:, 2 * WIDTH:3 * WIDTH]
    og = _sigmoid(z_ref[:, 3 * WIDTH:4 * WIDTH])

    n = CHUNK
    nc = tt // n
    gates = z_ref[:, 4 * WIDTH:]
    gi = gates + ib_ref[...]
    lf = -_softplus(-(pltpu.roll(gates, 128 - GATE_LANE, axis=1) + fb_ref[...]))
    bcs = _sel_dot(_chunk_cumsum_matrix(tt), lf)
    u = gi - bcs
    row_in_chunk = _iota((tt, 128), 0) & (n - 1)
    cmax = u
    for sh in (1, 2, 4, 8, 16, 32):
        cmax = jnp.where(row_in_chunk >= sh, jnp.maximum(cmax, pltpu.roll(cmax, sh, axis=0)), cmax)
    m_prev = m_ref[0:1, :]
    m_rows, scal_rows = [], []
    for c in range(nc):
        last = c * n + n - 1
        b_end = bcs[last:last + 1, :]
        m_new = b_end + jnp.maximum(m_prev, cmax[last:last + 1, :])
        m_rows.append(bcs[c * n:(c + 1) * n, :] + jnp.maximum(m_prev, cmax[c * n:(c + 1) * n, :]))
        scal_rows.append(jnp.concatenate([m_prev, b_end - m_new, jnp.exp(b_end + m_prev - m_new),
                                          jnp.zeros((CARRY_ROWS - 3, 128), F32)], axis=0))
        m_prev = m_new
    m_ref[0:1, :] = m_prev
    m_t = jnp.concatenate(m_rows, axis=0)

    expand = _onehot(_iota((128, WIDTH), 0) == jnp.right_shift(_iota((128, WIDTH), 1), 6))
    bt_x, mt_x, uc_x = _dot_sel(bcs, expand), _dot_sel(m_t, expand), _dot_sel(u, expand)
    scal_x = _dot_sel(jnp.concatenate(scal_rows, axis=0), expand)
    u_t = u.T

    bd_mask = _head_sum_matrix(BLOCK_W)
    bd_mask2 = jnp.concatenate([bd_mask, bd_mask], axis=1)
    bd_mask2_f = bd_mask2.astype(F32)
    ones = jnp.ones((n, BLOCK_W), BF16)
    ri, ci = _iota((n, BLOCK_W), 0), _iota((n, BLOCK_W), 1) & (n - 1)
    causal = ci <= ri
    q_c, k_c, v_c = _chunk_blocks(q), _chunk_blocks(k), _chunk_blocks(v)
    bt_c, mt_c, uc_c = _chunk_blocks(bt_x), _chunk_blocks(mt_x), _chunk_blocks(uc_x)
    scal = lambda c, b, row: scal_x[8 * c + row:8 * c + row + 1, b * BLOCK_W:(b + 1) * BLOCK_W]

    scores, upds = [], []
    for c in range(nc):
        for b in range(N_BLOCKS):
            i = c * N_BLOCKS + b
            u_row = jnp.concatenate([u_t[h:h + 1, c * n:(c + 1) * n]
                                     for h in range(b * BLOCK_HEADS, (b + 1) * BLOCK_HEADS)], axis=1)
            d_intra = jnp.where(causal, bt_c[i] + u_row, -jnp.inf)
            scores.append(_dot_nt(q_c[i], _blockdiag(k_c[i], bd_mask)) * jnp.exp(d_intra - mt_c[i]))
            w_k = jnp.exp(uc_c[i] + scal(c, b, 1))
            upds.append(_dot_tn(k_c[i] * w_k, jnp.concatenate([v_c[i].astype(BF16), ones], axis=1)) * bd_mask2_f)
    intra = [_dot(s, jnp.concatenate([_blockdiag(x, bd_mask), bd_mask], axis=1)) for s, x in zip(scores, v_c)]

    state = [c_ref[b] for b in range(N_BLOCKS)]
    y_rows = []
    for c in range(nc):
        y_blocks = []
        for b in range(N_BLOCKS):
            i = c * N_BLOCKS + b
            inter = _dot(q_c[i], state[b])
            w_inter = jnp.exp(bt_c[i] + scal(c, b, 0) - mt_c[i])
            num = w_inter * inter[:, :BLOCK_W] + intra[i][:, :BLOCK_W]
            den = w_inter * inter[:, BLOCK_W:] + intra[i][:, BLOCK_W:]
            y_blocks.append(num / jnp.maximum(jnp.abs(den), jnp.exp(-mt_c[i])))
            dec = scal(c, b, 2)
            state[b] = jnp.concatenate([dec, dec], axis=1) * state[b] + upds[i]
        y_rows.append(jnp.concatenate(y_blocks, axis=1))
    for b in range(N_BLOCKS):
        c_ref[b] = state[b]
    y_ref[...] = og * jnp.concatenate(y_rows, axis=0)


def _mlstm(z, p, layer):
    b, t, _ = z.shape
    tt = SEQ_TILE
    args = (z, p["conv_w"], p["conv_b"], p["i_b"], p["f_b"])
    in_specs = [pl.BlockSpec((None, tt, MLSTM_IN_PAD), lambda i, j: (i, j, 0))]
    in_specs += [_layer_spec(a, layer) for a in args[1:]]
    return pl.pallas_call(
        _mlstm_kernel,
        grid=(b, t // tt),
        in_specs=in_specs,
        out_specs=pl.BlockSpec((None, tt, WIDTH), lambda i, j: (i, j, 0)),
        out_shape=jax.ShapeDtypeStruct((b, t, WIDTH), F32),
        scratch_shapes=[pltpu.VMEM((CARRY_ROWS, 2 * WIDTH), F32),
                        pltpu.VMEM((N_BLOCKS, BLOCK_W, 2 * BLOCK_W), F32),
                        pltpu.VMEM((8, 128), F32)],
        compiler_params=_params("parallel", "arbitrary"),
        name="mlstm",
    )(*args)


def _hgrn_kernel(z_ref, lbl_ref, ng_ref, y_ref, s_ref, sel_ref, *, layer):
    tt = z_ref.shape[0]

    @pl.when(pl.program_id(1) == 0)
    def _():
        s_ref[...] = jnp.zeros_like(s_ref)

    logits = lbl_ref[...]
    pe = jnp.exp(logits - jnp.max(logits, axis=0, keepdims=True))
    pr = pe / jnp.sum(pe, axis=0, keepdims=True)
    lb = pr[0:1, :]
    for i in range(1, layer + 1):
        lb = lb + pr[i:i + 1, :]
    lb = lb - pr[0:1, :]

    q = _silu(z_ref[:, 0:WIDTH])
    f = lb + (1.0 - lb) * _sigmoid(z_ref[:, WIDTH:2 * WIDTH])
    k = 1.0 - f
    v = z_ref[:, 2 * WIDTH:3 * WIDTH]
    og = _sigmoid(z_ref[:, 3 * WIDTH:4 * WIDTH])
    gl = jnp.log(f)

    @pl.when(pl.program_id(1) == 0)
    def _():
        r, c = _iota((tt, tt), 0), _iota((tt, tt), 1)
        sel_ref[0] = _chunk_cumsum_matrix(tt)
        for lv in range(1, 7):
            sel_ref[lv] = _onehot(c == jnp.right_shift(r, lv) * (1 << lv) + (1 << (lv - 1)) - 1)

    bsum = _sel_dot(sel_ref[0], gl)
    b_end = jnp.concatenate([jnp.broadcast_to(bsum[c * CHUNK + CHUNK - 1:(c + 1) * CHUNK, :], (CHUNK, WIDTH))
                             for c in range(tt // CHUNK)], axis=0)
    q_in = q * jnp.exp(bsum)
    k_end = k * jnp.exp(b_end - bsum)
    g_tot = jnp.exp(b_end)

    n = CHUNK
    nc = tt // n
    hsum = _head_sum_matrix(WIDTH)
    bd_mask = _head_sum_matrix(BLOCK_W)
    bd_mask_f = bd_mask.astype(F32)
    ri, ci = _iota((n, BLOCK_W), 0), _iota((n, BLOCK_W), 1) & (n - 1)
    row_t = _iota((tt, 1), 0)
    bd = lambda xs: [_blockdiag(x, bd_mask) for x in xs]
    b_mids = jnp.dot(sel_ref[1:7].reshape(6 * tt, tt), bsum.astype(BF16), preferred_element_type=F32)

    def level_operands(lv):
        size, half = 1 << lv, 1 << (lv - 1)
        b_mid = b_mids[(lv - 1) * tt:lv * tt]
        right = (row_t & (size - 1)) >= half
        e = jnp.exp(jnp.where(right, bsum - b_mid, b_mid - bsum))
        return _chunk_blocks(jnp.where(right, q * e, 0.0)), bd(_chunk_blocks(jnp.where(right, 0.0, k * e)))

    nxt = level_operands(1)
    attn = [jnp.where(ri == ci, _dot_nt(x, y), 0.0) for x, y in zip(_chunk_blocks(q), bd(_chunk_blocks(k)))]
    for lv in range(1, 7):
        q_l, k_l = nxt
        if lv < 6:
            nxt = level_operands(lv + 1)
        same = jnp.right_shift(ri, lv) == jnp.right_shift(ci, lv)
        attn = [a + jnp.where(same, _dot_nt(x, y), 0.0) for a, x, y in zip(attn, q_l, k_l)]
    v_c = _chunk_blocks(v)
    intra = [_dot(a, y) for a, y in zip(attn, bd(v_c))]
    upd = [_fold_diag_blocks(_dot_tn(x, y) * bd_mask_f) for x, y in zip(v_c, _chunk_blocks(k_end))]

    s = [s_ref[:, b * BLOCK_W:(b + 1) * BLOCK_W] for b in range(N_BLOCKS)]
    y_rows = []
    q_in_c = _chunk_blocks(q_in)
    for c in range(nc):
        y_blocks = []
        for b in range(N_BLOCKS):
            i = c * N_BLOCKS + b
            y_blocks.append(_dot_nt(q_in_c[i], _blockdiag(s[b], bd_mask)) + intra[i])
            s[b] = s[b] * g_tot[c * n:c * n + 1, b * BLOCK_W:(b + 1) * BLOCK_W] + upd[i]
        y_rows.append(jnp.concatenate(y_blocks, axis=1))
    for b in range(N_BLOCKS):
        s_ref[:, b * BLOCK_W:(b + 1) * BLOCK_W] = s[b]
    o = jnp.concatenate(y_rows, axis=0)
    ms = _dot_sel2(o * o, hsum) * (1.0 / HEAD_DIM)
    y_ref[...] = o * lax.rsqrt(ms + NORM_EPS) * ng_ref[...] * og


def _hgrn(z, lb_logits, norm_g, layer):
    b, t, _ = z.shape
    tt = SEQ_TILE
    return pl.pallas_call(
        functools.partial(_hgrn_kernel, layer=layer),
        grid=(b, t // tt),
        in_specs=[pl.BlockSpec((None, tt, HGRN_IN), lambda i, j: (i, j, 0)),
                  _const_spec(lb_logits.shape), _layer_spec(norm_g, layer)],
        out_specs=pl.BlockSpec((None, tt, WIDTH), lambda i, j: (i, j, 0)),
        out_shape=jax.ShapeDtypeStruct((b, t, WIDTH), F32),
        scratch_shapes=[pltpu.VMEM((HEAD_DIM, WIDTH), F32), pltpu.VMEM((7, tt, tt), BF16)],
        compiler_params=_params("parallel", "arbitrary"),
        name="hgrn2",
    )(z, lb_logits, norm_g)


def _merge_kernel(h_ref, g_ref, wg_ref, bg_ref, ya_ref, yb_ref, yc_ref, yd_ref,
                  pa_ref, pb_ref, pc_ref, pd_ref, wo_ref, o_ref):
    h = h_ref[...]
    d = h.shape[1]
    xb = _rms(h, g_ref[...]).astype(BF16)
    merged = None
    for i, (y_ref, p_ref) in enumerate(((ya_ref, pa_ref), (yb_ref, pb_ref), (yc_ref, pc_ref), (yd_ref, pd_ref))):
        gate = _sigmoid(_dot_nt(xb, wg_ref[i * d:(i + 1) * d, :]) + bg_ref[:, i * d:(i + 1) * d])
        term = gate * jnp.dot(y_ref[...].astype(BF16), p_ref[...], preferred_element_type=F32)
        merged = term if merged is None else merged + term
    o_ref[...] = h + jnp.dot(merged.astype(BF16), wo_ref[...], preferred_element_type=F32)


def _merge(h, g, wg, bg, ys, ps, wo, layer):
    n, d = h.shape
    tm = TOKEN_TILE
    row = lambda w: pl.BlockSpec((tm, w), lambda i: (i, 0))
    in_specs = [row(d), _layer_spec(g, layer), _layer_spec(wg, layer, True), _layer_spec(bg, layer)]
    in_specs += [row(y.shape[1]) for y in ys]
    in_specs += [_layer_spec(p, layer, True) for p in ps]
    in_specs += [_layer_spec(wo, layer, True)]
    return pl.pallas_call(
        _merge_kernel,
        grid=(n // tm,),
        in_specs=in_specs,
        out_specs=row(d),
        out_shape=jax.ShapeDtypeStruct((n, d), F32),
        compiler_params=_params("parallel"),
        name="merge",
    )(h, g, wg, bg, *ys, *ps, wo)


def _ffn_kernel(h_ref, g_ref, wu_ref, cw_ref, cb_ref, wd_ref, gf_ref, o_ref, carry_ref, act_ref, *, final):
    tm = h_ref.shape[0]
    dff = wd_ref.shape[0]

    @pl.when(pl.program_id(1) == 0)
    def _():
        carry_ref[...] = jnp.zeros_like(carry_ref)

    h = h_ref[...]
    xb = _rms(h, g_ref[...]).astype(BF16)

    def up_cols(lo):
        return jnp.dot(xb, wu_ref[:, lo:lo + FF_TILE], preferred_element_type=F32)

    def conv_cols(up, lo):
        ext = jnp.concatenate([carry_ref[:, lo:lo + FF_TILE], up], axis=0)
        carry_ref[:, lo:lo + FF_TILE] = up[tm - CARRY_ROWS:, :]
        out = cb_ref[:, lo:lo + FF_TILE] + cw_ref[FFN_CONV - 1:FFN_CONV, lo:lo + FF_TILE] * up
        for j in range(1, FFN_CONV):
            out = out + (cw_ref[FFN_CONV - 1 - j:FFN_CONV - j, lo:lo + FF_TILE]
                         * pltpu.roll(ext, j, axis=0)[CARRY_ROWS:])
        return out

    nslices = dff // FF_TILE
    half = (nslices // 2) * FF_TILE
    nxt = (up_cols(dff), up_cols(0))
    for ci in range(nslices):
        lo = ci * FF_TILE
        up_g, up_u = nxt
        if ci + 1 < nslices:
            nxt = (up_cols(dff + lo + FF_TILE), up_cols(lo + FF_TILE))
        act_ref[:, lo:lo + FF_TILE] = (_silu(conv_cols(up_g, dff + lo)) * conv_cols(up_u, lo)).astype(BF16)
        if lo + FF_TILE == half:
            out = h + jnp.dot(act_ref[:, :half], wd_ref[:half, :], preferred_element_type=F32)
    out = out + jnp.dot(act_ref[:, half:], wd_ref[half:, :], preferred_element_type=F32)
    o_ref[...] = _rms(out, gf_ref[...]) if final else out


def _ffn(h, g, wu, cw, cb, wd, gf, layer, final):
    b, t, d = h.shape
    tm = TOKEN_TILE
    blk = pl.BlockSpec((None, tm, d), lambda i, j: (i, j, 0))
    return pl.pallas_call(
        functools.partial(_ffn_kernel, final=final),
        grid=(b, t // tm),
        in_specs=[blk, _layer_spec(g, layer), _layer_spec(wu, layer, True), _layer_spec(cw, layer),
                  _layer_spec(cb, layer), _layer_spec(wd, layer, True), _const_spec((1, d))],
        out_specs=blk,
        out_shape=jax.ShapeDtypeStruct((b, t, d), F32),
        scratch_shapes=[pltpu.VMEM((CARRY_ROWS, wu.shape[-1]), F32), pltpu.VMEM((tm, wd.shape[-2]), BF16)],
        compiler_params=_params("parallel", "arbitrary"),
        name="convffn",
    )(h, g, wu, cw, cb, wd, gf)


def _rope_tables(seq):
    half = HEAD_DIM // 2
    inv_freq = ROPE_THETA ** (-jnp.arange(half, dtype=F32) / half)
    ang = jnp.arange(seq).astype(F32)[:, None] * inv_freq[None, :]
    cos, sin = jnp.cos(ang), jnp.sin(ang)
    return jnp.concatenate([cos, cos, cos, cos], axis=1), jnp.concatenate([-sin, sin, -sin, sin], axis=1)


def _split_w_in(w):
    nl, d, _ = w.shape
    o_c = RWKV_IN + 3 * ATTN_WIDTH
    o_g = o_c + 4 * WIDTH
    o_d = o_g + 2 * N_HEADS
    o_m = o_d + HGRN_IN
    wt = jnp.swapaxes(w, 1, 2).astype(BF16)
    pad = jnp.zeros((nl, GATE_LANE - N_HEADS, d), BF16)
    w_gd = jnp.concatenate([wt[:, o_g:o_g + N_HEADS], pad, wt[:, o_g + N_HEADS:o_d], pad, wt[:, o_d:o_m]], axis=1)
    return wt, o_g, w_gd, wt[:, o_m:]


def _lane_pad(a):
    return jnp.pad(a, ((0, 0), (0, 128 - a.shape[1]))).reshape(a.shape[0], 1, 128)


def kernel(x, norm_mix_g, w_in, b_gate, rwkv_mu, rwkv_w0, rwkv_w2, rwkv_a0, rwkv_a2, rwkv_g2, rwkv_k_k, rwkv_k_a, rwkv_r_k, rwkv_ln_g, rwkv_ln_b, mlstm_conv_w, mlstm_conv_b, mlstm_i_b, mlstm_f_b, hgrn_lb_logits, hgrn_norm_g, p_rwkv, p_attn, p_mlstm, p_hgrn, w_out, norm_ffn_g, w_up, ffn_conv_w, ffn_conv_b, w_down, final_norm_g):
    bsz, seq, d = x.shape
    depth = w_in.shape[0]
    n = bsz * seq
    cos_t, sin_t = _rope_tables(seq)
    w_all, n_abc, w_gd, wg = _split_w_in(w_in)
    norm_mix, norm_ffn = _rows3(norm_mix_g), _rows3(norm_ffn_g)
    rwkv_p = dict(mu=_rows3(rwkv_mu), w0=_rows3(rwkv_w0), w2=rwkv_w2, a0=_rows3(rwkv_a0), a2=rwkv_a2,
                  g2=rwkv_g2, k_k=_rows3(rwkv_k_k), k_a=_rows3(rwkv_k_a), r_k=_rows3(rwkv_r_k),
                  ln_g=_rows3(rwkv_ln_g), ln_b=_rows3(rwkv_ln_b))
    mlstm_p = dict(conv_w=mlstm_conv_w, conv_b=_rows3(mlstm_conv_b), i_b=_lane_pad(mlstm_i_b),
                   f_b=_lane_pad(mlstm_f_b))
    hgrn_g = _rows3(hgrn_norm_g)
    bg = _rows3(b_gate)
    ps = [p.astype(BF16) for p in (p_rwkv, p_attn, p_mlstm, p_hgrn)]
    wo, wu, wd = w_out.astype(BF16), w_up.astype(BF16), w_down.astype(BF16)
    ffn_cb = _rows3(ffn_conv_b)
    gf = final_norm_g.reshape(1, d)
    h = x.reshape(n, d)
    for l in range(depth):
        za, zb, zc, zd = _inproj(h, norm_mix, w_all, n_abc, w_gd, cos_t, sin_t, seq, l)
        ya = _rwkv(za.reshape(bsz, seq, -1), rwkv_p, l)
        yb = _attention(zb.reshape(bsz, seq, -1))
        yc = _mlstm(zc.reshape(bsz, seq, -1), mlstm_p, l)
        yd = _hgrn(zd.reshape(bsz, seq, -1), hgrn_lb_logits, hgrn_g, l)
        ys = [y.reshape(n, -1) for y in (ya, yb, yc, yd)]
        h = _merge(h, norm_mix, wg, bg, ys, ps, wo, l)
        h = _ffn(h.reshape(bsz, seq, d), norm_ffn, wu, ffn_conv_w, ffn_cb, wd, gf, l, l == depth - 1).reshape(n, d)
    return h.reshape(bsz, seq, d)
```

```python
import functools

import jax
import jax.numpy as jnp
from jax import lax
from jax.experimental import pallas as pl
from jax.experimental.pallas import tpu as pltpu

F32 = jnp.float32
BF16 = jnp.bfloat16

LANES = 128
SUBLANES = 8
HEAD_DIM = 64
HEAD_SHIFT = 6
N_HEADS = 4
WIDTH = N_HEADS * HEAD_DIM
CHUNK = 64
BLOCK_HEADS = 2
BLOCK_W = BLOCK_HEADS * HEAD_DIM
N_BLOCKS = N_HEADS // BLOCK_HEADS
RWKV_LORA = 32 + 32 + 64
RWKV_IN = 3 * WIDTH + RWKV_LORA
RWKV_GN_EPS = 64e-5
ATTN_GROUPS = ((128, 1), (512, 4), (2048, 16))
ATTN_HEADS = 6
ATTN_WIDTH = ATTN_HEADS * HEAD_DIM
ATTN_OUT = 2 * HEAD_DIM
ROPE_THETA = 10000.0
MLSTM_CONV = 4
MLSTM_IN_PAD = 4 * WIDTH + LANES
GATE_LANE = 64
HGRN_IN = 4 * WIDTH
FFN_CONV = 3
NORM_EPS = 1e-6
NEG_INF = -1e30

TOKEN_TILE = 512
SEQ_TILE = 256
RWKV_TILE = 512
FF_TILE = 256
CARRY_ROWS = SUBLANES
VMEM_LIMIT = 56 * 1024 * 1024


def _dot(a, b):
    return jnp.dot(a.astype(BF16), b.astype(BF16), preferred_element_type=F32)


def _dot_nt(a, b):
    return lax.dot_general(a.astype(BF16), b.astype(BF16), (((1,), (1,)), ((), ())),
                           preferred_element_type=F32)


def _dot_tn(a, b):
    return lax.dot_general(a.astype(BF16), b.astype(BF16), (((0,), (0,)), ((), ())),
                           preferred_element_type=F32)


def _split3(x):
    hi = x.astype(BF16)
    r1 = x - hi.astype(F32)
    mid = r1.astype(BF16)
    lo = (r1 - mid.astype(F32)).astype(BF16)
    return hi, mid, lo


def _sel_dot(m01, x):
    hi, mid, lo = _split3(x)
    return (jnp.dot(m01, hi, preferred_element_type=F32)
            + jnp.dot(m01, mid, preferred_element_type=F32)
            + jnp.dot(m01, lo, preferred_element_type=F32))


def _dot_sel(x, m01):
    hi, mid, lo = _split3(x)
    return (jnp.dot(hi, m01, preferred_element_type=F32)
            + jnp.dot(mid, m01, preferred_element_type=F32)
            + jnp.dot(lo, m01, preferred_element_type=F32))


def _dot_sel2(x, m01):
    hi = x.astype(BF16)
    lo = (x - hi.astype(F32)).astype(BF16)
    return jnp.dot(hi, m01, preferred_element_type=F32) + jnp.dot(lo, m01, preferred_element_type=F32)


def _iota(shape, dim):
    return lax.broadcasted_iota(jnp.int32, shape, dim)


def _onehot(cond):
    return jnp.where(cond, 1.0, 0.0).astype(BF16)


def _chunk_cumsum_matrix(n):
    r, c = _iota((n, n), 0), _iota((n, n), 1)
    return _onehot((jnp.right_shift(r, HEAD_SHIFT) == jnp.right_shift(c, HEAD_SHIFT)) & (c <= r))


def _head_sum_matrix(n):
    r, c = _iota((n, n), 0), _iota((n, n), 1)
    return _onehot(jnp.right_shift(r, HEAD_SHIFT) == jnp.right_shift(c, HEAD_SHIFT))


def _shift_rows(x, carry, j):
    ext = jnp.concatenate([carry, x], axis=0)
    return pltpu.roll(ext, j, axis=0)[CARRY_ROWS:]


def _softplus(x):
    return jnp.maximum(x, 0.0) + jnp.log1p(jnp.exp(-jnp.abs(x)))


def _sigmoid(x):
    return jax.nn.sigmoid(x)


def _silu(x):
    return x * _sigmoid(x)


def _rms(x, g):
    return x * lax.rsqrt(jnp.mean(x * x, axis=-1, keepdims=True) + NORM_EPS) * g


def _const_spec(shape, single_buffer=False):
    nd = len(shape)
    if single_buffer:
        return pl.BlockSpec(shape, lambda *_: (0,) * nd, pipeline_mode=pl.Buffered(1))
    return pl.BlockSpec(shape, lambda *_: (0,) * nd)


def _layer_spec(a, layer, single_buffer=False):
    shape = (None,) + a.shape[1:]
    nd = a.ndim
    idx = lambda *_: (layer,) + (0,) * (nd - 1)
    if single_buffer:
        return pl.BlockSpec(shape, idx, pipeline_mode=pl.Buffered(1))
    return pl.BlockSpec(shape, idx)


def _rows3(a):
    return a.reshape(a.shape[0], 1, -1)


def _params(*sem):
    return pltpu.CompilerParams(dimension_semantics=sem, vmem_limit_bytes=VMEM_LIMIT)


def _inproj_kernel(x_ref, g_ref, w_ref, wgd_ref, cos_ref, sin_ref, za_ref, zb_ref, zc_ref, zd_ref):
    xb = _rms(x_ref[...], g_ref[...]).astype(BF16)

    def mm(lo, width):
        return _dot_nt(xb, w_ref[lo:lo + width, :])

    za_ref[...] = mm(0, RWKV_IN)
    o = RWKV_IN
    qk = mm(o, 2 * ATTN_WIDTH)
    w = 2 * ATTN_WIDTH
    low = (_iota(qk.shape, 1) & (HEAD_DIM - 1)) < HEAD_DIM // 2
    rot = jnp.where(low, pltpu.roll(qk, w - HEAD_DIM // 2, axis=1), pltpu.roll(qk, HEAD_DIM // 2, axis=1))
    cos = jnp.concatenate([cos_ref[...]] * (w // LANES), axis=1)
    sin = jnp.concatenate([sin_ref[...]] * (w // LANES), axis=1)
    roped = qk * cos + rot * sin
    zb_ref[:, 0:ATTN_WIDTH] = roped[:, 0:ATTN_WIDTH] * (HEAD_DIM ** -0.5)
    zb_ref[:, ATTN_WIDTH:w] = roped[:, ATTN_WIDTH:]
    zb_ref[:, w:] = mm(o + w, ATTN_WIDTH)
    o += 3 * ATTN_WIDTH
    zc_ref[:, 0:4 * WIDTH] = mm(o, 4 * WIDTH)
    gd = _dot_nt(xb, wgd_ref[...])
    zc_ref[:, 4 * WIDTH:] = gd[:, 0:LANES]
    zd_ref[...] = gd[:, LANES:]


def _inproj(h, g, w_all, n_abc, w_gd, cos_t, sin_t, seq, layer):
    n, d = h.shape
    tm = TOKEN_TILE
    nt = seq // tm
    widths = (RWKV_IN, 3 * ATTN_WIDTH, MLSTM_IN_PAD, HGRN_IN)
    row = lambda w: pl.BlockSpec((tm, w), lambda i: (i, 0))
    return pl.pallas_call(
        _inproj_kernel,
        grid=(n // tm,),
        in_specs=[row(d), _layer_spec(g, layer),
                  pl.BlockSpec((None, n_abc, d), lambda i: (layer, 0, 0), pipeline_mode=pl.Buffered(1)),
                  _layer_spec(w_gd, layer, True),
                  pl.BlockSpec((tm, LANES), lambda i: (i % nt, 0)),
                  pl.BlockSpec((tm, LANES), lambda i: (i % nt, 0))],
        out_specs=[row(w) for w in widths],
        out_shape=[jax.ShapeDtypeStruct((n, w), F32) for w in widths],
        compiler_params=_params("parallel"),
        name="inproj",
    )(h, g, w_all, w_gd, cos_t, sin_t)


def _blockdiag(x, bd_mask):
    xb = x.astype(BF16)
    return jnp.concatenate([xb] * BLOCK_HEADS, axis=0) * bd_mask


def _chunk_blocks(x):
    return [x[c * CHUNK:(c + 1) * CHUNK, b * BLOCK_W:(b + 1) * BLOCK_W]
            for c in range(x.shape[0] // CHUNK) for b in range(N_BLOCKS)]


def _fold_diag_blocks(full):
    out = full[0:HEAD_DIM]
    for h in range(1, BLOCK_HEADS):
        out = out + full[h * HEAD_DIM:(h + 1) * HEAD_DIM]
    return out


def _tri_inverse(a, bd_mask, same16, same32, eye):
    bd = lambda m: [_blockdiag(x, bd_mask) for x in m]
    mm = lambda xs, ys: [_dot(x, y) for x, y in zip(xs, ys)]
    n1 = [jnp.where(same16, x, 0.0) for x in a]
    n2 = mm(n1, bd(n1))
    n4 = mm(n2, bd(n2))
    n8 = mm(n4, bd(n4))
    t = [eye + x for x in n1]
    for p in (n2, n4, n8):
        t = [x + y for x, y in zip(t, mm(t, bd(p)))]
    for blk in (jnp.logical_and(same32, jnp.logical_not(same16)), jnp.logical_not(same32)):
        ab = bd([jnp.where(blk, x, 0.0) for x in a])
        t = [x + y for x, y in zip(t, mm(mm(t, ab), bd(t)))]
    return t


def _rwkv_kernel(z_ref, mu_ref, w0_ref, w2_ref, a0_ref, a2_ref, g2_ref, kk_ref, ka_ref, rk_ref,
                 lng_ref, lnb_ref, y_ref, carry_ref, s_ref):
    tt = z_ref.shape[0]

    @pl.when(pl.program_id(1) == 0)
    def _():
        carry_ref[...] = jnp.zeros_like(carry_ref)
        s_ref[...] = jnp.zeros_like(s_ref)

    z = z_ref[...]
    zprev = _shift_rows(z, carry_ref[...], 1)
    carry_ref[...] = z[tt - CARRY_ROWS:, :]
    z = z + mu_ref[...] * (zprev - z)
    r, k, v, lora = z[:, 0:WIDTH], z[:, WIDTH:2 * WIDTH], z[:, 2 * WIDTH:3 * WIDTH], z[:, 3 * WIDTH:]

    def lora_rows(w_ref, lo):
        w = w_ref[...].astype(BF16)
        parts = [jnp.zeros((lo, WIDTH), BF16)] if lo else []
        parts.append(w)
        if lo + w.shape[0] < RWKV_LORA:
            parts.append(jnp.zeros((RWKV_LORA - lo - w.shape[0], WIDTH), BF16))
        return jnp.concatenate(parts, axis=0)

    w2p = lora_rows(w2_ref, 0)
    a2p = lora_rows(a2_ref, w2_ref.shape[0])
    g2p = lora_rows(g2_ref, w2_ref.shape[0] + a2_ref.shape[0])
    logw = -_softplus(-(w0_ref[...] + _dot(jnp.tanh(lora), w2p))) - 0.5
    ld = -jnp.exp(logw)
    a = _sigmoid(a0_ref[...] + _dot(lora, a2p))
    g = _dot(_sigmoid(lora), g2p)

    hsum = _head_sum_matrix(WIDTH)
    kk = k * kk_ref[...]
    kk = kk * lax.rsqrt(_dot_sel2(kk * kk, hsum) + 1e-12)
    k = k * (1.0 + (a - 1.0) * ka_ref[...])
    sa, sb = -kk, kk * a

    cs = _sel_dot(_chunk_cumsum_matrix(tt), ld)
    tot = jnp.concatenate([jnp.broadcast_to(cs[c * CHUNK + CHUNK - 1:(c + 1) * CHUNK, :], (CHUNK, WIDTH))
                           for c in range(tt // CHUNK)], axis=0)
    g_inc, g_exc, g_inv, g_end = jnp.exp(cs), jnp.exp(cs - ld), jnp.exp(-cs), jnp.exp(tot - cs)
    g_tot = jnp.exp(tot)
    rt, at = r * g_inc, sa * g_exc
    bt, kt = sb * g_inv, k * g_inv
    bh, kh = sb * g_end, k * g_end

    n = CHUNK
    nc = tt // n
    bd_mask = _head_sum_matrix(BLOCK_W)
    bd_mask_f = bd_mask.astype(F32)
    ri, ci = _iota((n, BLOCK_W), 0), _iota((n, BLOCK_W), 1) & (n - 1)
    strict, incl, eye = ci < ri, ci <= ri, jnp.where(ci == ri, 1.0, 0.0)
    same16 = jnp.right_shift(ri, 4) == jnp.right_shift(ci, 4)
    same32 = jnp.right_shift(ri, 5) == jnp.right_shift(ci, 5)
    bd = lambda xs: [_blockdiag(x, bd_mask) for x in xs]
    at_c, rt_c, v_c = _chunk_blocks(at), _chunk_blocks(rt), _chunk_blocks(v)

    lhs = [jnp.concatenate([x, y], axis=0) for x, y in zip(at_c, rt_c)]
    rhs = [jnp.concatenate([x, y], axis=0) for x, y in zip(bd(_chunk_blocks(bt)), bd(_chunk_blocks(kt)))]
    gm = [_dot_nt(x, y) for x, y in zip(lhs, rhs)]
    a_ab = [jnp.where(strict, x[:n, :BLOCK_W], 0.0) for x in gm]
    a_rb = [jnp.where(incl, x[n:, :BLOCK_W], 0.0) for x in gm]
    a_ak = [jnp.where(strict, x[:n, BLOCK_W:], 0.0) for x in gm]
    a_rk = [jnp.where(incl, x[n:, BLOCK_W:], 0.0) for x in gm]
    v_bd = bd(v_c)
    av = [_dot(jnp.concatenate([x, y], axis=0), z) for x, y, z in zip(a_ak, a_rk, v_bd)]
    tinv = _tri_inverse(a_ab, bd_mask, same16, same32, eye)
    w12 = [_dot(t, jnp.concatenate([x, y], axis=1)) for t, x, y in zip(tinv, bd(at_c), bd([x[:n] for x in av]))]
    w1 = [x[:, :BLOCK_W] for x in w12]
    w2 = [x[:, BLOCK_W:] for x in w12]
    q12 = [_dot(x, jnp.concatenate([y, u], axis=1)) for x, y, u in zip(a_rb, bd(w1), bd(w2))]
    q1 = [x + y[:, :BLOCK_W] for x, y in zip(rt_c, q12)]
    q2 = [x[n:] + u[:, BLOCK_W:] for x, u in zip(av, q12)]
    bh_c, kh_c = _chunk_blocks(bh), _chunk_blocks(kh)
    p1 = [_dot_tn(x, y) * bd_mask_f for x, y in zip(w1, bh_c)]
    p2 = [_fold_diag_blocks(_dot_tn(jnp.concatenate([x, u], axis=0), jnp.concatenate([y, w], axis=0)) * bd_mask_f)
          for x, u, y, w in zip(w2, v_c, bh_c, kh_c)]

    s = [s_ref[:, b * BLOCK_W:(b + 1) * BLOCK_W] for b in range(N_BLOCKS)]
    y_rows = []
    for c in range(nc):
        y_blocks = []
        for b in range(N_BLOCKS):
            i = c * N_BLOCKS + b
            y_blocks.append(_dot_nt(q1[i], _blockdiag(s[b], bd_mask)) + q2[i])
            s[b] = s[b] * g_tot[c * n:c * n + 1, b * BLOCK_W:(b + 1) * BLOCK_W] + _dot(s[b], p1[i]) + p2[i]
        y_rows.append(jnp.concatenate(y_blocks, axis=1))
    for b in range(N_BLOCKS):
        s_ref[:, b * BLOCK_W:(b + 1) * BLOCK_W] = s[b]
    y = jnp.concatenate(y_rows, axis=0)

    mean = _dot_sel2(y, hsum) * (1.0 / HEAD_DIM)
    yc = y - mean
    var = _dot_sel2(yc * yc, hsum) * (1.0 / HEAD_DIM)
    yn = yc * lax.rsqrt(var + RWKV_GN_EPS) * lng_ref[...] + lnb_ref[...]
    bonus = _dot_sel2(r * k * rk_ref[...], hsum) * v
    y_ref[...] = (yn + bonus) * g


def _rwkv(z, p, layer):
    b, t, _ = z.shape
    tt = RWKV_TILE
    args = (z, p["mu"], p["w0"], p["w2"], p["a0"], p["a2"], p["g2"], p["k_k"], p["k_a"], p["r_k"],
            p["ln_g"], p["ln_b"])
    in_specs = [pl.BlockSpec((None, tt, RWKV_IN), lambda i, j: (i, j, 0))]
    in_specs += [_layer_spec(a, layer) for a in args[1:]]
    return pl.pallas_call(
        _rwkv_kernel,
        grid=(b, t // tt),
        in_specs=in_specs,
        out_specs=pl.BlockSpec((None, tt, WIDTH), lambda i, j: (i, j, 0)),
        out_shape=jax.ShapeDtypeStruct((b, t, WIDTH), F32),
        scratch_shapes=[pltpu.VMEM((CARRY_ROWS, RWKV_IN), F32),
                        pltpu.VMEM((HEAD_DIM, WIDTH), F32)],
        compiler_params=_params("parallel", "arbitrary"),
        name="rwkv7",
    )(*args)


ATTN_BLOCK = 128
ATTN_UNITS = 16


def _attn_group(q_ref, k_ref, v_ref, og_ref, lg_ref, dil, first):
    t = q_ref.shape[0]
    nblk = t // dil // ATTN_BLOCK
    has_prev = nblk > 1
    qi, kj = _iota((ATTN_BLOCK, ATTN_BLOCK), 0), _iota((ATTN_BLOCK, ATTN_BLOCK), 1)
    cur_ok, prev_ok = kj <= qi, kj >= qi
    second = jnp.logical_not(first)

    def step(i, carry):
        rows, prevs, pmasks = [], [], []
        for x in range(ATTN_UNITS):
            u = i * ATTN_UNITS + x
            r, nb = u // nblk, u % nblk
            start = r + nb * (ATTN_BLOCK * dil)
            rows.append(pl.ds(start, ATTN_BLOCK, stride=dil))
            prevs.append(pl.ds(jnp.maximum(start - ATTN_BLOCK * dil, r), ATTN_BLOCK, stride=dil))
            pmasks.append(jnp.logical_and(prev_ok, nb > 0))
        q2 = [q_ref[rw, :] for rw in rows]
        kc = [k_ref[rw, :].astype(BF16) for rw in rows]
        vc = [v_ref[rw, :].astype(BF16) for rw in rows]
        if has_prev:
            kp = [k_ref[rw, :].astype(BF16) for rw in prevs]
            vc = [jnp.concatenate([v_ref[pw, :].astype(BF16), x], axis=0) for pw, x in zip(prevs, vc)]
        heads = []
        for sel in (first, second):
            qj = [jnp.where(sel, x, 0.0) for x in q2]
            s = [jnp.where(cur_ok, _dot_nt(x, y), NEG_INF) for x, y in zip(qj, kc)]
            if has_prev:
                sp = [jnp.where(pm, _dot_nt(x, y), NEG_INF) for pm, x, y in zip(pmasks, qj, kp)]
                s = [jnp.concatenate([x, y], axis=1) for x, y in zip(sp, s)]
            m = [jnp.max(x, axis=-1, keepdims=True) for x in s]
            e = [jnp.exp(x - y) for x, y in zip(s, m)]
            l = [jnp.sum(x, axis=-1, keepdims=True) for x in e]
            o = [_dot(x, y) / z for x, y, z in zip(e, vc, l)]
            lse = [jnp.broadcast_to(x + jnp.log(y), (ATTN_BLOCK, ATTN_OUT)) for x, y in zip(m, l)]
            heads.append((o, lse))
        for x in range(ATTN_UNITS):
            og_ref[rows[x], :] = jnp.where(first, heads[0][0][x], heads[1][0][x])
            lg_ref[rows[x], :] = jnp.where(first, heads[0][1][x], heads[1][1][x])
        return carry

    lax.fori_loop(0, t // ATTN_BLOCK // ATTN_UNITS, step, 0)


def _attn_kernel(q0, k0, v0, q1, k1, v1, q2, k2, v2, o_ref, og0, og1, og2, lg0, lg1, lg2):
    first = _iota((ATTN_BLOCK, ATTN_OUT), 1) < HEAD_DIM
    groups = ((q0, k0, v0, og0, lg0), (q1, k1, v1, og1, lg1), (q2, k2, v2, og2, lg2))
    for (win, dil), refs in zip(ATTN_GROUPS, groups):
        _attn_group(*refs, dil, first)
    l0, l1, l2 = lg0[...], lg1[...], lg2[...]
    mx = jnp.maximum(jnp.maximum(l0, l1), l2)
    w0, w1, w2 = jnp.exp(l0 - mx), jnp.exp(l1 - mx), jnp.exp(l2 - mx)
    o_ref[...] = (w0 * og0[...] + w1 * og1[...] + w2 * og2[...]) / (w0 + w1 + w2)


def _attention(zb):
    b, t, _ = zb.shape
    for win, dil in ATTN_GROUPS:
        assert win // dil == ATTN_BLOCK and t % (dil * ATTN_BLOCK) == 0 and t % (ATTN_BLOCK * ATTN_UNITS) == 0
    col = lambda blk: pl.BlockSpec((None, t, ATTN_OUT), lambda i, blk=blk: (i, 0, blk))
    ng = len(ATTN_GROUPS)
    in_specs = []
    for g in range(ng):
        in_specs += [col(g), col(ng + g), col(2 * ng + g)]
    return pl.pallas_call(
        _attn_kernel,
        grid=(b,),
        in_specs=in_specs,
        out_specs=pl.BlockSpec((None, t, ATTN_OUT), lambda i: (i, 0, 0)),
        out_shape=jax.ShapeDtypeStruct((b, t, ATTN_OUT), F32),
        scratch_shapes=[pltpu.VMEM((t, ATTN_OUT), F32)] * (2 * ng),
        compiler_params=_params("parallel"),
        name="dilated_attention",
    )(*([zb] * (3 * ng)))


def _mlstm_kernel(z_ref, cw_ref, cb_ref, ib_ref, fb_ref, y_ref, carry_ref, c_ref, m_ref):
    tt = z_ref.shape[0]

    @pl.when(pl.program_id(1) == 0)
    def _():
        carry_ref[...] = jnp.zeros_like(carry_ref)
        c_ref[...] = jnp.zeros_like(c_ref)
        m_ref[...] = jnp.zeros_like(m_ref)

    qk_in = z_ref[:, 0:2 * WIDTH]
    ext = jnp.concatenate([carry_ref[...], qk_in], axis=0)
    acc = cb_ref[...] + cw_ref[MLSTM_CONV - 1:MLSTM_CONV, :] * qk_in
    for j in range(1, MLSTM_CONV):
        acc = acc + cw_ref[MLSTM_CONV - 1 - j:MLSTM_CONV - j, :] * pltpu.roll(ext, j, axis=0)[CARRY_ROWS:]
    carry_ref[...] = qk_in[tt - CARRY_ROWS:, :]
    qk = _silu(acc)
    q, k = qk[:, :WIDTH], qk[:, WIDTH:] * (HEAD_DIM ** -0.5)
    v = z_ref[:, 2 * WIDTH:3 * WIDTH]
    og = _sigmoid(z_ref[:, 3 * WIDTH:4 * WIDTH])

    n = CHUNK
    nc = tt // n
    gates = z_ref[:, 4 * WIDTH:]
    gi = gates + ib_ref[...]
    lf = -_softplus(-(pltpu.roll(gates, LANES - GATE_LANE, axis=1) + fb_ref[...]))
    bcs = _sel_dot(_chunk_cumsum_matrix(tt), lf)
    u = gi - bcs
    row_in_chunk = _iota((tt, LANES), 0) & (n - 1)
    cmax = u
    for sh in (1, 2, 4, 8, 16, 32):
        cmax = jnp.where(row_in_chunk >= sh, jnp.maximum(cmax, pltpu.roll(cmax, sh, axis=0)), cmax)
    m_prev = m_ref[0:1, :]
    m_rows, scal_rows = [], []
    for c in range(nc):
        last = c * n + n - 1
        b_end = bcs[last:last + 1, :]
        m_new = b_end + jnp.maximum(m_prev, cmax[last:last + 1, :])
        m_rows.append(bcs[c * n:(c + 1) * n, :] + jnp.maximum(m_prev, cmax[c * n:(c + 1) * n, :]))
        scal_rows.append(jnp.concatenate([m_prev, b_end - m_new, jnp.exp(b_end + m_prev - m_new),
                                          jnp.zeros((SUBLANES - 3, LANES), F32)], axis=0))
        m_prev = m_new
    m_ref[0:1, :] = m_prev
    m_t = jnp.concatenate(m_rows, axis=0)

    expand = _onehot(_iota((LANES, WIDTH), 0) == jnp.right_shift(_iota((LANES, WIDTH), 1), HEAD_SHIFT))
    bt_x, mt_x, uc_x = _dot_sel(bcs, expand), _dot_sel(m_t, expand), _dot_sel(u, expand)
    scal_x = _dot_sel(jnp.concatenate(scal_rows, axis=0), expand)
    u_t = u.T

    bd_mask = _head_sum_matrix(BLOCK_W)
    bd_mask2 = jnp.concatenate([bd_mask, bd_mask], axis=1)
    bd_mask2_f = bd_mask2.astype(F32)
    ones = jnp.ones((n, BLOCK_W), BF16)
    ri, ci = _iota((n, BLOCK_W), 0), _iota((n, BLOCK_W), 1) & (n - 1)
    causal = ci <= ri
    q_c, k_c, v_c = _chunk_blocks(q), _chunk_blocks(k), _chunk_blocks(v)
    bt_c, mt_c, uc_c = _chunk_blocks(bt_x), _chunk_blocks(mt_x), _chunk_blocks(uc_x)
    scal = lambda c, b, row: scal_x[SUBLANES * c + row:SUBLANES * c + row + 1, b * BLOCK_W:(b + 1) * BLOCK_W]

    scores, upds = [], []
    for c in range(nc):
        for b in range(N_BLOCKS):
            i = c * N_BLOCKS + b
            u_row = jnp.concatenate([u_t[h:h + 1, c * n:(c + 1) * n]
                                     for h in range(b * BLOCK_HEADS, (b + 1) * BLOCK_HEADS)], axis=1)
            d_intra = jnp.where(causal, bt_c[i] + u_row, -jnp.inf)
            scores.append(_dot_nt(q_c[i], _blockdiag(k_c[i], bd_mask)) * jnp.exp(d_intra - mt_c[i]))
            w_k = jnp.exp(uc_c[i] + scal(c, b, 1))
            upds.append(_dot_tn(k_c[i] * w_k, jnp.concatenate([v_c[i].astype(BF16), ones], axis=1)) * bd_mask2_f)
    intra = [_dot(s, jnp.concatenate([_blockdiag(x, bd_mask), bd_mask], axis=1)) for s, x in zip(scores, v_c)]

    state = [c_ref[b] for b in range(N_BLOCKS)]
    y_rows = []
    for c in range(nc):
        y_blocks = []
        for b in range(N_BLOCKS):
            i = c * N_BLOCKS + b
            inter = _dot(q_c[i], state[b])
            w_inter = jnp.exp(bt_c[i] + scal(c, b, 0) - mt_c[i])
            num = w_inter * inter[:, :BLOCK_W] + intra[i][:, :BLOCK_W]
            den = w_inter * inter[:, BLOCK_W:] + intra[i][:, BLOCK_W:]
            y_blocks.append(num / jnp.maximum(jnp.abs(den), jnp.exp(-mt_c[i])))
            dec = scal(c, b, 2)
            state[b] = jnp.concatenate([dec, dec], axis=1) * state[b] + upds[i]
        y_rows.append(jnp.concatenate(y_blocks, axis=1))
    for b in range(N_BLOCKS):
        c_ref[b] = state[b]
    y_ref[...] = og * jnp.concatenate(y_rows, axis=0)


def _mlstm(z, p, layer):
    b, t, _ = z.shape
    tt = SEQ_TILE
    args = (z, p["conv_w"], p["conv_b"], p["i_b"], p["f_b"])
    in_specs = [pl.BlockSpec((None, tt, MLSTM_IN_PAD), lambda i, j: (i, j, 0))]
    in_specs += [_layer_spec(a, layer) for a in args[1:]]
    return pl.pallas_call(
        _mlstm_kernel,
        grid=(b, t // tt),
        in_specs=in_specs,
        out_specs=pl.BlockSpec((None, tt, WIDTH), lambda i, j: (i, j, 0)),
        out_shape=jax.ShapeDtypeStruct((b, t, WIDTH), F32),
        scratch_shapes=[pltpu.VMEM((CARRY_ROWS, 2 * WIDTH), F32),
                        pltpu.VMEM((N_BLOCKS, BLOCK_W, 2 * BLOCK_W), F32),
                        pltpu.VMEM((SUBLANES, LANES), F32)],
        compiler_params=_params("parallel", "arbitrary"),
        name="mlstm",
    )(*args)


def _hgrn_kernel(z_ref, lbl_ref, ng_ref, y_ref, s_ref, sel_ref, *, layer):
    tt = z_ref.shape[0]

    @pl.when(pl.program_id(1) == 0)
    def _():
        s_ref[...] = jnp.zeros_like(s_ref)

    logits = lbl_ref[...]
    pe = jnp.exp(logits - jnp.max(logits, axis=0, keepdims=True))
    pr = pe / jnp.sum(pe, axis=0, keepdims=True)
    lb = pr[0:1, :]
    for i in range(1, layer + 1):
        lb = lb + pr[i:i + 1, :]
    lb = lb - pr[0:1, :]

    q = _silu(z_ref[:, 0:WIDTH])
    f = lb + (1.0 - lb) * _sigmoid(z_ref[:, WIDTH:2 * WIDTH])
    k = 1.0 - f
    v = z_ref[:, 2 * WIDTH:3 * WIDTH]
    og = _sigmoid(z_ref[:, 3 * WIDTH:4 * WIDTH])
    gl = jnp.log(f)

    @pl.when(pl.program_id(1) == 0)
    def _():
        r, c = _iota((tt, tt), 0), _iota((tt, tt), 1)
        sel_ref[0] = _chunk_cumsum_matrix(tt)
        for lv in range(1, 7):
            sel_ref[lv] = _onehot(c == jnp.right_shift(r, lv) * (1 << lv) + (1 << (lv - 1)) - 1)

    bsum = _sel_dot(sel_ref[0], gl)
    b_end = jnp.concatenate([jnp.broadcast_to(bsum[c * CHUNK + CHUNK - 1:(c + 1) * CHUNK, :], (CHUNK, WIDTH))
                             for c in range(tt // CHUNK)], axis=0)
    q_in = q * jnp.exp(bsum)
    k_end = k * jnp.exp(b_end - bsum)
    g_tot = jnp.exp(b_end)

    n = CHUNK
    nc = tt // n
    hsum = _head_sum_matrix(WIDTH)
    bd_mask = _head_sum_matrix(BLOCK_W)
    bd_mask_f = bd_mask.astype(F32)
    ri, ci = _iota((n, BLOCK_W), 0), _iota((n, BLOCK_W), 1) & (n - 1)
    row_t = _iota((tt, 1), 0)
    bd = lambda xs: [_blockdiag(x, bd_mask) for x in xs]
    b_mids = jnp.dot(sel_ref[1:7].reshape(6 * tt, tt), bsum.astype(BF16), preferred_element_type=F32)

    def level_operands(lv):
        size, half = 1 << lv, 1 << (lv - 1)
        b_mid = b_mids[(lv - 1) * tt:lv * tt]
        right = (row_t & (size - 1)) >= half
        e = jnp.exp(jnp.where(right, bsum - b_mid, b_mid - bsum))
        return _chunk_blocks(jnp.where(right, q * e, 0.0)), bd(_chunk_blocks(jnp.where(right, 0.0, k * e)))

    nxt = level_operands(1)
    attn = [jnp.where(ri == ci, _dot_nt(x, y), 0.0) for x, y in zip(_chunk_blocks(q), bd(_chunk_blocks(k)))]
    for lv in range(1, 7):
        q_l, k_l = nxt
        if lv < 6:
            nxt = level_operands(lv + 1)
        same = jnp.right_shift(ri, lv) == jnp.right_shift(ci, lv)
        attn = [a + jnp.where(same, _dot_nt(x, y), 0.0) for a, x, y in zip(attn, q_l, k_l)]
    v_c = _chunk_blocks(v)
    intra = [_dot(a, y) for a, y in zip(attn, bd(v_c))]
    upd = [_fold_diag_blocks(_dot_tn(x, y) * bd_mask_f) for x, y in zip(v_c, _chunk_blocks(k_end))]

    s = [s_ref[:, b * BLOCK_W:(b + 1) * BLOCK_W] for b in range(N_BLOCKS)]
    y_rows = []
    q_in_c = _chunk_blocks(q_in)
    for c in range(nc):
        y_blocks = []
        for b in range(N_BLOCKS):
            i = c * N_BLOCKS + b
            y_blocks.append(_dot_nt(q_in_c[i], _blockdiag(s[b], bd_mask)) + intra[i])
            s[b] = s[b] * g_tot[c * n:c * n + 1, b * BLOCK_W:(b + 1) * BLOCK_W] + upd[i]
        y_rows.append(jnp.concatenate(y_blocks, axis=1))
    for b in range(N_BLOCKS):
        s_ref[:, b * BLOCK_W:(b + 1) * BLOCK_W] = s[b]
    o = jnp.concatenate(y_rows, axis=0)
    ms = _dot_sel2(o * o, hsum) * (1.0 / HEAD_DIM)
    y_ref[...] = o * lax.rsqrt(ms + NORM_EPS) * ng_ref[...] * og


def _hgrn(z, lb_logits, norm_g, layer):
    b, t, _ = z.shape
    tt = SEQ_TILE
    return pl.pallas_call(
        functools.partial(_hgrn_kernel, layer=layer),
        grid=(b, t // tt),
        in_specs=[pl.BlockSpec((None, tt, HGRN_IN), lambda i, j: (i, j, 0)),
                  _const_spec(lb_logits.shape), _layer_spec(norm_g, layer)],
        out_specs=pl.BlockSpec((None, tt, WIDTH), lambda i, j: (i, j, 0)),
        out_shape=jax.ShapeDtypeStruct((b, t, WIDTH), F32),
        scratch_shapes=[pltpu.VMEM((HEAD_DIM, WIDTH), F32), pltpu.VMEM((7, tt, tt), BF16)],
        compiler_params=_params("parallel", "arbitrary"),
        name="hgrn2",
    )(z, lb_logits, norm_g)


def _merge_kernel(h_ref, g_ref, wg_ref, bg_ref, ya_ref, yb_ref, yc_ref, yd_ref,
                  pa_ref, pb_ref, pc_ref, pd_ref, wo_ref, o_ref):
    h = h_ref[...]
    d = h.shape[1]
    xb = _rms(h, g_ref[...]).astype(BF16)
    merged = None
    for i, (y_ref, p_ref) in enumerate(((ya_ref, pa_ref), (yb_ref, pb_ref), (yc_ref, pc_ref), (yd_ref, pd_ref))):
        gate = _sigmoid(_dot_nt(xb, wg_ref[i * d:(i + 1) * d, :]) + bg_ref[:, i * d:(i + 1) * d])
        term = gate * jnp.dot(y_ref[...].astype(BF16), p_ref[...], preferred_element_type=F32)
        merged = term if merged is None else merged + term
    o_ref[...] = h + jnp.dot(merged.astype(BF16), wo_ref[...], preferred_element_type=F32)


def _merge(h, g, wg, bg, ys, ps, wo, layer):
    n, d = h.shape
    tm = TOKEN_TILE
    row = lambda w: pl.BlockSpec((tm, w), lambda i: (i, 0))
    in_specs = [row(d), _layer_spec(g, layer), _layer_spec(wg, layer, True), _layer_spec(bg, layer)]
    in_specs += [row(y.shape[1]) for y in ys]
    in_specs += [_layer_spec(p, layer, True) for p in ps]
    in_specs += [_layer_spec(wo, layer, True)]
    return pl.pallas_call(
        _merge_kernel,
        grid=(n // tm,),
        in_specs=in_specs,
        out_specs=row(d),
        out_shape=jax.ShapeDtypeStruct((n, d), F32),
        compiler_params=_params("parallel"),
        name="merge",
    )(h, g, wg, bg, *ys, *ps, wo)


def _ffn_kernel(h_ref, g_ref, wu_ref, cw_ref, cb_ref, wd_ref, gf_ref, o_ref, carry_ref, act_ref, *, final):
    tm = h_ref.shape[0]
    dff = wd_ref.shape[0]

    @pl.when(pl.program_id(1) == 0)
    def _():
        carry_ref[...] = jnp.zeros_like(carry_ref)

    h = h_ref[...]
    xb = _rms(h, g_ref[...]).astype(BF16)

    def up_cols(lo):
        return jnp.dot(xb, wu_ref[:, lo:lo + FF_TILE], preferred_element_type=F32)

    def conv_cols(up, lo):
        ext = jnp.concatenate([carry_ref[:, lo:lo + FF_TILE], up], axis=0)
        carry_ref[:, lo:lo + FF_TILE] = up[tm - CARRY_ROWS:, :]
        out = cb_ref[:, lo:lo + FF_TILE] + cw_ref[FFN_CONV - 1:FFN_CONV, lo:lo + FF_TILE] * up
        for j in range(1, FFN_CONV):
            out = out + (cw_ref[FFN_CONV - 1 - j:FFN_CONV - j, lo:lo + FF_TILE]
                         * pltpu.roll(ext, j, axis=0)[CARRY_ROWS:])
        return out

    nslices = dff // FF_TILE
    half = (nslices // 2) * FF_TILE
    nxt = (up_cols(dff), up_cols(0))
    for ci in range(nslices):
        lo = ci * FF_TILE
        up_g, up_u = nxt
        if ci + 1 < nslices:
            nxt = (up_cols(dff + lo + FF_TILE), up_cols(lo + FF_TILE))
        act_ref[:, lo:lo + FF_TILE] = (_silu(conv_cols(up_g, dff + lo)) * conv_cols(up_u, lo)).astype(BF16)
        if lo + FF_TILE == half:
            out = h + jnp.dot(act_ref[:, :half], wd_ref[:half, :], preferred_element_type=F32)
    out = out + jnp.dot(act_ref[:, half:], wd_ref[half:, :], preferred_element_type=F32)
    o_ref[...] = _rms(out, gf_ref[...]) if final else out


def _ffn(h, g, wu, cw, cb, wd, gf, layer, final):
    b, t, d = h.shape
    tm = TOKEN_TILE
    blk = pl.BlockSpec((None, tm, d), lambda i, j: (i, j, 0))
    return pl.pallas_call(
        functools.partial(_ffn_kernel, final=final),
        grid=(b, t // tm),
        in_specs=[blk, _layer_spec(g, layer), _layer_spec(wu, layer, True), _layer_spec(cw, layer),
                  _layer_spec(cb, layer), _layer_spec(wd, layer, True), _const_spec((1, d))],
        out_specs=blk,
        out_shape=jax.ShapeDtypeStruct((b, t, d), F32),
        scratch_shapes=[pltpu.VMEM((CARRY_ROWS, wu.shape[-1]), F32), pltpu.VMEM((tm, wd.shape[-2]), BF16)],
        compiler_params=_params("parallel", "arbitrary"),
        name="convffn",
    )(h, g, wu, cw, cb, wd, gf)


def _rope_tables(seq):
    half = HEAD_DIM // 2
    inv_freq = ROPE_THETA ** (-jnp.arange(half, dtype=F32) / half)
    ang = jnp.arange(seq).astype(F32)[:, None] * inv_freq[None, :]
    cos, sin = jnp.cos(ang), jnp.sin(ang)
    return jnp.concatenate([cos, cos, cos, cos], axis=1), jnp.concatenate([-sin, sin, -sin, sin], axis=1)


def _split_w_in(w):
    nl, d, _ = w.shape
    o_c = RWKV_IN + 3 * ATTN_WIDTH
    o_g = o_c + 4 * WIDTH
    o_d = o_g + 2 * N_HEADS
    o_m = o_d + HGRN_IN
    wt = jnp.swapaxes(w, 1, 2).astype(BF16)
    pad = jnp.zeros((nl, GATE_LANE - N_HEADS, d), BF16)
    w_gd = jnp.concatenate([wt[:, o_g:o_g + N_HEADS], pad, wt[:, o_g + N_HEADS:o_d], pad, wt[:, o_d:o_m]], axis=1)
    return wt, o_g, w_gd, wt[:, o_m:]


def _lane_pad(a):
    return jnp.pad(a, ((0, 0), (0, LANES - a.shape[1]))).reshape(a.shape[0], 1, LANES)


def kernel(x, norm_mix_g, w_in, b_gate, rwkv_mu, rwkv_w0, rwkv_w2, rwkv_a0, rwkv_a2, rwkv_g2, rwkv_k_k, rwkv_k_a, rwkv_r_k, rwkv_ln_g, rwkv_ln_b, mlstm_conv_w, mlstm_conv_b, mlstm_i_b, mlstm_f_b, hgrn_lb_logits, hgrn_norm_g, p_rwkv, p_attn, p_mlstm, p_hgrn, w_out, norm_ffn_g, w_up, ffn_conv_w, ffn_conv_b, w_down, final_norm_g):
    bsz, seq, d = x.shape
    depth = w_in.shape[0]
    n = bsz * seq
    cos_t, sin_t = _rope_tables(seq)
    w_all, n_abc, w_gd, wg = _split_w_in(w_in)
    norm_mix, norm_ffn = _rows3(norm_mix_g), _rows3(norm_ffn_g)
    rwkv_p = dict(mu=_rows3(rwkv_mu), w0=_rows3(rwkv_w0), w2=rwkv_w2, a0=_rows3(rwkv_a0), a2=rwkv_a2,
                  g2=rwkv_g2, k_k=_rows3(rwkv_k_k), k_a=_rows3(rwkv_k_a), r_k=_rows3(rwkv_r_k),
                  ln_g=_rows3(rwkv_ln_g), ln_b=_rows3(rwkv_ln_b))
    mlstm_p = dict(conv_w=mlstm_conv_w, conv_b=_rows3(mlstm_conv_b), i_b=_lane_pad(mlstm_i_b),
                   f_b=_lane_pad(mlstm_f_b))
    hgrn_g = _rows3(hgrn_norm_g)
    bg = _rows3(b_gate)
    ps = [p.astype(BF16) for p in (p_rwkv, p_attn, p_mlstm, p_hgrn)]
    wo, wu, wd = w_out.astype(BF16), w_up.astype(BF16), w_down.astype(BF16)
    ffn_cb = _rows3(ffn_conv_b)
    gf = final_norm_g.reshape(1, d)
    h = x.reshape(n, d)
    for l in range(depth):
        za, zb, zc, zd = _inproj(h, norm_mix, w_all, n_abc, w_gd, cos_t, sin_t, seq, l)
        ya = _rwkv(za.reshape(bsz, seq, -1), rwkv_p, l)
        yb = _attention(zb.reshape(bsz, seq, -1))
        yc = _mlstm(zc.reshape(bsz, seq, -1), mlstm_p, l)
        yd = _hgrn(zd.reshape(bsz, seq, -1), hgrn_lb_logits, hgrn_g, l)
        ys = [y.reshape(n, -1) for y in (ya, yb, yc, yd)]
        h = _merge(h, norm_mix, wg, bg, ys, ps, wo, l)
        h = _ffn(h.reshape(bsz, seq, d), norm_ffn, wu, ffn_conv_w, ffn_cb, wd, gf, l, l == depth - 1).reshape(n, d)
    return h.reshape(bsz, seq, d)
```

```python
import functools

import jax
import jax.numpy as jnp
from jax import lax
from jax.experimental import pallas as pl
from jax.experimental.pallas import tpu as pltpu

F32 = jnp.float32
BF16 = jnp.bfloat16

LANES = 128
SUBLANES = 8
HEAD_DIM = 64
HEAD_SHIFT = 6
N_HEADS = 4
WIDTH = N_HEADS * HEAD_DIM
CHUNK = 64
BLOCK_HEADS = 2
BLOCK_W = BLOCK_HEADS * HEAD_DIM
N_BLOCKS = N_HEADS // BLOCK_HEADS
RWKV_LORA = 32 + 32 + 64
RWKV_IN = 3 * WIDTH + RWKV_LORA
RWKV_GN_EPS = 64e-5
ATTN_GROUPS = ((128, 1), (512, 4), (2048, 16))
ATTN_HEADS = 6
ATTN_WIDTH = ATTN_HEADS * HEAD_DIM
ATTN_OUT = 2 * HEAD_DIM
ROPE_THETA = 10000.0
MLSTM_CONV = 4
MLSTM_IN_PAD = 4 * WIDTH + LANES
GATE_LANE = 64
HGRN_IN = 4 * WIDTH
HGRN_PREP = 5 * WIDTH
FFN_CONV = 3
NORM_EPS = 1e-6
NEG_INF = -1e30

TOKEN_TILE = 512
SEQ_TILE = 256
RWKV_TILE = 512
FF_TILE = 256
CARRY_ROWS = SUBLANES
VMEM_LIMIT = 56 * 1024 * 1024


def _dot(a, b):
    return jnp.dot(a.astype(BF16), b.astype(BF16), preferred_element_type=F32)


def _dot_nt(a, b):
    return lax.dot_general(a.astype(BF16), b.astype(BF16), (((1,), (1,)), ((), ())),
                           preferred_element_type=F32)


def _dot_tn(a, b):
    return lax.dot_general(a.astype(BF16), b.astype(BF16), (((0,), (0,)), ((), ())),
                           preferred_element_type=F32)


def _split3(x):
    hi = x.astype(BF16)
    r1 = x - hi.astype(F32)
    mid = r1.astype(BF16)
    lo = (r1 - mid.astype(F32)).astype(BF16)
    return hi, mid, lo


def _sel_dot(m01, x):
    hi, mid, lo = _split3(x)
    return (jnp.dot(m01, hi, preferred_element_type=F32)
            + jnp.dot(m01, mid, preferred_element_type=F32)
            + jnp.dot(m01, lo, preferred_element_type=F32))


def _dot_sel(x, m01):
    hi, mid, lo = _split3(x)
    return (jnp.dot(hi, m01, preferred_element_type=F32)
            + jnp.dot(mid, m01, preferred_element_type=F32)
            + jnp.dot(lo, m01, preferred_element_type=F32))


def _dot_sel2(x, m01):
    hi = x.astype(BF16)
    lo = (x - hi.astype(F32)).astype(BF16)
    return jnp.dot(hi, m01, preferred_element_type=F32) + jnp.dot(lo, m01, preferred_element_type=F32)


def _iota(shape, dim):
    return lax.broadcasted_iota(jnp.int32, shape, dim)


def _onehot(cond):
    return jnp.where(cond, 1.0, 0.0).astype(BF16)


def _chunk_cumsum_matrix(n):
    r, c = _iota((n, n), 0), _iota((n, n), 1)
    return _onehot((jnp.right_shift(r, HEAD_SHIFT) == jnp.right_shift(c, HEAD_SHIFT)) & (c <= r))


def _head_sum_matrix(n):
    r, c = _iota((n, n), 0), _iota((n, n), 1)
    return _onehot(jnp.right_shift(r, HEAD_SHIFT) == jnp.right_shift(c, HEAD_SHIFT))


def _shift_rows(x, carry, j):
    ext = jnp.concatenate([carry, x], axis=0)
    return pltpu.roll(ext, j, axis=0)[CARRY_ROWS:]


def _softplus(x):
    return jnp.maximum(x, 0.0) + jnp.log1p(jnp.exp(-jnp.abs(x)))


def _sigmoid(x):
    return jax.nn.sigmoid(x)


def _silu(x):
    return x * _sigmoid(x)


def _rms(x, g):
    return x * lax.rsqrt(jnp.mean(x * x, axis=-1, keepdims=True) + NORM_EPS) * g


def _const_spec(shape, single_buffer=False):
    nd = len(shape)
    if single_buffer:
        return pl.BlockSpec(shape, lambda *_: (0,) * nd, pipeline_mode=pl.Buffered(1))
    return pl.BlockSpec(shape, lambda *_: (0,) * nd)


def _layer_spec(a, layer, single_buffer=False):
    shape = (None,) + a.shape[1:]
    nd = a.ndim
    idx = lambda *_: (layer,) + (0,) * (nd - 1)
    if single_buffer:
        return pl.BlockSpec(shape, idx, pipeline_mode=pl.Buffered(1))
    return pl.BlockSpec(shape, idx)


def _rows3(a):
    return a.reshape(a.shape[0], 1, -1)


def _params(*sem):
    return pltpu.CompilerParams(dimension_semantics=sem, vmem_limit_bytes=VMEM_LIMIT)


def _inproj_kernel(x_ref, g_ref, w_ref, wgd_ref, cos_ref, sin_ref, lbl_ref, za_ref, zb_ref, zc_ref, zd_ref, *, layer):
    xb = _rms(x_ref[...], g_ref[...]).astype(BF16)

    def mm(lo, width):
        return _dot_nt(xb, w_ref[lo:lo + width, :])

    za_ref[...] = mm(0, RWKV_IN)
    o = RWKV_IN
    qk = mm(o, 2 * ATTN_WIDTH)
    w = 2 * ATTN_WIDTH
    low = (_iota(qk.shape, 1) & (HEAD_DIM - 1)) < HEAD_DIM // 2
    rot = jnp.where(low, pltpu.roll(qk, w - HEAD_DIM // 2, axis=1), pltpu.roll(qk, HEAD_DIM // 2, axis=1))
    cos = jnp.concatenate([cos_ref[...]] * (w // LANES), axis=1)
    sin = jnp.concatenate([sin_ref[...]] * (w // LANES), axis=1)
    roped = qk * cos + rot * sin
    zb_ref[:, 0:ATTN_WIDTH] = roped[:, 0:ATTN_WIDTH] * (HEAD_DIM ** -0.5)
    zb_ref[:, ATTN_WIDTH:w] = roped[:, ATTN_WIDTH:]
    zb_ref[:, w:] = mm(o + w, ATTN_WIDTH)
    o += 3 * ATTN_WIDTH
    zc_ref[:, 0:4 * WIDTH] = mm(o, 4 * WIDTH)
    gd = _dot_nt(xb, wgd_ref[...])
    zc_ref[:, 4 * WIDTH:] = gd[:, 0:LANES]
    logits = lbl_ref[...]
    pe = jnp.exp(logits - jnp.max(logits, axis=0, keepdims=True))
    pr = pe / jnp.sum(pe, axis=0, keepdims=True)
    lb = pr[0:1, :]
    for i in range(1, layer + 1):
        lb = lb + pr[i:i + 1, :]
    lb = lb - pr[0:1, :]
    zd = gd[:, LANES:]
    f = lb + (1.0 - lb) * _sigmoid(zd[:, WIDTH:2 * WIDTH])
    zd_ref[:, 0:WIDTH] = _silu(zd[:, 0:WIDTH])
    zd_ref[:, WIDTH:2 * WIDTH] = 1.0 - f
    zd_ref[:, 2 * WIDTH:3 * WIDTH] = zd[:, 2 * WIDTH:3 * WIDTH]
    zd_ref[:, 3 * WIDTH:4 * WIDTH] = _sigmoid(zd[:, 3 * WIDTH:4 * WIDTH])
    zd_ref[:, 4 * WIDTH:] = jnp.log(f)


def _inproj(h, g, w_all, n_abc, w_gd, cos_t, sin_t, lb_logits, seq, layer):
    n, d = h.shape
    tm = TOKEN_TILE
    nt = seq // tm
    widths = (RWKV_IN, 3 * ATTN_WIDTH, MLSTM_IN_PAD, HGRN_PREP)
    row = lambda w: pl.BlockSpec((tm, w), lambda i: (i, 0))
    return pl.pallas_call(
        functools.partial(_inproj_kernel, layer=layer),
        grid=(n // tm,),
        in_specs=[row(d), _layer_spec(g, layer),
                  pl.BlockSpec((None, n_abc, d), lambda i: (layer, 0, 0), pipeline_mode=pl.Buffered(1)),
                  _layer_spec(w_gd, layer, True),
                  pl.BlockSpec((tm, LANES), lambda i: (i % nt, 0)),
                  pl.BlockSpec((tm, LANES), lambda i: (i % nt, 0)), _const_spec(lb_logits.shape)],
        out_specs=[row(w) for w in widths],
        out_shape=[jax.ShapeDtypeStruct((n, w), F32) for w in widths],
        compiler_params=_params("parallel"),
        name="inproj",
    )(h, g, w_all, w_gd, cos_t, sin_t, lb_logits)


def _blockdiag(x, bd_mask):
    xb = x.astype(BF16)
    return jnp.concatenate([xb] * BLOCK_HEADS, axis=0) * bd_mask


def _chunk_blocks(x):
    return [x[c * CHUNK:(c + 1) * CHUNK, b * BLOCK_W:(b + 1) * BLOCK_W]
            for c in range(x.shape[0] // CHUNK) for b in range(N_BLOCKS)]


def _fold_diag_blocks(full):
    out = full[0:HEAD_DIM]
    for h in range(1, BLOCK_HEADS):
        out = out + full[h * HEAD_DIM:(h + 1) * HEAD_DIM]
    return out


def _tri_inverse(a, bd_mask, same16, same32, eye):
    bd = lambda m: [_blockdiag(x, bd_mask) for x in m]
    mm = lambda xs, ys: [_dot(x, y) for x, y in zip(xs, ys)]
    n1 = [jnp.where(same16, x, 0.0) for x in a]
    n2 = mm(n1, bd(n1))
    n4 = mm(n2, bd(n2))
    n8 = mm(n4, bd(n4))
    t = [eye + x for x in n1]
    for p in (n2, n4, n8):
        t = [x + y for x, y in zip(t, mm(t, bd(p)))]
    for blk in (jnp.logical_and(same32, jnp.logical_not(same16)), jnp.logical_not(same32)):
        ab = bd([jnp.where(blk, x, 0.0) for x in a])
        t = [x + y for x, y in zip(t, mm(mm(t, ab), bd(t)))]
    return t


def _rwkv_kernel(z_ref, mu_ref, w0_ref, w2_ref, a0_ref, a2_ref, g2_ref, kk_ref, ka_ref, rk_ref,
                 lng_ref, lnb_ref, y_ref, carry_ref, s_ref):
    tt = z_ref.shape[0]

    @pl.when(pl.program_id(1) == 0)
    def _():
        carry_ref[...] = jnp.zeros_like(carry_ref)
        s_ref[...] = jnp.zeros_like(s_ref)

    z = z_ref[...]
    zprev = _shift_rows(z, carry_ref[...], 1)
    carry_ref[...] = z[tt - CARRY_ROWS:, :]
    z = z + mu_ref[...] * (zprev - z)
    r, k, v, lora = z[:, 0:WIDTH], z[:, WIDTH:2 * WIDTH], z[:, 2 * WIDTH:3 * WIDTH], z[:, 3 * WIDTH:]

    def lora_rows(w_ref, lo):
        w = w_ref[...].astype(BF16)
        parts = [jnp.zeros((lo, WIDTH), BF16)] if lo else []
        parts.append(w)
        if lo + w.shape[0] < RWKV_LORA:
            parts.append(jnp.zeros((RWKV_LORA - lo - w.shape[0], WIDTH), BF16))
        return jnp.concatenate(parts, axis=0)

    w2p = lora_rows(w2_ref, 0)
    a2p = lora_rows(a2_ref, w2_ref.shape[0])
    g2p = lora_rows(g2_ref, w2_ref.shape[0] + a2_ref.shape[0])
    logw = -_softplus(-(w0_ref[...] + _dot(jnp.tanh(lora), w2p))) - 0.5
    ld = -jnp.exp(logw)
    a = _sigmoid(a0_ref[...] + _dot(lora, a2p))
    g = _dot(_sigmoid(lora), g2p)

    hsum = _head_sum_matrix(WIDTH)
    kk = k * kk_ref[...]
    kk = kk * lax.rsqrt(_dot_sel2(kk * kk, hsum) + 1e-12)
    k = k * (1.0 + (a - 1.0) * ka_ref[...])
    sa, sb = -kk, kk * a

    cs = _sel_dot(_chunk_cumsum_matrix(tt), ld)
    tot = jnp.concatenate([jnp.broadcast_to(cs[c * CHUNK + CHUNK - 1:(c + 1) * CHUNK, :], (CHUNK, WIDTH))
                           for c in range(tt // CHUNK)], axis=0)
    g_inc, g_exc, g_inv, g_end = jnp.exp(cs), jnp.exp(cs - ld), jnp.exp(-cs), jnp.exp(tot - cs)
    g_tot = jnp.exp(tot)
    rt, at = r * g_inc, sa * g_exc
    bt, kt = sb * g_inv, k * g_inv
    bh, kh = sb * g_end, k * g_end

    n = CHUNK
    nc = tt // n
    bd_mask = _head_sum_matrix(BLOCK_W)
    bd_mask_f = bd_mask.astype(F32)
    ri, ci = _iota((n, BLOCK_W), 0), _iota((n, BLOCK_W), 1) & (n - 1)
    strict, incl, eye = ci < ri, ci <= ri, jnp.where(ci == ri, 1.0, 0.0)
    same16 = jnp.right_shift(ri, 4) == jnp.right_shift(ci, 4)
    same32 = jnp.right_shift(ri, 5) == jnp.right_shift(ci, 5)
    bd = lambda xs: [_blockdiag(x, bd_mask) for x in xs]
    at_c, rt_c, v_c = _chunk_blocks(at), _chunk_blocks(rt), _chunk_blocks(v)

    lhs = [jnp.concatenate([x, y], axis=0) for x, y in zip(at_c, rt_c)]
    rhs = [jnp.concatenate([x, y], axis=0) for x, y in zip(bd(_chunk_blocks(bt)), bd(_chunk_blocks(kt)))]
    gm = [_dot_nt(x, y) for x, y in zip(lhs, rhs)]
    a_ab = [jnp.where(strict, x[:n, :BLOCK_W], 0.0) for x in gm]
    a_rb = [jnp.where(incl, x[n:, :BLOCK_W], 0.0) for x in gm]
    a_ak = [jnp.where(strict, x[:n, BLOCK_W:], 0.0) for x in gm]
    a_rk = [jnp.where(incl, x[n:, BLOCK_W:], 0.0) for x in gm]
    v_bd = bd(v_c)
    av = [_dot(jnp.concatenate([x, y], axis=0), z) for x, y, z in zip(a_ak, a_rk, v_bd)]
    tinv = _tri_inverse(a_ab, bd_mask, same16, same32, eye)
    w12 = [_dot(t, jnp.concatenate([x, y], axis=1)) for t, x, y in zip(tinv, bd(at_c), bd([x[:n] for x in av]))]
    w1 = [x[:, :BLOCK_W] for x in w12]
    w2 = [x[:, BLOCK_W:] for x in w12]
    q12 = [_dot(x, jnp.concatenate([y, u], axis=1)) for x, y, u in zip(a_rb, bd(w1), bd(w2))]
    q1 = [x + y[:, :BLOCK_W] for x, y in zip(rt_c, q12)]
    q2 = [x[n:] + u[:, BLOCK_W:] for x, u in zip(av, q12)]
    bh_c, kh_c = _chunk_blocks(bh), _chunk_blocks(kh)
    p1 = [_dot_tn(x, y) * bd_mask_f for x, y in zip(w1, bh_c)]
    p2 = [_fold_diag_blocks(_dot_tn(jnp.concatenate([x, u], axis=0), jnp.concatenate([y, w], axis=0)) * bd_mask_f)
          for x, u, y, w in zip(w2, v_c, bh_c, kh_c)]

    s = [s_ref[:, b * BLOCK_W:(b + 1) * BLOCK_W] for b in range(N_BLOCKS)]
    y_rows = []
    for c in range(nc):
        y_blocks = []
        for b in range(N_BLOCKS):
            i = c * N_BLOCKS + b
            y_blocks.append(_dot_nt(q1[i], _blockdiag(s[b], bd_mask)) + q2[i])
            s[b] = s[b] * g_tot[c * n:c * n + 1, b * BLOCK_W:(b + 1) * BLOCK_W] + _dot(s[b], p1[i]) + p2[i]
        y_rows.append(jnp.concatenate(y_blocks, axis=1))
    for b in range(N_BLOCKS):
        s_ref[:, b * BLOCK_W:(b + 1) * BLOCK_W] = s[b]
    y = jnp.concatenate(y_rows, axis=0)

    mean = _dot_sel2(y, hsum) * (1.0 / HEAD_DIM)
    yc = y - mean
    var = _dot_sel2(yc * yc, hsum) * (1.0 / HEAD_DIM)
    yn = yc * lax.rsqrt(var + RWKV_GN_EPS) * lng_ref[...] + lnb_ref[...]
    bonus = _dot_sel2(r * k * rk_ref[...], hsum) * v
    y_ref[...] = (yn + bonus) * g


def _rwkv(z, p, layer):
    b, t, _ = z.shape
    tt = RWKV_TILE
    args = (z, p["mu"], p["w0"], p["w2"], p["a0"], p["a2"], p["g2"], p["k_k"], p["k_a"], p["r_k"],
            p["ln_g"], p["ln_b"])
    in_specs = [pl.BlockSpec((None, tt, RWKV_IN), lambda i, j: (i, j, 0))]
    in_specs += [_layer_spec(a, layer) for a in args[1:]]
    return pl.pallas_call(
        _rwkv_kernel,
        grid=(b, t // tt),
        in_specs=in_specs,
        out_specs=pl.BlockSpec((None, tt, WIDTH), lambda i, j: (i, j, 0)),
        out_shape=jax.ShapeDtypeStruct((b, t, WIDTH), F32),
        scratch_shapes=[pltpu.VMEM((CARRY_ROWS, RWKV_IN), F32),
                        pltpu.VMEM((HEAD_DIM, WIDTH), F32)],
        compiler_params=_params("parallel", "arbitrary"),
        name="rwkv7",
    )(*args)


ATTN_BLOCK = 128
ATTN_UNITS = 16


def _attn_group(q_ref, k_ref, v_ref, og_ref, lg_ref, dil, first):
    t = q_ref.shape[0]
    nblk = t // dil // ATTN_BLOCK
    has_prev = nblk > 1
    qi, kj = _iota((ATTN_BLOCK, ATTN_BLOCK), 0), _iota((ATTN_BLOCK, ATTN_BLOCK), 1)
    cur_ok, prev_ok = kj <= qi, kj >= qi
    second = jnp.logical_not(first)

    def step(i, carry):
        rows, prevs, pmasks = [], [], []
        for x in range(ATTN_UNITS):
            u = i * ATTN_UNITS + x
            r, nb = u // nblk, u % nblk
            start = r + nb * (ATTN_BLOCK * dil)
            rows.append(pl.ds(start, ATTN_BLOCK, stride=dil))
            prevs.append(pl.ds(jnp.maximum(start - ATTN_BLOCK * dil, r), ATTN_BLOCK, stride=dil))
            pmasks.append(jnp.logical_and(prev_ok, nb > 0))
        q2 = [q_ref[rw, :] for rw in rows]
        kc = [k_ref[rw, :].astype(BF16) for rw in rows]
        vc = [v_ref[rw, :].astype(BF16) for rw in rows]
        if has_prev:
            kp = [k_ref[rw, :].astype(BF16) for rw in prevs]
            vc = [jnp.concatenate([v_ref[pw, :].astype(BF16), x], axis=0) for pw, x in zip(prevs, vc)]
        heads = []
        for sel in (first, second):
            qj = [jnp.where(sel, x, 0.0) for x in q2]
            s = [jnp.where(cur_ok, _dot_nt(x, y), NEG_INF) for x, y in zip(qj, kc)]
            if has_prev:
                sp = [jnp.where(pm, _dot_nt(x, y), NEG_INF) for pm, x, y in zip(pmasks, qj, kp)]
                s = [jnp.concatenate([x, y], axis=1) for x, y in zip(sp, s)]
            m = [jnp.max(x, axis=-1, keepdims=True) for x in s]
            e = [jnp.exp(x - y) for x, y in zip(s, m)]
            l = [jnp.sum(x, axis=-1, keepdims=True) for x in e]
            o = [_dot(x, y) / z for x, y, z in zip(e, vc, l)]
            lse = [jnp.broadcast_to(x + jnp.log(y), (ATTN_BLOCK, ATTN_OUT)) for x, y in zip(m, l)]
            heads.append((o, lse))
        for x in range(ATTN_UNITS):
            og_ref[rows[x], :] = jnp.where(first, heads[0][0][x], heads[1][0][x])
            lg_ref[rows[x], :] = jnp.where(first, heads[0][1][x], heads[1][1][x])
        return carry

    lax.fori_loop(0, t // ATTN_BLOCK // ATTN_UNITS, step, 0)


def _attn_kernel(q0, k0, v0, q1, k1, v1, q2, k2, v2, o_ref, og0, og1, og2, lg0, lg1, lg2):
    first = _iota((ATTN_BLOCK, ATTN_OUT), 1) < HEAD_DIM
    groups = ((q0, k0, v0, og0, lg0), (q1, k1, v1, og1, lg1), (q2, k2, v2, og2, lg2))
    for (win, dil), refs in zip(ATTN_GROUPS, groups):
        _attn_group(*refs, dil, first)
    l0, l1, l2 = lg0[...], lg1[...], lg2[...]
    mx = jnp.maximum(jnp.maximum(l0, l1), l2)
    w0, w1, w2 = jnp.exp(l0 - mx), jnp.exp(l1 - mx), jnp.exp(l2 - mx)
    o_ref[...] = (w0 * og0[...] + w1 * og1[...] + w2 * og2[...]) / (w0 + w1 + w2)


def _attention(zb):
    b, t, _ = zb.shape
    for win, dil in ATTN_GROUPS:
        assert win // dil == ATTN_BLOCK and t % (dil * ATTN_BLOCK) == 0 and t % (ATTN_BLOCK * ATTN_UNITS) == 0
    col = lambda blk: pl.BlockSpec((None, t, ATTN_OUT), lambda i, blk=blk: (i, 0, blk))
    ng = len(ATTN_GROUPS)
    in_specs = []
    for g in range(ng):
        in_specs += [col(g), col(ng + g), col(2 * ng + g)]
    return pl.pallas_call(
        _attn_kernel,
        grid=(b,),
        in_specs=in_specs,
        out_specs=pl.BlockSpec((None, t, ATTN_OUT), lambda i: (i, 0, 0)),
        out_shape=jax.ShapeDtypeStruct((b, t, ATTN_OUT), F32),
        scratch_shapes=[pltpu.VMEM((t, ATTN_OUT), F32)] * (2 * ng),
        compiler_params=_params("parallel"),
        name="dilated_attention",
    )(*([zb] * (3 * ng)))


def _mlstm_kernel(z_ref, cw_ref, cb_ref, ib_ref, fb_ref, y_ref, carry_ref, c_ref, m_ref):
    tt = z_ref.shape[0]

    @pl.when(pl.program_id(1) == 0)
    def _():
        carry_ref[...] = jnp.zeros_like(carry_ref)
        c_ref[...] = jnp.zeros_like(c_ref)
        m_ref[...] = jnp.zeros_like(m_ref)

    qk_in = z_ref[:, 0:2 * WIDTH]
    ext = jnp.concatenate([carry_ref[...], qk_in], axis=0)
    acc = cb_ref[...] + cw_ref[MLSTM_CONV - 1:MLSTM_CONV, :] * qk_in
    for j in range(1, MLSTM_CONV):
        acc = acc + cw_ref[MLSTM_CONV - 1 - j:MLSTM_CONV - j, :] * pltpu.roll(ext, j, axis=0)[CARRY_ROWS:]
    carry_ref[...] = qk_in[tt - CARRY_ROWS:, :]
    qk = _silu(acc)
    q, k = qk[:, :WIDTH], qk[:, WIDTH:] * (HEAD_DIM ** -0.5)
    v = z_ref[:, 2 * WIDTH:3 * WIDTH]
    og = _sigmoid(z_ref[:, 3 * WIDTH:4 * WIDTH])

    n = CHUNK
    nc = tt // n
    gates = z_ref[:, 4 * WIDTH:]
    gi = gates + ib_ref[...]
    lf = -_softplus(-(pltpu.roll(gates, LANES - GATE_LANE, axis=1) + fb_ref[...]))
    bcs = _sel_dot(_chunk_cumsum_matrix(tt), lf)
    u = gi - bcs
    row_in_chunk = _iota((tt, LANES), 0) & (n - 1)
    cmax = u
    for sh in (1, 2, 4, 8, 16, 32):
        cmax = jnp.where(row_in_chunk >= sh, jnp.maximum(cmax, pltpu.roll(cmax, sh, axis=0)), cmax)
    m_prev = m_ref[0:1, :]
    m_rows, scal_rows = [], []
    for c in range(nc):
        last = c * n + n - 1
        b_end = bcs[last:last + 1, :]
        m_new = b_end + jnp.maximum(m_prev, cmax[last:last + 1, :])
        m_rows.append(bcs[c * n:(c + 1) * n, :] + jnp.maximum(m_prev, cmax[c * n:(c + 1) * n, :]))
        scal_rows.append(jnp.concatenate([m_prev, b_end - m_new, jnp.exp(b_end + m_prev - m_new),
                                          jnp.zeros((SUBLANES - 3, LANES), F32)], axis=0))
        m_prev = m_new
    m_ref[0:1, :] = m_prev
    m_t = jnp.concatenate(m_rows, axis=0)

    expand = _onehot(_iota((LANES, WIDTH), 0) == jnp.right_shift(_iota((LANES, WIDTH), 1), HEAD_SHIFT))
    bt_x, mt_x, uc_x = _dot_sel(bcs, expand), _dot_sel(m_t, expand), _dot_sel(u, expand)
    scal_x = _dot_sel(jnp.concatenate(scal_rows, axis=0), expand)
    u_t = u.T

    bd_mask = _head_sum_matrix(BLOCK_W)
    bd_mask2 = jnp.concatenate([bd_mask, bd_mask], axis=1)
    bd_mask2_f = bd_mask2.astype(F32)
    ones = jnp.ones((n, BLOCK_W), BF16)
    ri, ci = _iota((n, BLOCK_W), 0), _iota((n, BLOCK_W), 1) & (n - 1)
    causal = ci <= ri
    q_c, k_c, v_c = _chunk_blocks(q), _chunk_blocks(k), _chunk_blocks(v)
    bt_c, mt_c, uc_c = _chunk_blocks(bt_x), _chunk_blocks(mt_x), _chunk_blocks(uc_x)
    scal = lambda c, b, row: scal_x[SUBLANES * c + row:SUBLANES * c + row + 1, b * BLOCK_W:(b + 1) * BLOCK_W]

    scores, upds = [], []
    for c in range(nc):
        for b in range(N_BLOCKS):
            i = c * N_BLOCKS + b
            u_row = jnp.concatenate([u_t[h:h + 1, c * n:(c + 1) * n]
                                     for h in range(b * BLOCK_HEADS, (b + 1) * BLOCK_HEADS)], axis=1)
            d_intra = jnp.where(causal, bt_c[i] + u_row, -jnp.inf)
            scores.append(_dot_nt(q_c[i], _blockdiag(k_c[i], bd_mask)) * jnp.exp(d_intra - mt_c[i]))
            w_k = jnp.exp(uc_c[i] + scal(c, b, 1))
            upds.append(_dot_tn(k_c[i] * w_k, jnp.concatenate([v_c[i].astype(BF16), ones], axis=1)) * bd_mask2_f)
    intra = [_dot(s, jnp.concatenate([_blockdiag(x, bd_mask), bd_mask], axis=1)) for s, x in zip(scores, v_c)]

    state = [c_ref[b] for b in range(N_BLOCKS)]
    y_rows = []
    for c in range(nc):
        y_blocks = []
        for b in range(N_BLOCKS):
            i = c * N_BLOCKS + b
            inter = _dot(q_c[i], state[b])
            w_inter = jnp.exp(bt_c[i] + scal(c, b, 0) - mt_c[i])
            num = w_inter * inter[:, :BLOCK_W] + intra[i][:, :BLOCK_W]
            den = w_inter * inter[:, BLOCK_W:] + intra[i][:, BLOCK_W:]
            y_blocks.append(num / jnp.maximum(jnp.abs(den), jnp.exp(-mt_c[i])))
            dec = scal(c, b, 2)
            state[b] = jnp.concatenate([dec, dec], axis=1) * state[b] + upds[i]
        y_rows.append(jnp.concatenate(y_blocks, axis=1))
    for b in range(N_BLOCKS):
        c_ref[b] = state[b]
    y_ref[...] = og * jnp.concatenate(y_rows, axis=0)


def _mlstm(z, p, layer):
    b, t, _ = z.shape
    tt = SEQ_TILE
    args = (z, p["conv_w"], p["conv_b"], p["i_b"], p["f_b"])
    in_specs = [pl.BlockSpec((None, tt, MLSTM_IN_PAD), lambda i, j: (i, j, 0))]
    in_specs += [_layer_spec(a, layer) for a in args[1:]]
    return pl.pallas_call(
        _mlstm_kernel,
        grid=(b, t // tt),
        in_specs=in_specs,
        out_specs=pl.BlockSpec((None, tt, WIDTH), lambda i, j: (i, j, 0)),
        out_shape=jax.ShapeDtypeStruct((b, t, WIDTH), F32),
        scratch_shapes=[pltpu.VMEM((CARRY_ROWS, 2 * WIDTH), F32),
                        pltpu.VMEM((N_BLOCKS, BLOCK_W, 2 * BLOCK_W), F32),
                        pltpu.VMEM((SUBLANES, LANES), F32)],
        compiler_params=_params("parallel", "arbitrary"),
        name="mlstm",
    )(*args)


def _hgrn_kernel(z_ref, ng_ref, y_ref, s_ref, sel_ref):
    tt = z_ref.shape[0]

    @pl.when(pl.program_id(1) == 0)
    def _():
        s_ref[...] = jnp.zeros_like(s_ref)

    q = z_ref[:, 0:WIDTH]
    k = z_ref[:, WIDTH:2 * WIDTH]
    v = z_ref[:, 2 * WIDTH:3 * WIDTH]
    og = z_ref[:, 3 * WIDTH:4 * WIDTH]
    gl = z_ref[:, 4 * WIDTH:]

    @pl.when(pl.program_id(1) == 0)
    def _():
        r, c = _iota((tt, tt), 0), _iota((tt, tt), 1)
        sel_ref[0] = _chunk_cumsum_matrix(tt)
        for lv in range(1, 7):
            sel_ref[lv] = _onehot(c == jnp.right_shift(r, lv) * (1 << lv) + (1 << (lv - 1)) - 1)

    bsum = _sel_dot(sel_ref[0], gl)
    b_end = jnp.concatenate([jnp.broadcast_to(bsum[c * CHUNK + CHUNK - 1:(c + 1) * CHUNK, :], (CHUNK, WIDTH))
                             for c in range(tt // CHUNK)], axis=0)
    q_in = q * jnp.exp(bsum)
    k_end = k * jnp.exp(b_end - bsum)
    g_tot = jnp.exp(b_end)

    n = CHUNK
    nc = tt // n
    hsum = _head_sum_matrix(WIDTH)
    bd_mask = _head_sum_matrix(BLOCK_W)
    bd_mask_f = bd_mask.astype(F32)
    ri, ci = _iota((n, BLOCK_W), 0), _iota((n, BLOCK_W), 1) & (n - 1)
    row_t = _iota((tt, 1), 0)
    bd = lambda xs: [_blockdiag(x, bd_mask) for x in xs]
    b_mids = jnp.dot(sel_ref[1:7].reshape(6 * tt, tt), bsum.astype(BF16), preferred_element_type=F32)

    def level_operands(lv):
        size, half = 1 << lv, 1 << (lv - 1)
        b_mid = b_mids[(lv - 1) * tt:lv * tt]
        right = (row_t & (size - 1)) >= half
        e = jnp.exp(jnp.where(right, bsum - b_mid, b_mid - bsum))
        return _chunk_blocks(jnp.where(right, q * e, 0.0)), bd(_chunk_blocks(jnp.where(right, 0.0, k * e)))

    nxt = level_operands(1)
    attn = [jnp.where(ri == ci, _dot_nt(x, y), 0.0) for x, y in zip(_chunk_blocks(q), bd(_chunk_blocks(k)))]
    for lv in range(1, 7):
        q_l, k_l = nxt
        if lv < 6:
            nxt = level_operands(lv + 1)
        same = jnp.right_shift(ri, lv) == jnp.right_shift(ci, lv)
        attn = [a + jnp.where(same, _dot_nt(x, y), 0.0) for a, x, y in zip(attn, q_l, k_l)]
    v_c = _chunk_blocks(v)
    intra = [_dot(a, y) for a, y in zip(attn, bd(v_c))]
    upd = [_fold_diag_blocks(_dot_tn(x, y) * bd_mask_f) for x, y in zip(v_c, _chunk_blocks(k_end))]

    s = [s_ref[:, b * BLOCK_W:(b + 1) * BLOCK_W] for b in range(N_BLOCKS)]
    y_rows = []
    q_in_c = _chunk_blocks(q_in)
    for c in range(nc):
        y_blocks = []
        for b in range(N_BLOCKS):
            i = c * N_BLOCKS + b
            y_blocks.append(_dot_nt(q_in_c[i], _blockdiag(s[b], bd_mask)) + intra[i])
            s[b] = s[b] * g_tot[c * n:c * n + 1, b * BLOCK_W:(b + 1) * BLOCK_W] + upd[i]
        y_rows.append(jnp.concatenate(y_blocks, axis=1))
    for b in range(N_BLOCKS):
        s_ref[:, b * BLOCK_W:(b + 1) * BLOCK_W] = s[b]
    o = jnp.concatenate(y_rows, axis=0)
    ms = _dot_sel2(o * o, hsum) * (1.0 / HEAD_DIM)
    y_ref[...] = o * lax.rsqrt(ms + NORM_EPS) * ng_ref[...] * og


def _hgrn(z, norm_g, layer):
    b, t, _ = z.shape
    tt = SEQ_TILE
    return pl.pallas_call(
        _hgrn_kernel,
        grid=(b, t // tt),
        in_specs=[pl.BlockSpec((None, tt, HGRN_PREP), lambda i, j: (i, j, 0)), _layer_spec(norm_g, layer)],
        out_specs=pl.BlockSpec((None, tt, WIDTH), lambda i, j: (i, j, 0)),
        out_shape=jax.ShapeDtypeStruct((b, t, WIDTH), F32),
        scratch_shapes=[pltpu.VMEM((HEAD_DIM, WIDTH), F32), pltpu.VMEM((7, tt, tt), BF16)],
        compiler_params=_params("parallel", "arbitrary"),
        name="hgrn2",
    )(z, norm_g)


def _merge_kernel(h_ref, g_ref, wg_ref, bg_ref, ya_ref, yb_ref, yc_ref, yd_ref,
                  pa_ref, pb_ref, pc_ref, pd_ref, wo_ref, o_ref):
    h = h_ref[...]
    d = h.shape[1]
    xb = _rms(h, g_ref[...]).astype(BF16)
    merged = None
    for i, (y_ref, p_ref) in enumerate(((ya_ref, pa_ref), (yb_ref, pb_ref), (yc_ref, pc_ref), (yd_ref, pd_ref))):
        gate = _sigmoid(_dot_nt(xb, wg_ref[i * d:(i + 1) * d, :]) + bg_ref[:, i * d:(i + 1) * d])
        term = gate * jnp.dot(y_ref[...].astype(BF16), p_ref[...], preferred_element_type=F32)
        merged = term if merged is None else merged + term
    o_ref[...] = h + jnp.dot(merged.astype(BF16), wo_ref[...], preferred_element_type=F32)


def _merge(h, g, wg, bg, ys, ps, wo, layer):
    n, d = h.shape
    tm = TOKEN_TILE
    row = lambda w: pl.BlockSpec((tm, w), lambda i: (i, 0))
    in_specs = [row(d), _layer_spec(g, layer), _layer_spec(wg, layer, True), _layer_spec(bg, layer)]
    in_specs += [row(y.shape[1]) for y in ys]
    in_specs += [_layer_spec(p, layer, True) for p in ps]
    in_specs += [_layer_spec(wo, layer, True)]
    return pl.pallas_call(
        _merge_kernel,
        grid=(n // tm,),
        in_specs=in_specs,
        out_specs=row(d),
        out_shape=jax.ShapeDtypeStruct((n, d), F32),
        compiler_params=_params("parallel"),
        name="merge",
    )(h, g, wg, bg, *ys, *ps, wo)


def _ffn_kernel(h_ref, g_ref, wu_ref, cw_ref, cb_ref, wd_ref, gf_ref, o_ref, carry_ref, act_ref, *, final):
    tm = h_ref.shape[0]
    dff = wd_ref.shape[0]

    @pl.when(pl.program_id(1) == 0)
    def _():
        carry_ref[...] = jnp.zeros_like(carry_ref)

    h = h_ref[...]
    xb = _rms(h, g_ref[...]).astype(BF16)

    def up_cols(lo):
        return jnp.dot(xb, wu_ref[:, lo:lo + FF_TILE], preferred_element_type=F32)

    def conv_cols(up, lo):
        ext = jnp.concatenate([carry_ref[:, lo:lo + FF_TILE], up], axis=0)
        carry_ref[:, lo:lo + FF_TILE] = up[tm - CARRY_ROWS:, :]
        out = cb_ref[:, lo:lo + FF_TILE] + cw_ref[FFN_CONV - 1:FFN_CONV, lo:lo + FF_TILE] * up
        for j in range(1, FFN_CONV):
            out = out + (cw_ref[FFN_CONV - 1 - j:FFN_CONV - j, lo:lo + FF_TILE]
                         * pltpu.roll(ext, j, axis=0)[CARRY_ROWS:])
        return out

    nslices = dff // FF_TILE
    half = (nslices // 2) * FF_TILE
    nxt = (up_cols(dff), up_cols(0))
    for ci in range(nslices):
        lo = ci * FF_TILE
        up_g, up_u = nxt
        if ci + 1 < nslices:
            nxt = (up_cols(dff + lo + FF_TILE), up_cols(lo + FF_TILE))
        act_ref[:, lo:lo + FF_TILE] = (_silu(conv_cols(up_g, dff + lo)) * conv_cols(up_u, lo)).astype(BF16)
        if lo + FF_TILE == half:
            out = h + jnp.dot(act_ref[:, :half], wd_ref[:half, :], preferred_element_type=F32)
    out = out + jnp.dot(act_ref[:, half:], wd_ref[half:, :], preferred_element_type=F32)
    o_ref[...] = _rms(out, gf_ref[...]) if final else out


def _ffn(h, g, wu, cw, cb, wd, gf, layer, final):
    b, t, d = h.shape
    tm = TOKEN_TILE
    blk = pl.BlockSpec((None, tm, d), lambda i, j: (i, j, 0))
    return pl.pallas_call(
        functools.partial(_ffn_kernel, final=final),
        grid=(b, t // tm),
        in_specs=[blk, _layer_spec(g, layer), _layer_spec(wu, layer, True), _layer_spec(cw, layer),
                  _layer_spec(cb, layer), _layer_spec(wd, layer, True), _const_spec((1, d))],
        out_specs=blk,
        out_shape=jax.ShapeDtypeStruct((b, t, d), F32),
        scratch_shapes=[pltpu.VMEM((CARRY_ROWS, wu.shape[-1]), F32), pltpu.VMEM((tm, wd.shape[-2]), BF16)],
        compiler_params=_params("parallel", "arbitrary"),
        name="convffn",
    )(h, g, wu, cw, cb, wd, gf)


def _rope_tables(seq):
    half = HEAD_DIM // 2
    inv_freq = ROPE_THETA ** (-jnp.arange(half, dtype=F32) / half)
    ang = jnp.arange(seq).astype(F32)[:, None] * inv_freq[None, :]
    cos, sin = jnp.cos(ang), jnp.sin(ang)
    return jnp.concatenate([cos, cos, cos, cos], axis=1), jnp.concatenate([-sin, sin, -sin, sin], axis=1)


def _split_w_in(w):
    nl, d, _ = w.shape
    o_c = RWKV_IN + 3 * ATTN_WIDTH
    o_g = o_c + 4 * WIDTH
    o_d = o_g + 2 * N_HEADS
    o_m = o_d + HGRN_IN
    wt = jnp.swapaxes(w, 1, 2).astype(BF16)
    pad = jnp.zeros((nl, GATE_LANE - N_HEADS, d), BF16)
    w_gd = jnp.concatenate([wt[:, o_g:o_g + N_HEADS], pad, wt[:, o_g + N_HEADS:o_d], pad, wt[:, o_d:o_m]], axis=1)
    return wt, o_g, w_gd, wt[:, o_m:]


def _lane_pad(a):
    return jnp.pad(a, ((0, 0), (0, LANES - a.shape[1]))).reshape(a.shape[0], 1, LANES)


def kernel(x, norm_mix_g, w_in, b_gate, rwkv_mu, rwkv_w0, rwkv_w2, rwkv_a0, rwkv_a2, rwkv_g2, rwkv_k_k, rwkv_k_a, rwkv_r_k, rwkv_ln_g, rwkv_ln_b, mlstm_conv_w, mlstm_conv_b, mlstm_i_b, mlstm_f_b, hgrn_lb_logits, hgrn_norm_g, p_rwkv, p_attn, p_mlstm, p_hgrn, w_out, norm_ffn_g, w_up, ffn_conv_w, ffn_conv_b, w_down, final_norm_g):
    bsz, seq, d = x.shape
    depth = w_in.shape[0]
    n = bsz * seq
    cos_t, sin_t = _rope_tables(seq)
    w_all, n_abc, w_gd, wg = _split_w_in(w_in)
    norm_mix, norm_ffn = _rows3(norm_mix_g), _rows3(norm_ffn_g)
    rwkv_p = dict(mu=_rows3(rwkv_mu), w0=_rows3(rwkv_w0), w2=rwkv_w2, a0=_rows3(rwkv_a0), a2=rwkv_a2,
                  g2=rwkv_g2, k_k=_rows3(rwkv_k_k), k_a=_rows3(rwkv_k_a), r_k=_rows3(rwkv_r_k),
                  ln_g=_rows3(rwkv_ln_g), ln_b=_rows3(rwkv_ln_b))
    mlstm_p = dict(conv_w=mlstm_conv_w, conv_b=_rows3(mlstm_conv_b), i_b=_lane_pad(mlstm_i_b),
                   f_b=_lane_pad(mlstm_f_b))
    hgrn_g = _rows3(hgrn_norm_g)
    bg = _rows3(b_gate)
    ps = [p.astype(BF16) for p in (p_rwkv, p_attn, p_mlstm, p_hgrn)]
    wo, wu, wd = w_out.astype(BF16), w_up.astype(BF16), w_down.astype(BF16)
    ffn_cb = _rows3(ffn_conv_b)
    gf = final_norm_g.reshape(1, d)
    h = x.reshape(n, d)
    for l in range(depth):
        za, zb, zc, zd = _inproj(h, norm_mix, w_all, n_abc, w_gd, cos_t, sin_t, hgrn_lb_logits, seq, l)
        ya = _rwkv(za.reshape(bsz, seq, -1), rwkv_p, l)
        yb = _attention(zb.reshape(bsz, seq, -1))
        yc = _mlstm(zc.reshape(bsz, seq, -1), mlstm_p, l)
        yd = _hgrn(zd.reshape(bsz, seq, -1), hgrn_g, l)
        ys = [y.reshape(n, -1) for y in (ya, yb, yc, yd)]
        h = _merge(h, norm_mix, wg, bg, ys, ps, wo, l)
        h = _ffn(h.reshape(bsz, seq, d), norm_ffn, wu, ffn_conv_w, ffn_cb, wd, gf, l, l == depth - 1).reshape(n, d)
    return h.reshape(bsz, seq, d)
```

```python
import functools

import jax
import jax.numpy as jnp
from jax import lax
from jax.experimental import pallas as pl
from jax.experimental.pallas import tpu as pltpu

F32 = jnp.float32
BF16 = jnp.bfloat16

LANES = 128
SUBLANES = 8
HEAD_DIM = 64
HEAD_SHIFT = 6
N_HEADS = 4
WIDTH = N_HEADS * HEAD_DIM
CHUNK = 64
BLOCK_HEADS = 2
BLOCK_W = BLOCK_HEADS * HEAD_DIM
N_BLOCKS = N_HEADS // BLOCK_HEADS
RWKV_LORA = 32 + 32 + 64
RWKV_IN = 3 * WIDTH + RWKV_LORA
RWKV_GN_EPS = 64e-5
ATTN_GROUPS = ((128, 1), (512, 4), (2048, 16))
ATTN_HEADS = 6
ATTN_WIDTH = ATTN_HEADS * HEAD_DIM
ATTN_OUT = 2 * HEAD_DIM
ROPE_THETA = 10000.0
MLSTM_CONV = 4
MLSTM_IN_PAD = 4 * WIDTH + LANES
GATE_LANE = 64
HGRN_IN = 4 * WIDTH
HGRN_PREP = 5 * WIDTH
FFN_CONV = 3
NORM_EPS = 1e-6
NEG_INF = -1e30

TOKEN_TILE = 512
MERGE_TILE = 1024
SEQ_TILE = 256
RWKV_TILE = 512
FF_TILE = 256
CARRY_ROWS = SUBLANES
VMEM_LIMIT = 56 * 1024 * 1024


def _dot(a, b):
    return jnp.dot(a.astype(BF16), b.astype(BF16), preferred_element_type=F32)


def _dot_nt(a, b):
    return lax.dot_general(a.astype(BF16), b.astype(BF16), (((1,), (1,)), ((), ())),
                           preferred_element_type=F32)


def _dot_tn(a, b):
    return lax.dot_general(a.astype(BF16), b.astype(BF16), (((0,), (0,)), ((), ())),
                           preferred_element_type=F32)


def _split3(x):
    hi = x.astype(BF16)
    r1 = x - hi.astype(F32)
    mid = r1.astype(BF16)
    lo = (r1 - mid.astype(F32)).astype(BF16)
    return hi, mid, lo


def _sel_dot(m01, x):
    hi, mid, lo = _split3(x)
    return (jnp.dot(m01, hi, preferred_element_type=F32)
            + jnp.dot(m01, mid, preferred_element_type=F32)
            + jnp.dot(m01, lo, preferred_element_type=F32))


def _dot_sel(x, m01):
    hi, mid, lo = _split3(x)
    return (jnp.dot(hi, m01, preferred_element_type=F32)
            + jnp.dot(mid, m01, preferred_element_type=F32)
            + jnp.dot(lo, m01, preferred_element_type=F32))


def _dot_sel2(x, m01):
    hi = x.astype(BF16)
    lo = (x - hi.astype(F32)).astype(BF16)
    return jnp.dot(hi, m01, preferred_element_type=F32) + jnp.dot(lo, m01, preferred_element_type=F32)


def _iota(shape, dim):
    return lax.broadcasted_iota(jnp.int32, shape, dim)


def _onehot(cond):
    return jnp.where(cond, 1.0, 0.0).astype(BF16)


def _chunk_cumsum_matrix(n):
    r, c = _iota((n, n), 0), _iota((n, n), 1)
    return _onehot((jnp.right_shift(r, HEAD_SHIFT) == jnp.right_shift(c, HEAD_SHIFT)) & (c <= r))


def _head_sum_matrix(n):
    r, c = _iota((n, n), 0), _iota((n, n), 1)
    return _onehot(jnp.right_shift(r, HEAD_SHIFT) == jnp.right_shift(c, HEAD_SHIFT))


def _shift_rows(x, carry, j):
    ext = jnp.concatenate([carry, x], axis=0)
    return pltpu.roll(ext, j, axis=0)[CARRY_ROWS:]


def _softplus(x):
    return jnp.maximum(x, 0.0) + jnp.log1p(jnp.exp(-jnp.abs(x)))


def _sigmoid(x):
    return jax.nn.sigmoid(x)


def _silu(x):
    return x * _sigmoid(x)


def _rms(x, g):
    return x * lax.rsqrt(jnp.mean(x * x, axis=-1, keepdims=True) + NORM_EPS) * g


def _const_spec(shape, single_buffer=False):
    nd = len(shape)
    if single_buffer:
        return pl.BlockSpec(shape, lambda *_: (0,) * nd, pipeline_mode=pl.Buffered(1))
    return pl.BlockSpec(shape, lambda *_: (0,) * nd)


def _layer_spec(a, layer, single_buffer=False):
    shape = (None,) + a.shape[1:]
    nd = a.ndim
    idx = lambda *_: (layer,) + (0,) * (nd - 1)
    if single_buffer:
        return pl.BlockSpec(shape, idx, pipeline_mode=pl.Buffered(1))
    return pl.BlockSpec(shape, idx)


def _rows3(a):
    return a.reshape(a.shape[0], 1, -1)


def _params(*sem):
    return pltpu.CompilerParams(dimension_semantics=sem, vmem_limit_bytes=VMEM_LIMIT)


def _inproj_kernel(x_ref, g_ref, w_ref, wgd_ref, cos_ref, sin_ref, lbl_ref, za_ref, zb_ref, zc_ref, zd_ref, *, layer):
    xb = _rms(x_ref[...], g_ref[...]).astype(BF16)

    def mm(lo, width):
        return _dot_nt(xb, w_ref[lo:lo + width, :])

    za_ref[...] = mm(0, RWKV_IN)
    o = RWKV_IN
    qk = mm(o, 2 * ATTN_WIDTH)
    w = 2 * ATTN_WIDTH
    low = (_iota(qk.shape, 1) & (HEAD_DIM - 1)) < HEAD_DIM // 2
    rot = jnp.where(low, pltpu.roll(qk, w - HEAD_DIM // 2, axis=1), pltpu.roll(qk, HEAD_DIM // 2, axis=1))
    cos = jnp.concatenate([cos_ref[...]] * (w // LANES), axis=1)
    sin = jnp.concatenate([sin_ref[...]] * (w // LANES), axis=1)
    roped = qk * cos + rot * sin
    zb_ref[:, 0:ATTN_WIDTH] = roped[:, 0:ATTN_WIDTH] * (HEAD_DIM ** -0.5)
    zb_ref[:, ATTN_WIDTH:w] = roped[:, ATTN_WIDTH:]
    zb_ref[:, w:] = mm(o + w, ATTN_WIDTH)
    o += 3 * ATTN_WIDTH
    zc_ref[:, 0:4 * WIDTH] = mm(o, 4 * WIDTH)
    gd = _dot_nt(xb, wgd_ref[...])
    zc_ref[:, 4 * WIDTH:] = gd[:, 0:LANES]
    logits = lbl_ref[...]
    pe = jnp.exp(logits - jnp.max(logits, axis=0, keepdims=True))
    pr = pe / jnp.sum(pe, axis=0, keepdims=True)
    lb = pr[0:1, :]
    for i in range(1, layer + 1):
        lb = lb + pr[i:i + 1, :]
    lb = lb - pr[0:1, :]
    zd = gd[:, LANES:]
    f = lb + (1.0 - lb) * _sigmoid(zd[:, WIDTH:2 * WIDTH])
    zd_ref[:, 0:WIDTH] = _silu(zd[:, 0:WIDTH])
    zd_ref[:, WIDTH:2 * WIDTH] = 1.0 - f
    zd_ref[:, 2 * WIDTH:3 * WIDTH] = zd[:, 2 * WIDTH:3 * WIDTH]
    zd_ref[:, 3 * WIDTH:4 * WIDTH] = _sigmoid(zd[:, 3 * WIDTH:4 * WIDTH])
    zd_ref[:, 4 * WIDTH:] = jnp.log(f)


def _inproj(h, g, w_all, n_abc, w_gd, cos_t, sin_t, lb_logits, seq, layer):
    n, d = h.shape
    tm = TOKEN_TILE
    nt = seq // tm
    widths = (RWKV_IN, 3 * ATTN_WIDTH, MLSTM_IN_PAD, HGRN_PREP)
    row = lambda w: pl.BlockSpec((tm, w), lambda i: (i, 0))
    return pl.pallas_call(
        functools.partial(_inproj_kernel, layer=layer),
        grid=(n // tm,),
        in_specs=[row(d), _layer_spec(g, layer),
                  pl.BlockSpec((None, n_abc, d), lambda i: (layer, 0, 0), pipeline_mode=pl.Buffered(1)),
                  _layer_spec(w_gd, layer, True),
                  pl.BlockSpec((tm, LANES), lambda i: (i % nt, 0)),
                  pl.BlockSpec((tm, LANES), lambda i: (i % nt, 0)), _const_spec(lb_logits.shape)],
        out_specs=[row(w) for w in widths],
        out_shape=[jax.ShapeDtypeStruct((n, w), F32) for w in widths],
        compiler_params=_params("parallel"),
        name="inproj",
    )(h, g, w_all, w_gd, cos_t, sin_t, lb_logits)


def _blockdiag(x, bd_mask):
    xb = x.astype(BF16)
    return jnp.concatenate([xb] * BLOCK_HEADS, axis=0) * bd_mask


def _chunk_blocks(x):
    return [x[c * CHUNK:(c + 1) * CHUNK, b * BLOCK_W:(b + 1) * BLOCK_W]
            for c in range(x.shape[0] // CHUNK) for b in range(N_BLOCKS)]


def _fold_diag_blocks(full):
    out = full[0:HEAD_DIM]
    for h in range(1, BLOCK_HEADS):
        out = out + full[h * HEAD_DIM:(h + 1) * HEAD_DIM]
    return out


def _tri_inverse(a, bd_mask, same16, same32, eye):
    bd = lambda m: [_blockdiag(x, bd_mask) for x in m]
    mm = lambda xs, ys: [_dot(x, y) for x, y in zip(xs, ys)]
    n1 = [jnp.where(same16, x, 0.0) for x in a]
    n2 = mm(n1, bd(n1))
    n4 = mm(n2, bd(n2))
    n8 = mm(n4, bd(n4))
    t = [eye + x for x in n1]
    for p in (n2, n4, n8):
        t = [x + y for x, y in zip(t, mm(t, bd(p)))]
    for blk in (jnp.logical_and(same32, jnp.logical_not(same16)), jnp.logical_not(same32)):
        ab = bd([jnp.where(blk, x, 0.0) for x in a])
        t = [x + y for x, y in zip(t, mm(mm(t, ab), bd(t)))]
    return t


def _rwkv_kernel(z_ref, mu_ref, w0_ref, w2_ref, a0_ref, a2_ref, g2_ref, kk_ref, ka_ref, rk_ref,
                 lng_ref, lnb_ref, y_ref, carry_ref, s_ref):
    tt = z_ref.shape[0]

    @pl.when(pl.program_id(1) == 0)
    def _():
        carry_ref[...] = jnp.zeros_like(carry_ref)
        s_ref[...] = jnp.zeros_like(s_ref)

    z = z_ref[...]
    zprev = _shift_rows(z, carry_ref[...], 1)
    carry_ref[...] = z[tt - CARRY_ROWS:, :]
    z = z + mu_ref[...] * (zprev - z)
    r, k, v, lora = z[:, 0:WIDTH], z[:, WIDTH:2 * WIDTH], z[:, 2 * WIDTH:3 * WIDTH], z[:, 3 * WIDTH:]

    def lora_rows(w_ref, lo):
        w = w_ref[...].astype(BF16)
        parts = [jnp.zeros((lo, WIDTH), BF16)] if lo else []
        parts.append(w)
        if lo + w.shape[0] < RWKV_LORA:
            parts.append(jnp.zeros((RWKV_LORA - lo - w.shape[0], WIDTH), BF16))
        return jnp.concatenate(parts, axis=0)

    w2p = lora_rows(w2_ref, 0)
    a2p = lora_rows(a2_ref, w2_ref.shape[0])
    g2p = lora_rows(g2_ref, w2_ref.shape[0] + a2_ref.shape[0])
    logw = -_softplus(-(w0_ref[...] + _dot(jnp.tanh(lora), w2p))) - 0.5
    ld = -jnp.exp(logw)
    a = _sigmoid(a0_ref[...] + _dot(lora, a2p))
    g = _dot(_sigmoid(lora), g2p)

    hsum = _head_sum_matrix(WIDTH)
    kk = k * kk_ref[...]
    kk = kk * lax.rsqrt(_dot_sel2(kk * kk, hsum) + 1e-12)
    k = k * (1.0 + (a - 1.0) * ka_ref[...])
    sa, sb = -kk, kk * a

    cs = _sel_dot(_chunk_cumsum_matrix(tt), ld)
    tot = jnp.concatenate([jnp.broadcast_to(cs[c * CHUNK + CHUNK - 1:(c + 1) * CHUNK, :], (CHUNK, WIDTH))
                           for c in range(tt // CHUNK)], axis=0)
    g_inc, g_exc, g_inv, g_end = jnp.exp(cs), jnp.exp(cs - ld), jnp.exp(-cs), jnp.exp(tot - cs)
    g_tot = jnp.exp(tot)
    rt, at = r * g_inc, sa * g_exc
    bt, kt = sb * g_inv, k * g_inv
    bh, kh = sb * g_end, k * g_end

    n = CHUNK
    nc = tt // n
    bd_mask = _head_sum_matrix(BLOCK_W)
    bd_mask_f = bd_mask.astype(F32)
    ri, ci = _iota((n, BLOCK_W), 0), _iota((n, BLOCK_W), 1) & (n - 1)
    strict, incl, eye = ci < ri, ci <= ri, jnp.where(ci == ri, 1.0, 0.0)
    same16 = jnp.right_shift(ri, 4) == jnp.right_shift(ci, 4)
    same32 = jnp.right_shift(ri, 5) == jnp.right_shift(ci, 5)
    bd = lambda xs: [_blockdiag(x, bd_mask) for x in xs]
    at_c, rt_c, v_c = _chunk_blocks(at), _chunk_blocks(rt), _chunk_blocks(v)

    lhs = [jnp.concatenate([x, y], axis=0) for x, y in zip(at_c, rt_c)]
    rhs = [jnp.concatenate([x, y], axis=0) for x, y in zip(bd(_chunk_blocks(bt)), bd(_chunk_blocks(kt)))]
    gm = [_dot_nt(x, y) for x, y in zip(lhs, rhs)]
    a_ab = [jnp.where(strict, x[:n, :BLOCK_W], 0.0) for x in gm]
    a_rb = [jnp.where(incl, x[n:, :BLOCK_W], 0.0) for x in gm]
    a_ak = [jnp.where(strict, x[:n, BLOCK_W:], 0.0) for x in gm]
    a_rk = [jnp.where(incl, x[n:, BLOCK_W:], 0.0) for x in gm]
    v_bd = bd(v_c)
    av = [_dot(jnp.concatenate([x, y], axis=0), z) for x, y, z in zip(a_ak, a_rk, v_bd)]
    tinv = _tri_inverse(a_ab, bd_mask, same16, same32, eye)
    w12 = [_dot(t, jnp.concatenate([x, y], axis=1)) for t, x, y in zip(tinv, bd(at_c), bd([x[:n] for x in av]))]
    w1 = [x[:, :BLOCK_W] for x in w12]
    w2 = [x[:, BLOCK_W:] for x in w12]
    q12 = [_dot(x, jnp.concatenate([y, u], axis=1)) for x, y, u in zip(a_rb, bd(w1), bd(w2))]
    q1 = [x + y[:, :BLOCK_W] for x, y in zip(rt_c, q12)]
    q2 = [x[n:] + u[:, BLOCK_W:] for x, u in zip(av, q12)]
    bh_c, kh_c = _chunk_blocks(bh), _chunk_blocks(kh)
    p1 = [_dot_tn(x, y) * bd_mask_f for x, y in zip(w1, bh_c)]
    p2 = [_fold_diag_blocks(_dot_tn(jnp.concatenate([x, u], axis=0), jnp.concatenate([y, w], axis=0)) * bd_mask_f)
          for x, u, y, w in zip(w2, v_c, bh_c, kh_c)]

    s = [s_ref[:, b * BLOCK_W:(b + 1) * BLOCK_W] for b in range(N_BLOCKS)]
    y_rows = []
    for c in range(nc):
        y_blocks = []
        for b in range(N_BLOCKS):
            i = c * N_BLOCKS + b
            y_blocks.append(_dot_nt(q1[i], _blockdiag(s[b], bd_mask)) + q2[i])
            s[b] = s[b] * g_tot[c * n:c * n + 1, b * BLOCK_W:(b + 1) * BLOCK_W] + _dot(s[b], p1[i]) + p2[i]
        y_rows.append(jnp.concatenate(y_blocks, axis=1))
    for b in range(N_BLOCKS):
        s_ref[:, b * BLOCK_W:(b + 1) * BLOCK_W] = s[b]
    y = jnp.concatenate(y_rows, axis=0)

    mean = _dot_sel2(y, hsum) * (1.0 / HEAD_DIM)
    yc = y - mean
    var = _dot_sel2(yc * yc, hsum) * (1.0 / HEAD_DIM)
    yn = yc * lax.rsqrt(var + RWKV_GN_EPS) * lng_ref[...] + lnb_ref[...]
    bonus = _dot_sel2(r * k * rk_ref[...], hsum) * v
    y_ref[...] = (yn + bonus) * g


def _rwkv(z, p, layer):
    b, t, _ = z.shape
    tt = RWKV_TILE
    args = (z, p["mu"], p["w0"], p["w2"], p["a0"], p["a2"], p["g2"], p["k_k"], p["k_a"], p["r_k"],
            p["ln_g"], p["ln_b"])
    in_specs = [pl.BlockSpec((None, tt, RWKV_IN), lambda i, j: (i, j, 0))]
    in_specs += [_layer_spec(a, layer) for a in args[1:]]
    return pl.pallas_call(
        _rwkv_kernel,
        grid=(b, t // tt),
        in_specs=in_specs,
        out_specs=pl.BlockSpec((None, tt, WIDTH), lambda i, j: (i, j, 0)),
        out_shape=jax.ShapeDtypeStruct((b, t, WIDTH), F32),
        scratch_shapes=[pltpu.VMEM((CARRY_ROWS, RWKV_IN), F32),
                        pltpu.VMEM((HEAD_DIM, WIDTH), F32)],
        compiler_params=_params("parallel", "arbitrary"),
        name="rwkv7",
    )(*args)


ATTN_BLOCK = 128
ATTN_UNITS = 16


def _attn_group(q_ref, k_ref, v_ref, og_ref, lg_ref, dil, first):
    t = q_ref.shape[0]
    nblk = t // dil // ATTN_BLOCK
    has_prev = nblk > 1
    qi, kj = _iota((ATTN_BLOCK, ATTN_BLOCK), 0), _iota((ATTN_BLOCK, ATTN_BLOCK), 1)
    cur_ok, prev_ok = kj <= qi, kj >= qi
    second = jnp.logical_not(first)

    def step(i, carry):
        rows, prevs, pmasks = [], [], []
        for x in range(ATTN_UNITS):
            u = i * ATTN_UNITS + x
            r, nb = u // nblk, u % nblk
            start = r + nb * (ATTN_BLOCK * dil)
            rows.append(pl.ds(start, ATTN_BLOCK, stride=dil))
            prevs.append(pl.ds(jnp.maximum(start - ATTN_BLOCK * dil, r), ATTN_BLOCK, stride=dil))
            pmasks.append(jnp.logical_and(prev_ok, nb > 0))
        q2 = [q_ref[rw, :] for rw in rows]
        kc = [k_ref[rw, :].astype(BF16) for rw in rows]
        vc = [v_ref[rw, :].astype(BF16) for rw in rows]
        if has_prev:
            kp = [k_ref[rw, :].astype(BF16) for rw in prevs]
            vc = [jnp.concatenate([v_ref[pw, :].astype(BF16), x], axis=0) for pw, x in zip(prevs, vc)]
        heads = []
        for sel in (first, second):
            qj = [jnp.where(sel, x, 0.0) for x in q2]
            s = [jnp.where(cur_ok, _dot_nt(x, y), NEG_INF) for x, y in zip(qj, kc)]
            if has_prev:
                sp = [jnp.where(pm, _dot_nt(x, y), NEG_INF) for pm, x, y in zip(pmasks, qj, kp)]
                s = [jnp.concatenate([x, y], axis=1) for x, y in zip(sp, s)]
            m = [jnp.max(x, axis=-1, keepdims=True) for x in s]
            e = [jnp.exp(x - y) for x, y in zip(s, m)]
            l = [jnp.sum(x, axis=-1, keepdims=True) for x in e]
            o = [_dot(x, y) / z for x, y, z in zip(e, vc, l)]
            lse = [jnp.broadcast_to(x + jnp.log(y), (ATTN_BLOCK, ATTN_OUT)) for x, y in zip(m, l)]
            heads.append((o, lse))
        for x in range(ATTN_UNITS):
            og_ref[rows[x], :] = jnp.where(first, heads[0][0][x], heads[1][0][x])
            lg_ref[rows[x], :] = jnp.where(first, heads[0][1][x], heads[1][1][x])
        return carry

    lax.fori_loop(0, t // ATTN_BLOCK // ATTN_UNITS, step, 0)


def _attn_kernel(q0, k0, v0, q1, k1, v1, q2, k2, v2, o_ref, og0, og1, og2, lg0, lg1, lg2):
    first = _iota((ATTN_BLOCK, ATTN_OUT), 1) < HEAD_DIM
    groups = ((q0, k0, v0, og0, lg0), (q1, k1, v1, og1, lg1), (q2, k2, v2, og2, lg2))
    for (win, dil), refs in zip(ATTN_GROUPS, groups):
        _attn_group(*refs, dil, first)
    l0, l1, l2 = lg0[...], lg1[...], lg2[...]
    mx = jnp.maximum(jnp.maximum(l0, l1), l2)
    w0, w1, w2 = jnp.exp(l0 - mx), jnp.exp(l1 - mx), jnp.exp(l2 - mx)
    o_ref[...] = (w0 * og0[...] + w1 * og1[...] + w2 * og2[...]) / (w0 + w1 + w2)


def _attention(zb):
    b, t, _ = zb.shape
    for win, dil in ATTN_GROUPS:
        assert win // dil == ATTN_BLOCK and t % (dil * ATTN_BLOCK) == 0 and t % (ATTN_BLOCK * ATTN_UNITS) == 0
    col = lambda blk: pl.BlockSpec((None, t, ATTN_OUT), lambda i, blk=blk: (i, 0, blk))
    ng = len(ATTN_GROUPS)
    in_specs = []
    for g in range(ng):
        in_specs += [col(g), col(ng + g), col(2 * ng + g)]
    return pl.pallas_call(
        _attn_kernel,
        grid=(b,),
        in_specs=in_specs,
        out_specs=pl.BlockSpec((None, t, ATTN_OUT), lambda i: (i, 0, 0)),
        out_shape=jax.ShapeDtypeStruct((b, t, ATTN_OUT), F32),
        scratch_shapes=[pltpu.VMEM((t, ATTN_OUT), F32)] * (2 * ng),
        compiler_params=_params("parallel"),
        name="dilated_attention",
    )(*([zb] * (3 * ng)))


def _mlstm_kernel(z_ref, cw_ref, cb_ref, ib_ref, fb_ref, y_ref, carry_ref, c_ref, m_ref):
    tt = z_ref.shape[0]

    @pl.when(pl.program_id(1) == 0)
    def _():
        carry_ref[...] = jnp.zeros_like(carry_ref)
        c_ref[...] = jnp.zeros_like(c_ref)
        m_ref[...] = jnp.zeros_like(m_ref)

    qk_in = z_ref[:, 0:2 * WIDTH]
    ext = jnp.concatenate([carry_ref[...], qk_in], axis=0)
    acc = cb_ref[...] + cw_ref[MLSTM_CONV - 1:MLSTM_CONV, :] * qk_in
    for j in range(1, MLSTM_CONV):
        acc = acc + cw_ref[MLSTM_CONV - 1 - j:MLSTM_CONV - j, :] * pltpu.roll(ext, j, axis=0)[CARRY_ROWS:]
    carry_ref[...] = qk_in[tt - CARRY_ROWS:, :]
    qk = _silu(acc)
    q, k = qk[:, :WIDTH], qk[:, WIDTH:] * (HEAD_DIM ** -0.5)
    v = z_ref[:, 2 * WIDTH:3 * WIDTH]
    og = _sigmoid(z_ref[:, 3 * WIDTH:4 * WIDTH])

    n = CHUNK
    nc = tt // n
    gates = z_ref[:, 4 * WIDTH:]
    gi = gates + ib_ref[...]
    lf = -_softplus(-(pltpu.roll(gates, LANES - GATE_LANE, axis=1) + fb_ref[...]))
    bcs = _sel_dot(_chunk_cumsum_matrix(tt), lf)
    u = gi - bcs
    row_in_chunk = _iota((tt, LANES), 0) & (n - 1)
    cmax = u
    for sh in (1, 2, 4, 8, 16, 32):
        cmax = jnp.where(row_in_chunk >= sh, jnp.maximum(cmax, pltpu.roll(cmax, sh, axis=0)), cmax)
    m_prev = m_ref[0:1, :]
    m_rows, scal_rows = [], []
    for c in range(nc):
        last = c * n + n - 1
        b_end = bcs[last:last + 1, :]
        m_new = b_end + jnp.maximum(m_prev, cmax[last:last + 1, :])
        m_rows.append(bcs[c * n:(c + 1) * n, :] + jnp.maximum(m_prev, cmax[c * n:(c + 1) * n, :]))
        scal_rows.append(jnp.concatenate([m_prev, b_end - m_new, jnp.exp(b_end + m_prev - m_new),
                                          jnp.zeros((SUBLANES - 3, LANES), F32)], axis=0))
        m_prev = m_new
    m_ref[0:1, :] = m_prev
    m_t = jnp.concatenate(m_rows, axis=0)

    expand = _onehot(_iota((LANES, WIDTH), 0) == jnp.right_shift(_iota((LANES, WIDTH), 1), HEAD_SHIFT))
    bt_x, mt_x, uc_x = _dot_sel(bcs, expand), _dot_sel(m_t, expand), _dot_sel(u, expand)
    scal_x = _dot_sel(jnp.concatenate(scal_rows, axis=0), expand)
    u_t = u.T

    bd_mask = _head_sum_matrix(BLOCK_W)
    bd_mask2 = jnp.concatenate([bd_mask, bd_mask], axis=1)
    bd_mask2_f = bd_mask2.astype(F32)
    ones = jnp.ones((n, BLOCK_W), BF16)
    ri, ci = _iota((n, BLOCK_W), 0), _iota((n, BLOCK_W), 1) & (n - 1)
    causal = ci <= ri
    q_c, k_c, v_c = _chunk_blocks(q), _chunk_blocks(k), _chunk_blocks(v)
    bt_c, mt_c, uc_c = _chunk_blocks(bt_x), _chunk_blocks(mt_x), _chunk_blocks(uc_x)
    scal = lambda c, b, row: scal_x[SUBLANES * c + row:SUBLANES * c + row + 1, b * BLOCK_W:(b + 1) * BLOCK_W]

    scores, upds = [], []
    for c in range(nc):
        for b in range(N_BLOCKS):
            i = c * N_BLOCKS + b
            u_row = jnp.concatenate([u_t[h:h + 1, c * n:(c + 1) * n]
                                     for h in range(b * BLOCK_HEADS, (b + 1) * BLOCK_HEADS)], axis=1)
            d_intra = jnp.where(causal, bt_c[i] + u_row, -jnp.inf)
            scores.append(_dot_nt(q_c[i], _blockdiag(k_c[i], bd_mask)) * jnp.exp(d_intra - mt_c[i]))
            w_k = jnp.exp(uc_c[i] + scal(c, b, 1))
            upds.append(_dot_tn(k_c[i] * w_k, jnp.concatenate([v_c[i].astype(BF16), ones], axis=1)) * bd_mask2_f)
    intra = [_dot(s, jnp.concatenate([_blockdiag(x, bd_mask), bd_mask], axis=1)) for s, x in zip(scores, v_c)]

    state = [c_ref[b] for b in range(N_BLOCKS)]
    y_rows = []
    for c in range(nc):
        y_blocks = []
        for b in range(N_BLOCKS):
            i = c * N_BLOCKS + b
            inter = _dot(q_c[i], state[b])
            w_inter = jnp.exp(bt_c[i] + scal(c, b, 0) - mt_c[i])
            num = w_inter * inter[:, :BLOCK_W] + intra[i][:, :BLOCK_W]
            den = w_inter * inter[:, BLOCK_W:] + intra[i][:, BLOCK_W:]
            y_blocks.append(num / jnp.maximum(jnp.abs(den), jnp.exp(-mt_c[i])))
            dec = scal(c, b, 2)
            state[b] = jnp.concatenate([dec, dec], axis=1) * state[b] + upds[i]
        y_rows.append(jnp.concatenate(y_blocks, axis=1))
    for b in range(N_BLOCKS):
        c_ref[b] = state[b]
    y_ref[...] = og * jnp.concatenate(y_rows, axis=0)


def _mlstm(z, p, layer):
    b, t, _ = z.shape
    tt = SEQ_TILE
    args = (z, p["conv_w"], p["conv_b"], p["i_b"], p["f_b"])
    in_specs = [pl.BlockSpec((None, tt, MLSTM_IN_PAD), lambda i, j: (i, j, 0))]
    in_specs += [_layer_spec(a, layer) for a in args[1:]]
    return pl.pallas_call(
        _mlstm_kernel,
        grid=(b, t // tt),
        in_specs=in_specs,
        out_specs=pl.BlockSpec((None, tt, WIDTH), lambda i, j: (i, j, 0)),
        out_shape=jax.ShapeDtypeStruct((b, t, WIDTH), F32),
        scratch_shapes=[pltpu.VMEM((CARRY_ROWS, 2 * WIDTH), F32),
                        pltpu.VMEM((N_BLOCKS, BLOCK_W, 2 * BLOCK_W), F32),
                        pltpu.VMEM((SUBLANES, LANES), F32)],
        compiler_params=_params("parallel", "arbitrary"),
        name="mlstm",
    )(*args)


def _hgrn_kernel(z_ref, ng_ref, y_ref, s_ref, sel_ref):
    tt = z_ref.shape[0]

    @pl.when(pl.program_id(1) == 0)
    def _():
        s_ref[...] = jnp.zeros_like(s_ref)

    q = z_ref[:, 0:WIDTH]
    k = z_ref[:, WIDTH:2 * WIDTH]
    v = z_ref[:, 2 * WIDTH:3 * WIDTH]
    og = z_ref[:, 3 * WIDTH:4 * WIDTH]
    gl = z_ref[:, 4 * WIDTH:]

    @pl.when(pl.program_id(1) == 0)
    def _():
        r, c = _iota((tt, tt), 0), _iota((tt, tt), 1)
        sel_ref[0] = _chunk_cumsum_matrix(tt)
        for lv in range(1, 7):
            sel_ref[lv] = _onehot(c == jnp.right_shift(r, lv) * (1 << lv) + (1 << (lv - 1)) - 1)

    bsum = _sel_dot(sel_ref[0], gl)
    b_end = jnp.concatenate([jnp.broadcast_to(bsum[c * CHUNK + CHUNK - 1:(c + 1) * CHUNK, :], (CHUNK, WIDTH))
                             for c in range(tt // CHUNK)], axis=0)
    q_in = q * jnp.exp(bsum)
    k_end = k * jnp.exp(b_end - bsum)
    g_tot = jnp.exp(b_end)

    n = CHUNK
    nc = tt // n
    hsum = _head_sum_matrix(WIDTH)
    bd_mask = _head_sum_matrix(BLOCK_W)
    bd_mask_f = bd_mask.astype(F32)
    ri, ci = _iota((n, BLOCK_W), 0), _iota((n, BLOCK_W), 1) & (n - 1)
    row_t = _iota((tt, 1), 0)
    bd = lambda xs: [_blockdiag(x, bd_mask) for x in xs]
    b_mids = jnp.dot(sel_ref[1:7].reshape(6 * tt, tt), bsum.astype(BF16), preferred_element_type=F32)

    def level_operands(lv):
        size, half = 1 << lv, 1 << (lv - 1)
        b_mid = b_mids[(lv - 1) * tt:lv * tt]
        right = (row_t & (size - 1)) >= half
        e = jnp.exp(jnp.where(right, bsum - b_mid, b_mid - bsum))
        return _chunk_blocks(jnp.where(right, q * e, 0.0)), bd(_chunk_blocks(jnp.where(right, 0.0, k * e)))

    nxt = level_operands(1)
    attn = [jnp.where(ri == ci, _dot_nt(x, y), 0.0) for x, y in zip(_chunk_blocks(q), bd(_chunk_blocks(k)))]
    for lv in range(1, 7):
        q_l, k_l = nxt
        if lv < 6:
            nxt = level_operands(lv + 1)
        same = jnp.right_shift(ri, lv) == jnp.right_shift(ci, lv)
        attn = [a + jnp.where(same, _dot_nt(x, y), 0.0) for a, x, y in zip(attn, q_l, k_l)]
    v_c = _chunk_blocks(v)
    intra = [_dot(a, y) for a, y in zip(attn, bd(v_c))]
    upd = [_fold_diag_blocks(_dot_tn(x, y) * bd_mask_f) for x, y in zip(v_c, _chunk_blocks(k_end))]

    s = [s_ref[:, b * BLOCK_W:(b + 1) * BLOCK_W] for b in range(N_BLOCKS)]
    y_rows = []
    q_in_c = _chunk_blocks(q_in)
    for c in range(nc):
        y_blocks = []
        for b in range(N_BLOCKS):
            i = c * N_BLOCKS + b
            y_blocks.append(_dot_nt(q_in_c[i], _blockdiag(s[b], bd_mask)) + intra[i])
            s[b] = s[b] * g_tot[c * n:c * n + 1, b * BLOCK_W:(b + 1) * BLOCK_W] + upd[i]
        y_rows.append(jnp.concatenate(y_blocks, axis=1))
    for b in range(N_BLOCKS):
        s_ref[:, b * BLOCK_W:(b + 1) * BLOCK_W] = s[b]
    o = jnp.concatenate(y_rows, axis=0)
    ms = _dot_sel2(o * o, hsum) * (1.0 / HEAD_DIM)
    y_ref[...] = o * lax.rsqrt(ms + NORM_EPS) * ng_ref[...] * og


def _hgrn(z, norm_g, layer):
    b, t, _ = z.shape
    tt = SEQ_TILE
    return pl.pallas_call(
        _hgrn_kernel,
        grid=(b, t // tt),
        in_specs=[pl.BlockSpec((None, tt, HGRN_PREP), lambda i, j: (i, j, 0)), _layer_spec(norm_g, layer)],
        out_specs=pl.BlockSpec((None, tt, WIDTH), lambda i, j: (i, j, 0)),
        out_shape=jax.ShapeDtypeStruct((b, t, WIDTH), F32),
        scratch_shapes=[pltpu.VMEM((HEAD_DIM, WIDTH), F32), pltpu.VMEM((7, tt, tt), BF16)],
        compiler_params=_params("parallel", "arbitrary"),
        name="hgrn2",
    )(z, norm_g)


def _merge_kernel(h_ref, g_ref, wg_ref, bg_ref, ya_ref, yb_ref, yc_ref, yd_ref,
                  pa_ref, pb_ref, pc_ref, pd_ref, wo_ref, o_ref):
    h = h_ref[...]
    d = h.shape[1]
    xb = _rms(h, g_ref[...]).astype(BF16)
    merged = None
    for i, (y_ref, p_ref) in enumerate(((ya_ref, pa_ref), (yb_ref, pb_ref), (yc_ref, pc_ref), (yd_ref, pd_ref))):
        gate = _sigmoid(_dot_nt(xb, wg_ref[i * d:(i + 1) * d, :]) + bg_ref[:, i * d:(i + 1) * d])
        term = gate * jnp.dot(y_ref[...].astype(BF16), p_ref[...], preferred_element_type=F32)
        merged = term if merged is None else merged + term
    o_ref[...] = h + jnp.dot(merged.astype(BF16), wo_ref[...], preferred_element_type=F32)


def _merge(h, g, wg, bg, ys, ps, wo, layer):
    n, d = h.shape
    tm = MERGE_TILE
    row = lambda w: pl.BlockSpec((tm, w), lambda i: (i, 0))
    in_specs = [row(d), _layer_spec(g, layer), _layer_spec(wg, layer, True), _layer_spec(bg, layer)]
    in_specs += [row(y.shape[1]) for y in ys]
    in_specs += [_layer_spec(p, layer, True) for p in ps]
    in_specs += [_layer_spec(wo, layer, True)]
    return pl.pallas_call(
        _merge_kernel,
        grid=(n // tm,),
        in_specs=in_specs,
        out_specs=row(d),
        out_shape=jax.ShapeDtypeStruct((n, d), F32),
        compiler_params=_params("parallel"),
        name="merge",
    )(h, g, wg, bg, *ys, *ps, wo)


def _ffn_kernel(h_ref, g_ref, wu_ref, cw_ref, cb_ref, wd_ref, gf_ref, o_ref, carry_ref, act_ref, *, final):
    tm = h_ref.shape[0]
    dff = wd_ref.shape[0]

    @pl.when(pl.program_id(1) == 0)
    def _():
        carry_ref[...] = jnp.zeros_like(carry_ref)

    h = h_ref[...]
    xb = _rms(h, g_ref[...]).astype(BF16)

    def up_cols(lo):
        return jnp.dot(xb, wu_ref[:, lo:lo + FF_TILE], preferred_element_type=F32)

    def conv_cols(up, lo):
        ext = jnp.concatenate([carry_ref[:, lo:lo + FF_TILE], up], axis=0)
        carry_ref[:, lo:lo + FF_TILE] = up[tm - CARRY_ROWS:, :]
        out = cb_ref[:, lo:lo + FF_TILE] + cw_ref[FFN_CONV - 1:FFN_CONV, lo:lo + FF_TILE] * up
        for j in range(1, FFN_CONV):
            out = out + (cw_ref[FFN_CONV - 1 - j:FFN_CONV - j, lo:lo + FF_TILE]
                         * pltpu.roll(ext, j, axis=0)[CARRY_ROWS:])
        return out

    nslices = dff // FF_TILE
    half = (nslices // 2) * FF_TILE
    nxt = (up_cols(dff), up_cols(0))
    for ci in range(nslices):
        lo = ci * FF_TILE
        up_g, up_u = nxt
        if ci + 1 < nslices:
            nxt = (up_cols(dff + lo + FF_TILE), up_cols(lo + FF_TILE))
        act_ref[:, lo:lo + FF_TILE] = (_silu(conv_cols(up_g, dff + lo)) * conv_cols(up_u, lo)).astype(BF16)
        if lo + FF_TILE == half:
            out = h + jnp.dot(act_ref[:, :half], wd_ref[:half, :], preferred_element_type=F32)
    out = out + jnp.dot(act_ref[:, half:], wd_ref[half:, :], preferred_element_type=F32)
    o_ref[...] = _rms(out, gf_ref[...]) if final else out


def _ffn(h, g, wu, cw, cb, wd, gf, layer, final):
    b, t, d = h.shape
    tm = TOKEN_TILE
    blk = pl.BlockSpec((None, tm, d), lambda i, j: (i, j, 0))
    return pl.pallas_call(
        functools.partial(_ffn_kernel, final=final),
        grid=(b, t // tm),
        in_specs=[blk, _layer_spec(g, layer), _layer_spec(wu, layer, True), _layer_spec(cw, layer),
                  _layer_spec(cb, layer), _layer_spec(wd, layer, True), _const_spec((1, d))],
        out_specs=blk,
        out_shape=jax.ShapeDtypeStruct((b, t, d), F32),
        scratch_shapes=[pltpu.VMEM((CARRY_ROWS, wu.shape[-1]), F32), pltpu.VMEM((tm, wd.shape[-2]), BF16)],
        compiler_params=_params("parallel", "arbitrary"),
        name="convffn",
    )(h, g, wu, cw, cb, wd, gf)


def _rope_tables(seq):
    half = HEAD_DIM // 2
    inv_freq = ROPE_THETA ** (-jnp.arange(half, dtype=F32) / half)
    ang = jnp.arange(seq).astype(F32)[:, None] * inv_freq[None, :]
    cos, sin = jnp.cos(ang), jnp.sin(ang)
    return jnp.concatenate([cos, cos, cos, cos], axis=1), jnp.concatenate([-sin, sin, -sin, sin], axis=1)


def _split_w_in(w):
    nl, d, _ = w.shape
    o_c = RWKV_IN + 3 * ATTN_WIDTH
    o_g = o_c + 4 * WIDTH
    o_d = o_g + 2 * N_HEADS
    o_m = o_d + HGRN_IN
    wt = jnp.swapaxes(w, 1, 2).astype(BF16)
    pad = jnp.zeros((nl, GATE_LANE - N_HEADS, d), BF16)
    w_gd = jnp.concatenate([wt[:, o_g:o_g + N_HEADS], pad, wt[:, o_g + N_HEADS:o_d], pad, wt[:, o_d:o_m]], axis=1)
    return wt, o_g, w_gd, wt[:, o_m:]


def _lane_pad(a):
    return jnp.pad(a, ((0, 0), (0, LANES - a.shape[1]))).reshape(a.shape[0], 1, LANES)


def kernel(x, norm_mix_g, w_in, b_gate, rwkv_mu, rwkv_w0, rwkv_w2, rwkv_a0, rwkv_a2, rwkv_g2, rwkv_k_k, rwkv_k_a, rwkv_r_k, rwkv_ln_g, rwkv_ln_b, mlstm_conv_w, mlstm_conv_b, mlstm_i_b, mlstm_f_b, hgrn_lb_logits, hgrn_norm_g, p_rwkv, p_attn, p_mlstm, p_hgrn, w_out, norm_ffn_g, w_up, ffn_conv_w, ffn_conv_b, w_down, final_norm_g):
    bsz, seq, d = x.shape
    depth = w_in.shape[0]
    n = bsz * seq
    cos_t, sin_t = _rope_tables(seq)
    w_all, n_abc, w_gd, wg = _split_w_in(w_in)
    norm_mix, norm_ffn = _rows3(norm_mix_g), _rows3(norm_ffn_g)
    rwkv_p = dict(mu=_rows3(rwkv_mu), w0=_rows3(rwkv_w0), w2=rwkv_w2, a0=_rows3(rwkv_a0), a2=rwkv_a2,
                  g2=rwkv_g2, k_k=_rows3(rwkv_k_k), k_a=_rows3(rwkv_k_a), r_k=_rows3(rwkv_r_k),
                  ln_g=_rows3(rwkv_ln_g), ln_b=_rows3(rwkv_ln_b))
    mlstm_p = dict(conv_w=mlstm_conv_w, conv_b=_rows3(mlstm_conv_b), i_b=_lane_pad(mlstm_i_b),
                   f_b=_lane_pad(mlstm_f_b))
    hgrn_g = _rows3(hgrn_norm_g)
    bg = _rows3(b_gate)
    ps = [p.astype(BF16) for p in (p_rwkv, p_attn, p_mlstm, p_hgrn)]
    wo, wu, wd = w_out.astype(BF16), w_up.astype(BF16), w_down.astype(BF16)
    ffn_cb = _rows3(ffn_conv_b)
    gf = final_norm_g.reshape(1, d)
    h = x.reshape(n, d)
    for l in range(depth):
        za, zb, zc, zd = _inproj(h, norm_mix, w_all, n_abc, w_gd, cos_t, sin_t, hgrn_lb_logits, seq, l)
        ya = _rwkv(za.reshape(bsz, seq, -1), rwkv_p, l)
        yb = _attention(zb.reshape(bsz, seq, -1))
        yc = _mlstm(zc.reshape(bsz, seq, -1), mlstm_p, l)
        yd = _hgrn(zd.reshape(bsz, seq, -1), hgrn_g, l)
        ys = [y.reshape(n, -1) for y in (ya, yb, yc, yd)]
        h = _merge(h, norm_mix, wg, bg, ys, ps, wo, l)
        h = _ffn(h.reshape(bsz, seq, d), norm_ffn, wu, ffn_conv_w, ffn_cb, wd, gf, l, l == depth - 1).reshape(n, d)
    return h.reshape(bsz, seq, d)
```
